```python
import math
import jax, jax.numpy as jnp
from jax import lax
import numpy as np

D_MODEL = 1024
BATCH = 2
SEQ = 8192
DEPTH = 2
DEC_BATCH = 128
DEC_SEQ = 8
PAST_LEN = 2048
PAGE_SIZE = 128

HEAD_DIM = 128
ATTN_HEADS = D_MODEL // (2 * HEAD_DIM)
KV_HEADS = ATTN_HEADS // 2
GROUP = ATTN_HEADS // KV_HEADS
ATTN_DIM = ATTN_HEADS * HEAD_DIM
KV_DIM = KV_HEADS * HEAD_DIM
IDX_HEADS = 8
IDX_DIM = 64
TOPK_MAX = 256
Q_BLOCK = 128
GDN_HEADS = D_MODEL // (2 * HEAD_DIM)
GDN_DK = HEAD_DIM
GDN_DV = HEAD_DIM
GDN_DIM = GDN_HEADS * GDN_DV
CONV_W = 4
CONV_DIM = GDN_HEADS * (2 * GDN_DK + GDN_DV)
GDN_CHUNK = 64
MIX_DIM = ATTN_DIM + GDN_DIM
COL_SIZES = (ATTN_DIM, KV_DIM, KV_DIM, IDX_HEADS * IDX_DIM, IDX_DIM, IDX_HEADS, CONV_DIM, GDN_DIM, GDN_HEADS, GDN_HEADS)
IN_DIM = sum(COL_SIZES)
D_FF = 11 * D_MODEL // 4
N_EXPERTS = 8
TOP_K = 2
E_FF = D_FF // 2
N_DENSE = (DEPTH + 1) // 2
N_MOE = DEPTH // 2
ROPE_THETA = 10000.0
EPS = 1e-6

kernel_name = 'hymba_dsa_gdn_adaln_decode_step'


def rms_norm(x, g):
    xf = x.astype(jnp.float32)
    y = xf * lax.rsqrt(jnp.mean(xf * xf, axis=-1, keepdims=True) + EPS)
    return (y * g.astype(jnp.float32)).astype(x.dtype)


def l2_norm(x):
    xf = x.astype(jnp.float32)
    return xf * lax.rsqrt(jnp.sum(xf * xf, axis=-1, keepdims=True) + EPS)


def rope(x, pos):
    d = x.shape[-1]
    half = d // 2
    inv = ROPE_THETA ** (-jnp.arange(half, dtype=jnp.float32) / half)
    ang = pos.astype(jnp.float32)[:, None] * inv[None, :]
    cos = jnp.cos(ang)[None, :, None, :]
    sin = jnp.sin(ang)[None, :, None, :]
    xf = x.astype(jnp.float32)
    x1, x2 = xf[..., :half], xf[..., half:]
    return jnp.concatenate([x1 * cos - x2 * sin, x2 * cos + x1 * sin], axis=-1).astype(x.dtype)


def ada_mod(c, w, b):
    m = jax.nn.silu(c) @ w + b
    return jnp.split(m[:, None, :], 6, axis=-1)


def ada_norm(x, g, shift, scale):
    return rms_norm(x, g) * (1 + scale) + shift


def split_cols(z):
    out = []
    start = 0
    for s in COL_SIZES:
        out.append(z[..., start:start + s])
        start += s
    return out


def mixer_inputs(x, shift, scale, norm_g, w_in, q_g, k_g, pos):
    B, T = x.shape[:2]
    h = ada_norm(x, norm_g, shift, scale)
    q, k, v, qi, ki, wi, gqkv, gz, ga, gb = split_cols(h @ w_in)
    q = rope(rms_norm(q.reshape(B, T, ATTN_HEADS, HEAD_DIM), q_g), pos)
    k = rope(rms_norm(k.reshape(B, T, KV_HEADS, HEAD_DIM), k_g), pos)
    v = v.reshape(B, T, KV_HEADS, HEAD_DIM)
    qi = rope(qi.reshape(B, T, IDX_HEADS, IDX_DIM), pos)
    ki = rope(ki.reshape(B, T, 1, IDX_DIM), pos).reshape(B, T, IDX_DIM)
    wi = wi * IDX_HEADS ** -0.5
    return q, k, v, qi, ki, wi, gqkv, gz, ga, gb


def indexer_select(qi, wi, ki, q_pos, n_sel):
    L = ki.shape[1]
    s = jnp.einsum('bthd,bld->bthl', qi, ki, preferred_element_type=jnp.float32) * IDX_DIM ** -0.5
    score = jnp.einsum('bthl,bth->btl', jax.nn.relu(s), wi.astype(jnp.float32))
    causal = jnp.arange(L)[None, :] <= q_pos[:, None]
    score = jnp.where(causal[None], score, -jnp.inf)
    _, idx = lax.top_k(score, n_sel)
    valid = idx <= q_pos[None, :, None]
    return idx, valid


def sparse_attend(q, k_sel, v_sel, valid):
    B, T = q.shape[:2]
    qg = q.reshape(B, T, KV_HEADS, GROUP, HEAD_DIM)
    s = jnp.einsum('bthgd,btshd->bthgs', qg, k_sel, preferred_element_type=jnp.float32) * HEAD_DIM ** -0.5
    s = jnp.where(valid[:, :, None, None, :], s, -jnp.inf)
    p = jax.nn.softmax(s, axis=-1)
    o = jnp.einsum('bthgs,btshd->bthgd', p.astype(v_sel.dtype), v_sel)
    return o.reshape(B, T, ATTN_DIM)


def take_rows(a, i):
    return jax.vmap(lambda aa, ii: aa[ii])(a, i)


def dsa_prompt(q, k, v, qi, ki, wi):
    B, S = q.shape[:2]
    n_sel = min(TOPK_MAX, S // 4)
    n_blocks = S // Q_BLOCK

    def block(i):
        start = i * Q_BLOCK
        qb = lax.dynamic_slice_in_dim(q, start, Q_BLOCK, axis=1)
        qib = lax.dynamic_slice_in_dim(qi, start, Q_BLOCK, axis=1)
        wib = lax.dynamic_slice_in_dim(wi, start, Q_BLOCK, axis=1)
        q_pos = start + jnp.arange(Q_BLOCK)
        idx, valid = indexer_select(qib, wib, ki, q_pos, n_sel)
        return sparse_attend(qb, take_rows(k, idx), take_rows(v, idx), valid)

    out = lax.map(block, jnp.arange(n_blocks))
    return out.transpose(1, 0, 2, 3).reshape(B, S, ATTN_DIM)


def dsa_sample(q, k, v, qi, ki, wi, cache_k, cache_v, cache_kidx, page_table, layer):
    DB, T = q.shape[:2]
    past = page_table.shape[1] * PAGE_SIZE
    n_sel = min(TOPK_MAX, (past + T) // 4)
    ki_past = cache_kidx[page_table, layer].reshape(DB, past, IDX_DIM)
    ki_all = jnp.concatenate([ki_past.astype(ki.dtype), ki], axis=1)
    q_pos = past + jnp.arange(T)
    idx, valid = indexer_select(qi, wi, ki_all, q_pos, n_sel)
    in_past = (idx < past)[..., None, None]
    pidx = jnp.minimum(idx, past - 1)
    phys = take_rows(page_table, pidx // PAGE_SIZE)
    off = pidx % PAGE_SIZE
    nidx = jnp.clip(idx - past, 0, T - 1)
    k_sel = jnp.where(in_past, cache_k[phys, layer, off].astype(k.dtype), take_rows(k, nidx))
    v_sel = jnp.where(in_past, cache_v[phys, layer, off].astype(v.dtype), take_rows(v, nidx))
    return sparse_attend(q, k_sel, v_sel, valid)


def gated_delta_chunked(q, k, v, g, beta, s0, chunk):
    f32 = jnp.float32
    B, T, H, DK = q.shape
    DV = v.shape[-1]
    n = T // chunk

    def blk4(t):
        return t.astype(f32).reshape(B, n, chunk, H, t.shape[-1]).transpose(1, 0, 3, 2, 4)

    def blk3(t):
        return t.astype(f32).reshape(B, n, chunk, H).transpose(1, 0, 3, 2)

    qc, kc, vc = blk4(q), blk4(k), blk4(v)
    gc = jnp.cumsum(blk3(g), axis=-1)
    bc = blk3(beta)
    incl = jnp.tril(jnp.ones((chunk, chunk), dtype=bool))
    strict = jnp.tril(jnp.ones((chunk, chunk), dtype=bool), -1)
    decay = jnp.exp(jnp.where(incl, gc[..., :, None] - gc[..., None, :], -jnp.inf))
    kb = kc * bc[..., None]
    a_mat = jnp.where(strict, jnp.einsum('nbhid,nbhjd->nbhij', kb, kc) * decay, 0.0)
    m = a_mat + jnp.eye(chunk, dtype=f32)
    u = lax.linalg.triangular_solve(m, vc * bc[..., None], left_side=True, lower=True, unit_diagonal=True)
    w = lax.linalg.triangular_solve(m, kb * jnp.exp(gc)[..., None], left_side=True, lower=True, unit_diagonal=True)
    qk = jnp.einsum('nbhid,nbhjd->nbhij', qc, kc) * decay

    def step(s, xs):
        q_i, k_i, u_i, w_i, qk_i, g_i = xs
        v_new = u_i - jnp.einsum('bhcd,bhde->bhce', w_i, s)
        o_i = (jnp.einsum('bhcd,bhde->bhce', q_i * jnp.exp(g_i)[..., None], s)
               + jnp.einsum('bhij,bhje->bhie', qk_i, v_new))
        g_last = g_i[..., -1]
        s = (s * jnp.exp(g_last)[..., None, None]
             + jnp.einsum('bhcd,bhce->bhde', k_i * jnp.exp(g_last[..., None] - g_i)[..., None], v_new))
        return s, o_i

    s, o = lax.scan(step, s0.astype(f32), (qc, kc, u, w, qk, gc))
    return o.transpose(1, 0, 3, 2, 4).reshape(B, T, H, DV), s


def gdn_mixer(qkv, z, a, b, conv_buf, s0, conv_w, a_log, dt_bias, norm_g, chunk):
    B, T = qkv.shape[:2]
    xpad = jnp.concatenate([conv_buf.astype(qkv.dtype), qkv], axis=1)
    y = xpad[:, 0:T] * conv_w[0]
    for j in range(1, CONV_W):
        y = y + xpad[:, j:j + T] * conv_w[j]
    y = jax.nn.silu(y)
    new_buf = xpad[:, T:]
    q, k, v = jnp.split(y, [GDN_HEADS * GDN_DK, 2 * GDN_HEADS * GDN_DK], axis=-1)
    q = l2_norm(q.reshape(B, T, GDN_HEADS, GDN_DK)) * GDN_DK ** -0.5
    k = l2_norm(k.reshape(B, T, GDN_HEADS, GDN_DK))
    v = v.reshape(B, T, GDN_HEADS, GDN_DV)
    beta = jax.nn.sigmoid(b.astype(jnp.float32))
    g = -jnp.exp(a_log.astype(jnp.float32)) * jax.nn.softplus(a.astype(jnp.float32) + dt_bias.astype(jnp.float32))
    o, s = gated_delta_chunked(q, k, v, g, beta, s0, chunk)
    o = rms_norm(o, norm_g) * jax.nn.silu(z.reshape(B, T, GDN_HEADS, GDN_DV).astype(jnp.float32))
    return o.reshape(B, T, GDN_DIM).astype(qkv.dtype), new_buf, s.astype(s0.dtype)


def swiglu(h, wg, wu, wd):
    return (jax.nn.silu(h @ wg) * (h @ wu)) @ wd


def moe_ffn(h, rw, rb, wg, wu, wd):
    logits = (h @ rw + rb).astype(jnp.float32)
    top_v, top_i = lax.top_k(logits, TOP_K)
    gates = jax.nn.softmax(top_v, axis=-1)
    dense_gate = jnp.sum(jax.nn.one_hot(top_i, N_EXPERTS, dtype=jnp.float32) * gates[..., None], axis=-2)
    y = jnp.zeros_like(h)
    for e in range(N_EXPERTS):
        y = y + dense_gate[..., e:e + 1].astype(h.dtype) * swiglu(h, wg[e], wu[e], wd[e])
    return y


def setup_inputs(seed: int = 0) -> dict:
    key = jax.random.key(seed)
    ks = jax.random.split(key, 32)
    f32 = jnp.float32
    n_pages = PAST_LEN // PAGE_SIZE
    n_pool = (5 * DEC_BATCH * n_pages + 3) // 4

    def nrm(k, shape, s):
        return jax.random.normal(k, shape, f32) * s

    perm = jax.random.permutation(ks[0], n_pool)[: DEC_BATCH * n_pages]
    page_table = perm.reshape(DEC_BATCH, n_pages).astype(jnp.int32)
    a_vals = jax.random.uniform(ks[1], (DEPTH, GDN_HEADS), f32, 1.0, 16.0)
    dt = jnp.exp(jax.random.uniform(ks[2], (DEPTH, GDN_HEADS), f32, math.log(1e-3), math.log(1e-1)))
    return {
        'x_prompt': nrm(ks[3], (BATCH, SEQ, D_MODEL), 1.0),
        'x_sample': nrm(ks[4], (DEC_BATCH, DEC_SEQ, D_MODEL), 1.0),
        'cache_k': nrm(ks[5], (n_pool, DEPTH, PAGE_SIZE, KV_HEADS, HEAD_DIM), 1.0),
        'cache_v': nrm(ks[6], (n_pool, DEPTH, PAGE_SIZE, KV_HEADS, HEAD_DIM), 1.0),
        'cache_kidx': nrm(ks[7], (n_pool, DEPTH, PAGE_SIZE, IDX_DIM), 1.0),
        'state_gdn': nrm(ks[8], (DEC_BATCH, DEPTH, GDN_HEADS, GDN_DK, GDN_DV), 0.1),
        'state_conv': nrm(ks[9], (DEC_BATCH, DEPTH, CONV_W - 1, CONV_DIM), 1.0),
        'page_table': page_table,
        'c_prompt': nrm(ks[10], (BATCH, D_MODEL), 1.0),
        'c_sample': nrm(ks[11], (DEC_BATCH, D_MODEL), 1.0),
        'mod_w': nrm(ks[12], (DEPTH, D_MODEL, 6 * D_MODEL), 0.5 * D_MODEL ** -0.5),
        'mod_b': nrm(ks[13], (DEPTH, 6 * D_MODEL), 0.1),
        'norm_mix_g': 1.0 + nrm(ks[14], (DEPTH, D_MODEL), 0.02),
        'norm_ffn_g': 1.0 + nrm(ks[15], (DEPTH, D_MODEL), 0.02),
        'w_in': nrm(ks[16], (DEPTH, D_MODEL, IN_DIM), D_MODEL ** -0.5),
        'q_norm_g': 1.0 + nrm(ks[17], (DEPTH, HEAD_DIM), 0.02),
        'k_norm_g': 1.0 + nrm(ks[18], (DEPTH, HEAD_DIM), 0.02),
        'conv_w': nrm(ks[19], (DEPTH, CONV_W, CONV_DIM), 0.5),
        'a_log': jnp.log(a_vals),
        'dt_bias': dt + jnp.log(-jnp.expm1(-dt)),
        'gdn_norm_g': 1.0 + nrm(ks[20], (DEPTH, GDN_DV), 0.02),
        'w_out': nrm(ks[21], (DEPTH, MIX_DIM, D_MODEL), MIX_DIM ** -0.5),
        'ffn_w_gate': nrm(ks[22], (N_DENSE, D_MODEL, D_FF), D_MODEL ** -0.5),
        'ffn_w_up': nrm(ks[23], (N_DENSE, D_MODEL, D_FF), D_MODEL ** -0.5),
        'ffn_w_down': nrm(ks[24], (N_DENSE, D_FF, D_MODEL), D_FF ** -0.5),
        'router_w': nrm(ks[25], (N_MOE, D_MODEL, N_EXPERTS), D_MODEL ** -0.5),
        'router_b': nrm(ks[26], (N_MOE, N_EXPERTS), 0.01),
        'moe_w_gate': nrm(ks[27], (N_MOE, N_EXPERTS, D_MODEL, E_FF), D_MODEL ** -0.5),
        'moe_w_up': nrm(ks[28], (N_MOE, N_EXPERTS, D_MODEL, E_FF), D_MODEL ** -0.5),
        'moe_w_down': nrm(ks[29], (N_MOE, N_EXPERTS, E_FF, D_MODEL), E_FF ** -0.5),
    }


def reference(x_prompt, x_sample, cache_k, cache_v, cache_kidx, state_gdn, state_conv, page_table,
              c_prompt, c_sample, mod_w, mod_b, norm_mix_g, norm_ffn_g, w_in, q_norm_g, k_norm_g,
              conv_w, a_log, dt_bias, gdn_norm_g, w_out, ffn_w_gate, ffn_w_up, ffn_w_down,
              router_w, router_b, moe_w_gate, moe_w_up, moe_w_down):
    bp, seq = x_prompt.shape[:2]
    dec = x_sample.shape[1]
    past = page_table.shape[1] * PAGE_SIZE
    pos_p = jnp.arange(seq)
    pos_s = past + jnp.arange(dec)
    xp, xs = x_prompt, x_sample
    kp_l, vp_l, kip_l, ks_l, vs_l, kis_l = [], [], [], [], [], []
    sp_l, ss_l, cp_l, cs_l = [], [], [], []
    for l in range(DEPTH):
        mp = ada_mod(c_prompt, mod_w[l], mod_b[l])
        ms = ada_mod(c_sample, mod_w[l], mod_b[l])

        q, k, v, qi, ki, wi, gqkv, gz, ga, gb = mixer_inputs(
            xp, mp[0], mp[1], norm_mix_g[l], w_in[l], q_norm_g[l], k_norm_g[l], pos_p)
        att = dsa_prompt(q, k, v, qi, ki, wi)
        zero_buf = jnp.zeros((bp, CONV_W - 1, CONV_DIM), xp.dtype)
        zero_s = jnp.zeros((bp, GDN_HEADS, GDN_DK, GDN_DV), state_gdn.dtype)
        gdn, conv_p, s_p = gdn_mixer(gqkv, gz, ga, gb, zero_buf, zero_s, conv_w[l], a_log[l], dt_bias[l],
                                     gdn_norm_g[l], min(GDN_CHUNK, seq))
        xp = xp + mp[2] * (jnp.concatenate([att, gdn], axis=-1) @ w_out[l])
        kp_l.append(k)
        vp_l.append(v)
        kip_l.append(ki)
        sp_l.append(s_p)
        cp_l.append(conv_p)

        q, k, v, qi, ki, wi, gqkv, gz, ga, gb = mixer_inputs(
            xs, ms[0], ms[1], norm_mix_g[l], w_in[l], q_norm_g[l], k_norm_g[l], pos_s)
        att = dsa_sample(q, k, v, qi, ki, wi, cache_k, cache_v, cache_kidx, page_table, l)
        gdn, conv_s, s_s = gdn_mixer(gqkv, gz, ga, gb, state_conv[:, l], state_gdn[:, l], conv_w[l], a_log[l],
                                     dt_bias[l], gdn_norm_g[l], dec)
        xs = xs + ms[2] * (jnp.concatenate([att, gdn], axis=-1) @ w_out[l])
        ks_l.append(k)
        vs_l.append(v)
        kis_l.append(ki)
        ss_l.append(s_s)
        cs_l.append(conv_s)

        hp = ada_norm(xp, norm_ffn_g[l], mp[3], mp[4])
        hs = ada_norm(xs, norm_ffn_g[l], ms[3], ms[4])
        i = l // 2
        if l % 2 == 0:
            fp = swiglu(hp, ffn_w_gate[i], ffn_w_up[i], ffn_w_down[i])
            fs = swiglu(hs, ffn_w_gate[i], ffn_w_up[i], ffn_w_down[i])
        else:
            fp = moe_ffn(hp, router_w[i], router_b[i], moe_w_gate[i], moe_w_up[i], moe_w_down[i])
            fs = moe_ffn(hs, router_w[i], router_b[i], moe_w_gate[i], moe_w_up[i], moe_w_down[i])
        xp = xp + mp[5] * fp
        xs = xs + ms[5] * fs

    new_k_prompt = jnp.stack(kp_l, axis=1)
    new_v_prompt = jnp.stack(vp_l, axis=1)
    new_kidx_prompt = jnp.stack(kip_l, axis=1)
    new_k_sample = jnp.stack(ks_l, axis=1)
    new_v_sample = jnp.stack(vs_l, axis=1)
    new_kidx_sample = jnp.stack(kis_l, axis=1)
    new_gdn_prompt = jnp.stack(sp_l, axis=1)
    new_gdn_sample = jnp.stack(ss_l, axis=1)
    new_conv_prompt = jnp.stack(cp_l, axis=1)
    new_conv_sample = jnp.stack(cs_l, axis=1)
    return (xp, xs, new_k_prompt, new_v_prompt, new_kidx_prompt, new_k_sample, new_v_sample, new_kidx_sample,
            new_gdn_prompt, new_gdn_sample, new_conv_prompt, new_conv_sample)
```

```python
import functools
import math

import jax
import jax.numpy as jnp
from jax import lax
from jax.experimental import pallas as pl
from jax.experimental.pallas import tpu as pltpu

F32 = jnp.float32
BF16 = jnp.bfloat16
I32 = jnp.int32

HEAD_DIM = 128
ATTN_HEADS = 4
KV_HEADS = 2
GROUP = ATTN_HEADS // KV_HEADS
IDX_HEADS = 8
IDX_DIM = 64
TOPK_MAX = 256
GDN_HEADS = 4
GDN_CHUNK = 64
CONV_W = 4
N_EXPERTS = 8
PAGE = 128
ROPE_THETA = 10000.0
EPS = 1e-6

LANES = 128
SUBLANES = 8
VMEM_LIMIT = 56 * 1024 * 1024
NEG = -1e30
INT_MIN = -2147483648
WI_LANE = 0
GA_LANE = 8
GB_LANE = 12


def _cparams(sem):
    return pltpu.CompilerParams(dimension_semantics=sem, vmem_limit_bytes=VMEM_LIMIT)


def _dot(a, b):
    return jnp.dot(a, b, preferred_element_type=F32)


def _dot_nt(a, b):
    return lax.dot_general(a, b, (((1,), (1,)), ((), ())), preferred_element_type=F32)


def _dot_tn(a, b):
    return lax.dot_general(a, b, (((0,), (0,)), ((), ())), preferred_element_type=F32)


def _split(x):
    hi = x.astype(BF16)
    lo = (x - hi.astype(F32)).astype(BF16)
    return hi, lo


def _mm3(a, b, dot=_dot):
    ah, al = _split(a)
    bh, bl = _split(b)
    return dot(ah, bh) + (dot(ah, bl) + dot(al, bh))


def _split_three(a):
    a1 = a.astype(BF16)
    r1 = a - a1.astype(F32)
    a2 = r1.astype(BF16)
    a3 = (r1 - a2.astype(F32)).astype(BF16)
    return a1, a2, a3


def _mm_exact_rhs(a, b_bf16):
    a1, a2, a3 = _split_three(a)
    return _dot(a1, b_bf16) + (_dot(a2, b_bf16) + _dot(a3, b_bf16))


def _mm_exact_lhs(m_bf16, a):
    a1, a2, a3 = _split_three(a)
    return _dot(m_bf16, a1) + (_dot(m_bf16, a2) + _dot(m_bf16, a3))


def _silu(x):
    return x * jax.nn.sigmoid(x)


def _softplus(x):
    return jnp.maximum(x, 0.0) + jnp.log1p(jnp.exp(-jnp.abs(x)))


def _mods_kernel(c_ref, w_ref, b_ref, o_ref):
    o_ref[...] = _mm3(_silu(c_ref[...]), w_ref[...]) + b_ref[...]


def _mods(c_all, mod_w, mod_b):
    n_layers, d, n6 = mod_w.shape
    rows = c_all.shape[0]
    tn = n6 // 4
    return pl.pallas_call(
        _mods_kernel,
        out_shape=jax.ShapeDtypeStruct((n_layers, rows, n6), F32),
        grid=(n_layers, n6 // tn),
        in_specs=[
            pl.BlockSpec((rows, d), lambda l, j: (0, 0)),
            pl.BlockSpec((None, d, tn), lambda l, j: (l, 0, j)),
            pl.BlockSpec((None, 1, tn), lambda l, j: (l, 0, j)),
        ],
        out_specs=pl.BlockSpec((None, rows, tn), lambda l, j: (l, 0, j)),
        compiler_params=_cparams(("arbitrary", "arbitrary")),
        name="ada_mods",
    )(c_all, mod_w, mod_b.reshape(n_layers, 1, n6))


def _rms(x):
    return x * lax.rsqrt(jnp.mean(x * x, axis=-1, keepdims=True) + EPS)


def _inproj_kernel(x_ref, sh_ref, sc_ref, g_ref, wa_ref, wih_ref, wil_ref, wg_ref, qg_ref, kg_ref,
                   cq_ref, sq_ref, ci_ref, sa_ref, sb_ref,
                   q_o, k_o, v_o, kb_o, vb_o, qi3_o, ki_o, ki3_o, small_o, gq_o, gz_o):
    x = x_ref[...]
    h = (_rms(x) * g_ref[...]) * (1.0 + sc_ref[...]) + sh_ref[...]
    tm = x.shape[0] * x.shape[1]
    h2 = h.reshape(tm, x.shape[2])
    hb, hl = _split(h2)
    za = _dot(hb, wa_ref[...])
    wih = wih_ref[...]
    zi = _dot(hb, wih) + (_dot(hb, wil_ref[...]) + _dot(hl, wih))
    zg = _dot(hb, wg_ref[...])

    cq, sq = cq_ref[...], sq_ref[...]
    qg, kg = qg_ref[...], kg_ref[...]
    for hh in range(ATTN_HEADS):
        qn = _rms(za[:, hh * HEAD_DIM:(hh + 1) * HEAD_DIM]) * qg
        qr = qn * cq + pltpu.roll(qn, HEAD_DIM // 2, 1) * sq
        q_o[:, hh * HEAD_DIM:(hh + 1) * HEAD_DIM] = (qr * (HEAD_DIM ** -0.5)).astype(BF16)
    koff = ATTN_HEADS * HEAD_DIM
    for hh in range(KV_HEADS):
        kn = _rms(za[:, koff + hh * HEAD_DIM:koff + (hh + 1) * HEAD_DIM]) * kg
        kr = kn * cq + pltpu.roll(kn, HEAD_DIM // 2, 1) * sq
        k_o[:, hh * HEAD_DIM:(hh + 1) * HEAD_DIM] = kr
        kb_o[:, hh * HEAD_DIM:(hh + 1) * HEAD_DIM] = kr.astype(BF16)
    voff = koff + KV_HEADS * HEAD_DIM
    v = za[:, voff:voff + KV_HEADS * HEAD_DIM]
    v_o[...] = v
    vb_o[...] = v.astype(BF16)

    ci, sa, sb = ci_ref[...], sa_ref[...], sb_ref[...]
    first = lax.broadcasted_iota(I32, (tm, LANES), 1) < IDX_DIM

    def rope64(t):
        return t * ci + pltpu.roll(t, LANES - IDX_DIM // 2, 1) * sa + pltpu.roll(t, IDX_DIM // 2, 1) * sb

    for hh in range(IDX_HEADS):
        r = rope64(zi[:, hh * LANES:(hh + 1) * LANES]) * (IDX_DIM ** -0.5)
        hi = r.astype(BF16).astype(F32)
        qi3_o[hh, :, 0:LANES] = jnp.where(first, hi, r - hi).astype(BF16)
        qi3_o[hh, :, LANES:2 * LANES] = jnp.where(first, hi, 0.0).astype(BF16)
    r = rope64(zi[:, IDX_HEADS * LANES:(IDX_HEADS + 1) * LANES])
    ki_o[...] = r[:, :IDX_DIM]
    hi = r.astype(BF16).astype(F32)
    ki3_o[:, 0:LANES] = hi.astype(BF16)
    ki3_o[:, LANES:2 * LANES] = jnp.where(first, r - hi, 0.0).astype(BF16)
    misc = zi[:, (IDX_HEADS + 1) * LANES:(IDX_HEADS + 2) * LANES]
    lane = lax.broadcasted_iota(I32, (tm, LANES), 1)
    small_o[...] = jnp.where(lane < IDX_HEADS, misc * (IDX_HEADS ** -0.5), misc)
    gdim = gq_o.shape[1]
    gq_o[...] = zg[:, :gdim]
    gz_o[...] = zg[:, gdim:]


def _inproj(x3, sh, sc, g, wa, wih, wil, wg, qg, kg, tabs, tile_of, tab_of, tm):
    ng, _, d = x3.shape
    rows = ng * SUBLANES
    gt = tm // SUBLANES
    nt = rows // tm
    gdim = GDN_HEADS * 3 * HEAD_DIM
    zdim = GDN_HEADS * HEAD_DIM
    const = lambda t: (0, 0)
    row = lambda t: (t, 0)
    tab = lambda t: (tab_of(t), 0)
    in_specs = [
        pl.BlockSpec((gt, SUBLANES, d), lambda t: (t, 0, 0)),
        pl.BlockSpec((gt, 1, d), lambda t: (tile_of(t), 0, 0)),
        pl.BlockSpec((gt, 1, d), lambda t: (tile_of(t), 0, 0)),
        pl.BlockSpec((1, d), const),
        pl.BlockSpec(wa.shape, const),
        pl.BlockSpec(wih.shape, const),
        pl.BlockSpec(wil.shape, const),
        pl.BlockSpec(wg.shape, const),
        pl.BlockSpec((1, HEAD_DIM), const),
        pl.BlockSpec((1, HEAD_DIM), const),
    ] + [pl.BlockSpec((tm, LANES), tab)] * 5
    kvd = KV_HEADS * HEAD_DIM
    out_shape = [
        jax.ShapeDtypeStruct((rows, ATTN_HEADS * HEAD_DIM), BF16),
        jax.ShapeDtypeStruct((rows, kvd), F32),
        jax.ShapeDtypeStruct((rows, kvd), F32),
        jax.ShapeDtypeStruct((rows, kvd), BF16),
        jax.ShapeDtypeStruct((rows, kvd), BF16),
        jax.ShapeDtypeStruct((IDX_HEADS, rows, 2 * LANES), BF16),
        jax.ShapeDtypeStruct((rows, IDX_DIM), F32),
        jax.ShapeDtypeStruct((rows, 2 * LANES), BF16),
        jax.ShapeDtypeStruct((rows, LANES), F32),
        jax.ShapeDtypeStruct((rows, gdim), F32),
        jax.ShapeDtypeStruct((rows, zdim), F32),
    ]
    out_specs = [
        pl.BlockSpec((tm, ATTN_HEADS * HEAD_DIM), row),
        pl.BlockSpec((tm, kvd), row),
        pl.BlockSpec((tm, kvd), row),
        pl.BlockSpec((tm, kvd), row),
        pl.BlockSpec((tm, kvd), row),
        pl.BlockSpec((IDX_HEADS, tm, 2 * LANES), lambda t: (0, t, 0)),
        pl.BlockSpec((tm, IDX_DIM), row),
        pl.BlockSpec((tm, 2 * LANES), row),
        pl.BlockSpec((tm, LANES), row),
        pl.BlockSpec((tm, gdim), row),
        pl.BlockSpec((tm, zdim), row),
    ]
    return pl.pallas_call(
        _inproj_kernel,
        out_shape=out_shape,
        grid=(nt,),
        in_specs=in_specs,
        out_specs=out_specs,
        compiler_params=_cparams(("arbitrary",)),
        name="in_proj",
    )(x3, sh, sc, g, wa, wih, wil, wg, qg, kg, *tabs)


def _sort_key(score):
    bits = pltpu.bitcast(score, I32)
    return jnp.where(bits < 0, bits ^ jnp.int32(0x7FFFFFFF), bits)


def _count(key_scr, nkc, kc, rows, n, preds):
    def body(c, accs):
        base = pl.multiple_of(c * kc, kc)
        accs = list(accs)
        for j in range(kc // LANES):
            kk = key_scr[:, pl.ds(base + j * LANES, LANES)]
            ps = preds(kk, base + j * LANES)
            for i in range(n):
                accs[i] = accs[i] + jnp.where(ps[i], 1.0, 0.0)
        return tuple(accs)

    accs = lax.fori_loop(0, nkc, body, tuple(jnp.zeros((rows, LANES), F32) for _ in range(n)))
    return [jnp.sum(a, axis=1, keepdims=True) for a in accs]


def _select_threshold(key_scr, nkc, kc, rows, n_sel, idx_bits):
    n_sel_f = float(n_sel)

    def bit_step(b, acc):
        cand = acc | (jnp.int32(1) << (31 - b))
        cand_s = jnp.broadcast_to(cand ^ jnp.int32(INT_MIN), (rows, LANES))
        cnt, = _count(key_scr, nkc, kc, rows, 1, lambda kk, c0: (kk >= cand_s,))
        return jnp.where(cnt >= n_sel_f, cand, acc)

    acc = lax.fori_loop(0, 32, bit_step, jnp.zeros((rows, 1), I32))
    thr = acc ^ jnp.int32(INT_MIN)
    thr_b = jnp.broadcast_to(thr, (rows, LANES))
    cnt_gt, cnt_eq = _count(key_scr, nkc, kc, rows, 2, lambda kk, c0: (kk > thr_b, kk == thr_b))
    need = n_sel_f - cnt_gt
    excess = (acc != 0) & (cnt_eq > need)
    any_excess = jnp.max(jnp.where(excess, 1.0, 0.0)) > 0.0
    lane = lax.broadcasted_iota(I32, (rows, LANES), 1)

    def tie_search():
        def idx_step(b, p):
            cand = p | (jnp.int32(1) << (idx_bits - 1 - b))
            cand_b = jnp.broadcast_to(cand, (rows, LANES))
            cnt, = _count(key_scr, nkc, kc, rows, 1,
                          lambda kk, c0: ((kk == thr_b) & ((lane + c0) < cand_b),))
            return jnp.where(cnt < need, cand, p)

        return lax.fori_loop(0, idx_bits, idx_step, jnp.zeros((rows, 1), I32))

    cut = lax.cond(any_excess, tie_search, lambda: jnp.zeros((rows, 1), I32))
    cut = jnp.where(excess, cut, jnp.int32(2147483647))
    return thr, cut


def _dsa_prompt_kernel(qi3_ref, small_ref, q_ref, ki3_ref, k_ref, v_ref, o_ref, key_scr, w_scr,
                       *, tq, kc, n_sel, idx_bits):
    i = pl.program_id(1)
    nkc = (i * tq + tq + kc - 1) // kc
    q3 = qi3_ref[...].reshape(IDX_HEADS * tq, 2 * LANES)
    wi = small_ref[:, WI_LANE:WI_LANE + IDX_HEADS]
    for h in range(IDX_HEADS):
        w_scr[h] = jnp.broadcast_to(wi[:, h:h + 1], (tq, LANES))
    row = i * tq + lax.broadcasted_iota(I32, (tq, LANES), 0)
    lane = lax.broadcasted_iota(I32, (tq, LANES), 1)

    def score_chunk(c, carry):
        base = pl.multiple_of(c * kc, kc)
        s = _dot_nt(q3, ki3_ref[pl.ds(base, kc), :])
        for j in range(kc // LANES):
            acc = None
            for h in range(IDX_HEADS):
                t = w_scr[h] * jnp.maximum(s[h * tq:(h + 1) * tq, j * LANES:(j + 1) * LANES], 0.0)
                acc = t if acc is None else acc + t
            col = lane + (base + j * LANES)
            key_scr[:, pl.ds(base + j * LANES, LANES)] = jnp.where(col <= row, _sort_key(acc),
                                                                  jnp.int32(INT_MIN))
        return carry

    lax.fori_loop(0, nkc, score_chunk, 0)
    thr, cut = _select_threshold(key_scr, nkc, kc, tq, n_sel, idx_bits)
    thr_b = jnp.broadcast_to(thr, (tq, LANES))
    cut_b = jnp.broadcast_to(cut, (tq, LANES))

    qs = []
    for g in range(KV_HEADS):
        qs.append(jnp.concatenate(
            [q_ref[:, (g * GROUP + a) * HEAD_DIM:(g * GROUP + a + 1) * HEAD_DIM] for a in range(GROUP)], axis=0))

    def attend_chunk(c, carry):
        base = pl.multiple_of(c * kc, kc)
        biases = []
        for j in range(kc // LANES):
            kk = key_scr[:, pl.ds(base + j * LANES, LANES)]
            col = lane + (base + j * LANES)
            sel = ((kk > thr_b) | ((kk == thr_b) & (col <= cut_b))) & (col <= row)
            biases.append(jnp.where(sel, 0.0, NEG))
        bias = jnp.concatenate(biases, axis=1)
        bias = jnp.concatenate([bias] * GROUP, axis=0)
        new = []
        for g in range(KV_HEADS):
            m, l, acc = carry[g]
            kg = k_ref[pl.ds(base, kc), g * HEAD_DIM:(g + 1) * HEAD_DIM]
            vg = v_ref[pl.ds(base, kc), g * HEAD_DIM:(g + 1) * HEAD_DIM]
            s = _dot_nt(qs[g], kg) + bias
            m_new = jnp.maximum(m, jnp.max(s, axis=1, keepdims=True))
            alpha = jnp.exp(m - m_new)
            p = jnp.exp(s - m_new)
            l = alpha * l + jnp.sum(p, axis=1, keepdims=True)
            acc = alpha * acc + _dot(p.astype(BF16), vg)
            new.append((m_new, l, acc))
        return tuple(new)

    init = tuple((jnp.full((GROUP * tq, 1), NEG, F32), jnp.zeros((GROUP * tq, 1), F32),
                  jnp.zeros((GROUP * tq, HEAD_DIM), F32)) for _ in range(KV_HEADS))
    res = lax.fori_loop(0, nkc, attend_chunk, init)
    for g in range(KV_HEADS):
        _, l, acc = res[g]
        o = acc / l
        for a in range(GROUP):
            hh = g * GROUP + a
            o_ref[:, hh * HEAD_DIM:(hh + 1) * HEAD_DIM] = o[a * tq:(a + 1) * tq].astype(BF16)


def _dsa_prompt(qi3, small, q_bf, ki3, k_bf, v_bf, batch, seq):
    tq = LANES
    kc = min(512, seq)
    nq = seq // tq
    n_sel = min(TOPK_MAX, seq // 4)
    idx_bits = max(1, (seq - 1).bit_length())
    kvd = KV_HEADS * HEAD_DIM
    qrow = lambda b, i: (b * nq + i, 0)
    kern = functools.partial(_dsa_prompt_kernel, tq=tq, kc=kc, n_sel=n_sel, idx_bits=idx_bits)
    return pl.pallas_call(
        kern,
        out_shape=jax.ShapeDtypeStruct((batch * seq, ATTN_HEADS * HEAD_DIM), BF16),
        grid=(batch, nq),
        in_specs=[
            pl.BlockSpec((IDX_HEADS, tq, 2 * LANES), lambda b, i: (0, b * nq + i, 0)),
            pl.BlockSpec((tq, LANES), qrow),
            pl.BlockSpec((tq, ATTN_HEADS * HEAD_DIM), qrow),
            pl.BlockSpec((seq, 2 * LANES), lambda b, i: (b, 0)),
            pl.BlockSpec((seq, kvd), lambda b, i: (b, 0)),
            pl.BlockSpec((seq, kvd), lambda b, i: (b, 0)),
        ],
        out_specs=pl.BlockSpec((tq, ATTN_HEADS * HEAD_DIM), qrow),
        scratch_shapes=[pltpu.VMEM((tq, seq), I32), pltpu.VMEM((IDX_HEADS, tq, LANES), F32)],
        compiler_params=_cparams(("arbitrary", "arbitrary")),
        name="dsa_prompt",
    )(qi3, small, q_bf, ki3, k_bf, v_bf)


SEQ_PER_STEP = 2


def _sample_scores_kernel(pt_ref, qi3_ref, small_ref, ki3n_ref, *rest, n_pages, t_len):
    pages = rest[:SEQ_PER_STEP * n_pages]
    s_o = rest[SEQ_PER_STEP * n_pages]
    nr = SEQ_PER_STEP * t_len
    q3 = qi3_ref[...].reshape(IDX_HEADS * nr, 2 * LANES)
    wi = small_ref[:, WI_LANE:WI_LANE + IDX_HEADS]
    first = lax.broadcasted_iota(I32, (PAGE, LANES), 1) < IDX_DIM
    knew =jnp.concatenate([ki3n_ref[...], jnp.zeros((PAGE - nr, 2 * LANES), BF16)], axis=0)
    lane = lax.broadcasted_iota(I32, (t_len, LANES), 1)
    trow = lax.broadcasted_iota(I32, (t_len, LANES), 0)

    def head_sum(s, j):
        acc = None
        for h in range(IDX_HEADS):
            r0 = h * nr + j * t_len
            t = wi[j * t_len:(j + 1) * t_len, h:h + 1] * jnp.maximum(s[r0:r0 + t_len], 0.0)
            acc = t if acc is None else acc + t
        return acc

    for j in range(SEQ_PER_STEP):
        for p in range(n_pages):
            kp = pages[j * n_pages + p][...]
            kp2 = jnp.concatenate([kp, kp], axis=1)
            hi = kp2.astype(BF16).astype(F32)
            k3 = jnp.concatenate([hi, jnp.where(first, kp2 - hi, 0.0)], axis=1).astype(BF16)
            s_o[j, :, p * PAGE:(p + 1) * PAGE] = head_sum(_dot_nt(q3, k3), j)
        sn = head_sum(_dot_nt(q3, knew), j)
        ok = (lane >= j * t_len) & (lane - j * t_len <= trow)
        s_o[j, :, n_pages * PAGE:(n_pages + 1) * PAGE] = jnp.where(ok, sn, -jnp.inf)


def _sample_scores(page_table, qi3, small, ki3, cache_kidx, layer, rows_p, t_len):
    db, n_pages = page_table.shape
    nr = SEQ_PER_STEP * t_len
    base = rows_p // nr
    lp = (n_pages + 1) * PAGE
    in_specs = [
        pl.BlockSpec((IDX_HEADS, nr, 2 * LANES), lambda n, pt: (0, base + n, 0)),
        pl.BlockSpec((nr, LANES), lambda n, pt: (base + n, 0)),
        pl.BlockSpec((nr, 2 * LANES), lambda n, pt: (base + n, 0)),
    ]
    for j in range(SEQ_PER_STEP):
        for p in range(n_pages):
            in_specs.append(pl.BlockSpec(
                (None, None, PAGE, IDX_DIM),
                lambda n, pt, j=j, p=p: (pt[n * SEQ_PER_STEP + j, p], layer, 0, 0)))
    kern = functools.partial(_sample_scores_kernel, n_pages=n_pages, t_len=t_len)
    return pl.pallas_call(
        kern,
        out_shape=jax.ShapeDtypeStruct((db, t_len, lp), F32),
        grid_spec=pltpu.PrefetchScalarGridSpec(
            num_scalar_prefetch=1,
            grid=(db // SEQ_PER_STEP,),
            in_specs=in_specs,
            out_specs=pl.BlockSpec((SEQ_PER_STEP, t_len, lp), lambda n, pt: (n, 0, 0)),
        ),
        compiler_params=_cparams(("arbitrary",)),
        name="dsa_sample_scores",
    )(page_table, qi3, small, ki3, *([cache_kidx] * (SEQ_PER_STEP * n_pages)))


def _sample_select_kernel(s_ref, b_ref, key_scr, *, n_sel, idx_bits):
    rows, lp = s_ref.shape
    nkc = lp // LANES
    for c in range(nkc):
        sc = s_ref[:, c * LANES:(c + 1) * LANES]
        key_scr[:, c * LANES:(c + 1) * LANES] = jnp.where(sc > -jnp.inf, _sort_key(sc), jnp.int32(INT_MIN))
    thr, cut = _select_threshold(key_scr, nkc, LANES, rows, n_sel, idx_bits)
    thr_b = jnp.broadcast_to(thr, (rows, LANES))
    cut_b = jnp.broadcast_to(cut, (rows, LANES))
    lane = lax.broadcasted_iota(I32, (rows, LANES), 1)
    for c in range(nkc):
        kk = key_scr[:, c * LANES:(c + 1) * LANES]
        valid = s_ref[:, c * LANES:(c + 1) * LANES] > -jnp.inf
        sel = ((kk > thr_b) | ((kk == thr_b) & ((lane + c * LANES) <= cut_b))) & valid
        b_ref[:, c * LANES:(c + 1) * LANES] = jnp.where(sel, 0.0, NEG)


def _sample_select(scores2d, n_sel):
    rows, lp = scores2d.shape
    tr = min(LANES, rows)
    kern = functools.partial(_sample_select_kernel, n_sel=n_sel, idx_bits=max(1, (lp - 1).bit_length()))
    return pl.pallas_call(
        kern,
        out_shape=jax.ShapeDtypeStruct((rows, lp), F32),
        grid=(rows // tr,),
        in_specs=[pl.BlockSpec((tr, lp), lambda r: (r, 0))],
        out_specs=pl.BlockSpec((tr, lp), lambda r: (r, 0)),
        scratch_shapes=[pltpu.VMEM((tr, lp), I32)],
        compiler_params=_cparams(("arbitrary",)),
        name="dsa_sample_select",
    )(scores2d)


def _sample_attend_kernel(pt_ref, q_ref, bias_ref, kn_ref, vn_ref, *rest, n_pages, t_len):
    npg = SEQ_PER_STEP * n_pages
    kpages, vpages = rest[:npg], rest[npg:2 * npg]
    o_ref = rest[2 * npg]
    kc_scr, vc_scr = rest[2 * npg + 1], rest[2 * npg + 2]
    nr = SEQ_PER_STEP * t_len
    lp = (n_pages + 1) * PAGE
    kvd = KV_HEADS * HEAD_DIM
    bias = jnp.concatenate([bias_ref[...]] * GROUP, axis=0)
    pad = jnp.zeros((PAGE - nr, kvd), BF16)
    kc_scr[n_pages * PAGE:lp, :] = jnp.concatenate([kn_ref[...], pad], axis=0)
    vc_scr[n_pages * PAGE:lp, :] = jnp.concatenate([vn_ref[...], pad], axis=0)
    for j in range(SEQ_PER_STEP):
        for p in range(n_pages):
            kc_scr[p * PAGE:(p + 1) * PAGE, :] = kpages[j * n_pages + p][...].astype(BF16)
            vc_scr[p * PAGE:(p + 1) * PAGE, :] = vpages[j * n_pages + p][...].astype(BF16)
        for g in range(KV_HEADS):
            qs = jnp.concatenate(
                [q_ref[:, (g * GROUP + a) * HEAD_DIM:(g * GROUP + a + 1) * HEAD_DIM] for a in range(GROUP)],
                axis=0)
            s = _dot_nt(qs, kc_scr[:, g * HEAD_DIM:(g + 1) * HEAD_DIM]) + bias
            m = jnp.max(s, axis=1, keepdims=True)
            p_ = jnp.exp(s - m)
            l = jnp.sum(p_, axis=1, keepdims=True)
            o = _dot(p_.astype(BF16), vc_scr[:, g * HEAD_DIM:(g + 1) * HEAD_DIM]) / l
            for a in range(GROUP):
                hh = g * GROUP + a
                r0 = a * nr + j * t_len
                o_ref[j * t_len:(j + 1) * t_len, hh * HEAD_DIM:(hh + 1) * HEAD_DIM] = (
                    o[r0:r0 + t_len].astype(BF16))


def _sample_attend(page_table, q_bf, bias2d, k_bf, v_bf, cache_k4, cache_v4, layer, rows_p, t_len):
    db, n_pages = page_table.shape
    nr = SEQ_PER_STEP * t_len
    base = rows_p // nr
    lp = (n_pages + 1) * PAGE
    kvd = KV_HEADS * HEAD_DIM
    in_specs = [
        pl.BlockSpec((nr, ATTN_HEADS * HEAD_DIM), lambda n, pt: (base + n, 0)),
        pl.BlockSpec((nr, lp), lambda n, pt: (n, 0)),
        pl.BlockSpec((nr, kvd), lambda n, pt: (base + n, 0)),
        pl.BlockSpec((nr, kvd), lambda n, pt: (base + n, 0)),
    ]
    for _ in range(2):
        for j in range(SEQ_PER_STEP):
            for p in range(n_pages):
                in_specs.append(pl.BlockSpec(
                    (None, None, PAGE, kvd),
                    lambda n, pt, j=j, p=p: (pt[n * SEQ_PER_STEP + j, p], layer, 0, 0)))
    kern = functools.partial(_sample_attend_kernel, n_pages=n_pages, t_len=t_len)
    npg = SEQ_PER_STEP * n_pages
    return pl.pallas_call(
        kern,
        out_shape=jax.ShapeDtypeStruct((db * t_len, ATTN_HEADS * HEAD_DIM), BF16),
        grid_spec=pltpu.PrefetchScalarGridSpec(
            num_scalar_prefetch=1,
            grid=(db // SEQ_PER_STEP,),
            in_specs=in_specs,
            out_specs=pl.BlockSpec((nr, ATTN_HEADS * HEAD_DIM), lambda n, pt: (n, 0)),
            scratch_shapes=[pltpu.VMEM((lp, kvd), BF16), pltpu.VMEM((lp, kvd), BF16)],
        ),
        compiler_params=_cparams(("arbitrary",)),
        name="dsa_sample_attend",
    )(page_table, q_bf, bias2d, k_bf, v_bf, *([cache_k4] * npg), *([cache_v4] * npg))


def _chunk_masks(n, chunk):
    ri = lax.broadcasted_iota(I32, (n, n), 0)
    ci = lax.broadcasted_iota(I32, (n, n), 1)

    def same(size):
        sh = size.bit_length() - 1
        return (ri >> sh) == (ci >> sh)

    same_c = same(chunk)
    incl = same_c & (ci <= ri)
    strict = same_c & (ci < ri)
    base = min(SUBLANES, chunk)
    levels = []
    s = base
    while s < chunk:
        levels.append(same(2 * s) & jnp.logical_not(same(s)))
        s *= 2
    return incl, strict, same(base), levels, (ri == ci)


def _unit_lower_inverse(a, same_base, levels, eye):
    ident = jnp.where(eye, 1.0, 0.0)
    ad = jnp.where(same_base, a, 0.0)
    a2 = _mm3(ad, ad)
    a4 = _mm3(a2, a2)
    t = _mm3(_mm3(ident - ad, ident + a2), ident + a4)
    for lv in levels:
        off = jnp.where(lv, a, 0.0)
        t = t - _mm3(_mm3(t, off), t)
    return t


def _gdn_intra(q, k, v, beta_c, gc_c, gc_r, masks):
    incl, strict, same_base, levels, eye = masks
    decay = jnp.exp(jnp.where(incl, gc_c - gc_r, -jnp.inf))
    kb = k * beta_c
    a = jnp.where(strict, _mm3(kb, k, _dot_nt) * decay, 0.0)
    t = _unit_lower_inverse(a, same_base, levels, eye)
    uw = _mm3(t, jnp.concatenate([v * beta_c, kb * jnp.exp(gc_c)], axis=1))
    qk = _mm3(q, k, _dot_nt) * decay
    return uw[:, :HEAD_DIM], uw[:, HEAD_DIM:], qk


def _l2(x):
    return x * lax.rsqrt(jnp.sum(x * x, axis=-1, keepdims=True) + EPS)


def _cum_matrices(n, chunk):
    ri = lax.broadcasted_iota(I32, (n, n), 0)
    ci = lax.broadcasted_iota(I32, (n, n), 1)
    sh = chunk.bit_length() - 1
    same = (ri >> sh) == (ci >> sh)
    lower = jnp.where(same & (ci <= ri), 1.0, 0.0).astype(BF16)
    upper = jnp.where(same & (ri <= ci), 1.0, 0.0).astype(BF16)
    return lower, upper


def _gdn_gates(sm, smt, alog_l, dtb_l, alog_c, dtb_c, chunk):
    n = sm.shape[0]
    lower, upper = _cum_matrices(n, chunk)
    g_tile = -jnp.exp(alog_l) * _softplus(sm + dtb_l)
    beta_tile = jax.nn.sigmoid(sm)
    gc_cols = _mm_exact_lhs(lower, g_tile)
    g_rows = -jnp.exp(alog_c) * _softplus(smt + dtb_c)
    gc_rows = _mm_exact_rhs(g_rows, upper)
    return beta_tile, gc_cols, gc_rows


def _gdn_prompt_kernel(gq_ref, gz_ref, sm_ref, smt_ref, cw_ref, alog_l, dtb_l, alog_c, dtb_c, ng_ref,
                       o_ref, s_o_ref, stage, s_scr, *, tt, chunk):
    t_idx = pl.program_id(1)
    hd = HEAD_DIM
    nh = GDN_HEADS

    @pl.when(t_idx == 0)
    def _():
        stage[0:SUBLANES, :] = jnp.zeros((SUBLANES, stage.shape[1]), F32)
        s_scr[...] = jnp.zeros(s_scr.shape, F32)

    x = gq_ref[...]
    stage[SUBLANES:SUBLANES + tt, :] = x
    y = None
    for j in range(CONV_W):
        term = stage[pl.ds(SUBLANES - (CONV_W - 1) + j, tt), :] * cw_ref[j:j + 1, :]
        y = term if y is None else y + term
    stage[0:SUBLANES, :] = x[tt - SUBLANES:tt, :]
    y = _silu(y)

    beta_tile, gc_cols, gc_rows = _gdn_gates(sm_ref[...], smt_ref[...], alog_l[...], dtb_l[...],
                                             alog_c[...], dtb_c[...], chunk)
    masks = _chunk_masks(tt, chunk)
    incl = masks[0]
    ng = ng_ref[...]
    nchunks = tt // chunk
    for h in range(nh):
        q = _l2(y[:, h * hd:(h + 1) * hd]) * (hd ** -0.5)
        k = _l2(y[:, (nh + h) * hd:(nh + h + 1) * hd])
        v = y[:, (2 * nh + h) * hd:(2 * nh + h + 1) * hd]
        beta_c = beta_tile[:, GB_LANE + h:GB_LANE + h + 1]
        gc_c = gc_cols[:, GA_LANE + h:GA_LANE + h + 1]
        gc_r = gc_rows[h:h + 1, :]
        u, w, qk = _gdn_intra(q, k, v, beta_c, gc_c, gc_r, masks)
        qg = q * jnp.exp(gc_c)
        s = s_scr[h]
        vnew, ointer = [], []
        for c in range(nchunks):
            r = slice(c * chunk, (c + 1) * chunk)
            vn = u[r] - _mm3(w[r], s)
            ointer.append(_mm3(qg[r], s))
            g_last = gc_c[(c + 1) * chunk - 1:(c + 1) * chunk, :]
            kdec = k[r] * jnp.exp(g_last - gc_c[r])
            s = s * jnp.exp(g_last) + _mm3(kdec, vn, _dot_tn)
            vnew.append(vn)
        s_scr[h] = s
        o = jnp.concatenate(ointer, axis=0) + _mm3(jnp.where(incl, qk, 0.0), jnp.concatenate(vnew, axis=0))
        o = _rms(o) * ng * _silu(gz_ref[:, h * hd:(h + 1) * hd])
        o_ref[:, h * hd:(h + 1) * hd] = o.astype(BF16)

    @pl.when(t_idx == pl.num_programs(1) - 1)
    def _():
        s_o_ref[...] = s_scr[...]


def _gdn_prompt(gq, gz, small, small_t, conv_w_l, alog_l, dtb_l, alog_c, dtb_c, ng, batch, seq):
    tt = min(256, seq)
    chunk = min(GDN_CHUNK, seq)
    nt = seq // tt
    gdim = gq.shape[1]
    zdim = gz.shape[1]
    row = lambda b, t: (b * nt + t, 0)
    const = lambda b, t: (0, 0)
    kern = functools.partial(_gdn_prompt_kernel, tt=tt, chunk=chunk)
    return pl.pallas_call(
        kern,
        out_shape=[jax.ShapeDtypeStruct((batch * seq, zdim), BF16),
                   jax.ShapeDtypeStruct((batch, GDN_HEADS, HEAD_DIM, HEAD_DIM), F32)],
        grid=(batch, nt),
        in_specs=[
            pl.BlockSpec((tt, gdim), row),
            pl.BlockSpec((tt, zdim), row),
            pl.BlockSpec((tt, LANES), row),
            pl.BlockSpec((SUBLANES, tt), lambda b, t: (0, b * nt + t)),
            pl.BlockSpec((CONV_W, gdim), const),
            pl.BlockSpec((1, LANES), const),
            pl.BlockSpec((1, LANES), const),
            pl.BlockSpec((SUBLANES, 1), const),
            pl.BlockSpec((SUBLANES, 1), const),
            pl.BlockSpec((1, HEAD_DIM), const),
        ],
        out_specs=[pl.BlockSpec((tt, zdim), row),
                   pl.BlockSpec((None, GDN_HEADS, HEAD_DIM, HEAD_DIM), lambda b, t: (b, 0, 0, 0))],
        scratch_shapes=[pltpu.VMEM((tt + SUBLANES, gdim), F32),
                        pltpu.VMEM((GDN_HEADS, HEAD_DIM, HEAD_DIM), F32)],
        compiler_params=_cparams(("arbitrary", "arbitrary")),
        name="gdn_prompt",
    )(gq, gz, small, small_t, conv_w_l, alog_l, dtb_l, alog_c, dtb_c, ng)


def _gdn_sample_kernel(gq_ref, gz_ref, sm_ref, smt_ref, cst_ref, s0_ref, cw_ref, alog_l, dtb_l, alog_c,
                       dtb_c, ng_ref, o_ref, s_o_ref, stage, uw_scr, vn_scr, oi_scr, *, nb, t_len):
    hd = HEAD_DIM
    nh = GDN_HEADS
    n = nb * t_len
    gdim = gq_ref.shape[1]
    stage[:, 0:SUBLANES, :] = cst_ref[...].reshape(nb, SUBLANES, gdim)
    stage[:, SUBLANES:SUBLANES + t_len, :] = gq_ref[...].reshape(nb, t_len, gdim)
    y = None
    for j in range(CONV_W):
        term = stage[:, pl.ds(SUBLANES - (CONV_W - 1) + j, t_len), :] * cw_ref[j:j + 1, :]
        y = term if y is None else y + term
    y = _silu(y).reshape(n, gdim)

    beta_tile, gc_cols, gc_rows = _gdn_gates(sm_ref[...], smt_ref[...], alog_l[...], dtb_l[...],
                                             alog_c[...], dtb_c[...], t_len)
    masks = _chunk_masks(n, t_len)
    incl = masks[0]
    ng = ng_ref[...]
    ri = lax.broadcasted_iota(I32, (n, n), 0)
    ci = lax.broadcasted_iota(I32, (n, n), 1)
    sh = t_len.bit_length() - 1
    pick_last = jnp.where(((ri >> sh) == (ci >> sh)) & ((ci & (t_len - 1)) == t_len - 1), 1.0, 0.0).astype(BF16)
    g_last_cols = _mm_exact_lhs(pick_last, gc_cols)
    for h in range(nh):
        q = _l2(y[:, h * hd:(h + 1) * hd]) * (hd ** -0.5)
        k = _l2(y[:, (nh + h) * hd:(nh + h + 1) * hd])
        v = y[:, (2 * nh + h) * hd:(2 * nh + h + 1) * hd]
        beta_c = beta_tile[:, GB_LANE + h:GB_LANE + h + 1]
        gc_c = gc_cols[:, GA_LANE + h:GA_LANE + h + 1]
        gc_r = gc_rows[h:h + 1, :]
        u, w, qk = _gdn_intra(q, k, v, beta_c, gc_c, gc_r, masks)
        g_last_c = g_last_cols[:, GA_LANE + h:GA_LANE + h + 1]
        uw_scr[0] = u
        uw_scr[1] = w
        uw_scr[2] = q * jnp.exp(gc_c)
        uw_scr[3] = k * jnp.exp(g_last_c - gc_c)
        uw_scr[4] = jnp.broadcast_to(jnp.exp(g_last_c), (n, hd))

        def seq_step(i, carry):
            r0 = pl.multiple_of(i * t_len, t_len)
            rows = pl.ds(r0, t_len)
            s = s0_ref[i, h]
            vn = uw_scr[0, rows, :] - _mm3(uw_scr[1, rows, :], s)
            oi_scr[rows, :] = _mm3(uw_scr[2, rows, :], s)
            vn_scr[rows, :] = vn
            dec = uw_scr[4, pl.ds(r0, 1), :]
            s_o_ref[i, h] = s * dec + _mm3(uw_scr[3, rows, :], vn, _dot_tn)
            return carry

        lax.fori_loop(0, nb, seq_step, 0)
        o = oi_scr[...] + _mm3(jnp.where(incl, qk, 0.0), vn_scr[...])
        o = _rms(o) * ng * _silu(gz_ref[:, h * hd:(h + 1) * hd])
        o_ref[:, h * hd:(h + 1) * hd] = o.astype(BF16)


def _gdn_sample(gq, gz, small, small_t, cstate, state_gdn, layer, conv_w_l, alog_l, dtb_l, alog_c, dtb_c, ng,
                rows_p, db, t_len):
    nb = min(16, db)
    n = nb * t_len
    base = rows_p // n
    gdim = gq.shape[1]
    zdim = gz.shape[1]
    row = lambda i: (base + i, 0)
    const = lambda i: (0, 0)
    kern = functools.partial(_gdn_sample_kernel, nb=nb, t_len=t_len)
    return pl.pallas_call(
        kern,
        out_shape=[jax.ShapeDtypeStruct((db * t_len, zdim), BF16),
                   jax.ShapeDtypeStruct((db, GDN_HEADS, HEAD_DIM, HEAD_DIM), F32)],
        grid=(db // nb,),
        in_specs=[
            pl.BlockSpec((n, gdim), row),
            pl.BlockSpec((n, zdim), row),
            pl.BlockSpec((n, LANES), row),
            pl.BlockSpec((SUBLANES, n), lambda i: (0, base + i)),
            pl.BlockSpec((n, gdim), lambda i: (i, 0)),
            pl.BlockSpec((nb, None, GDN_HEADS, HEAD_DIM, HEAD_DIM), lambda i: (i, layer, 0, 0, 0)),
            pl.BlockSpec((CONV_W, gdim), const),
            pl.BlockSpec((1, LANES), const),
            pl.BlockSpec((1, LANES), const),
            pl.BlockSpec((SUBLANES, 1), const),
            pl.BlockSpec((SUBLANES, 1), const),
            pl.BlockSpec((1, HEAD_DIM), const),
        ],
        out_specs=[pl.BlockSpec((n, zdim), lambda i: (i, 0)),
                   pl.BlockSpec((nb, GDN_HEADS, HEAD_DIM, HEAD_DIM), lambda i: (i, 0, 0, 0))],
        scratch_shapes=[pltpu.VMEM((nb, 2 * SUBLANES, gdim), F32),
                        pltpu.VMEM((5, n, HEAD_DIM), F32),
                        pltpu.VMEM((n, HEAD_DIM), F32),
                        pltpu.VMEM((n, HEAD_DIM), F32)],
        compiler_params=_cparams(("arbitrary",)),
        name="gdn_sample",
    )(gq, gz, small, small_t, cstate, state_gdn, conv_w_l, alog_l, dtb_l, alog_c, dtb_c, ng)


def _outproj_kernel(x_ref, att_ref, gdn_ref, gate_ref, sh_ref, sc_ref, g_ref, wo_ref, *rest, moe):
    if moe:
        rw_ref, rb_ref, x_o, h_o, lg_o = rest
    else:
        x_o, h_o = rest
    x = x_ref[...]
    adim = att_ref.shape[1]
    y = _dot(att_ref[...], wo_ref[0:adim, :]) + _dot(gdn_ref[...], wo_ref[adim:, :])
    xn = x + gate_ref[...] * y.reshape(x.shape)
    x_o[...] = xn
    h = (_rms(xn) * g_ref[...]) * (1.0 + sc_ref[...]) + sh_ref[...]
    h2 = h.reshape(y.shape)
    h_o[...] = h2.astype(BF16)
    if moe:
        lg_o[...] = _mm3(h2, rw_ref[...]) + rb_ref[...]


def _outproj(x3, att, gdn, gate, sh, sc, g, wo, router, tile_of, tm):
    ng, _, d = x3.shape
    rows = ng * SUBLANES
    gt = tm // SUBLANES
    moe = router is not None
    const = lambda t: (0, 0)
    row = lambda t: (t, 0)
    modspec = pl.BlockSpec((gt, 1, d), lambda t: (tile_of(t), 0, 0))
    in_specs = [
        pl.BlockSpec((gt, SUBLANES, d), lambda t: (t, 0, 0)),
        pl.BlockSpec((tm, att.shape[1]), row),
        pl.BlockSpec((tm, gdn.shape[1]), row),
        modspec, modspec, modspec,
        pl.BlockSpec((1, d), const),
        pl.BlockSpec(wo.shape, const),
    ]
    out_shape = [jax.ShapeDtypeStruct(x3.shape, F32), jax.ShapeDtypeStruct((rows, d), BF16)]
    out_specs = [pl.BlockSpec((gt, SUBLANES, d), lambda t: (t, 0, 0)), pl.BlockSpec((tm, d), row)]
    args = [x3, att, gdn, gate, sh, sc, g, wo]
    if moe:
        in_specs += [pl.BlockSpec((d, LANES), const), pl.BlockSpec((1, LANES), const)]
        out_shape.append(jax.ShapeDtypeStruct((rows, LANES), F32))
        out_specs.append(pl.BlockSpec((tm, LANES), row))
        args += list(router)
    return pl.pallas_call(
        functools.partial(_outproj_kernel, moe=moe),
        out_shape=out_shape,
        grid=(rows // tm,),
        in_specs=in_specs,
        out_specs=out_specs,
        compiler_params=_cparams(("arbitrary",)),
        name="out_proj",
    )(*args)


def _ffn_kernel(x_ref, h_ref, gate_ref, wg_ref, wu_ref, wd_ref, o_ref, *, fc):
    h = h_ref[...]
    dff = wg_ref.shape[1]
    acc = None
    for c in range(dff // fc):
        a = _dot(h, wg_ref[:, c * fc:(c + 1) * fc])
        u = _dot(h, wu_ref[:, c * fc:(c + 1) * fc])
        t = _dot((_silu(a) * u).astype(BF16), wd_ref[c * fc:(c + 1) * fc, :])
        acc = t if acc is None else acc + t
    x = x_ref[...]
    o_ref[...] = x + gate_ref[...] * acc.reshape(x.shape)


def _ffn(x3, hff, gate, wg, wu, wd, tile_of, tm):
    ng, _, d = x3.shape
    rows = ng * SUBLANES
    gt = tm // SUBLANES
    dff = wg.shape[1]
    fc = dff
    for cand in (1408, 1024, 768, 512, 256, 128):
        if dff % cand == 0:
            fc = cand
            break
    const = lambda t: (0, 0)
    return pl.pallas_call(
        functools.partial(_ffn_kernel, fc=fc),
        out_shape=jax.ShapeDtypeStruct(x3.shape, F32),
        grid=(rows // tm,),
        in_specs=[
            pl.BlockSpec((gt, SUBLANES, d), lambda t: (t, 0, 0)),
            pl.BlockSpec((tm, d), lambda t: (t, 0)),
            pl.BlockSpec((gt, 1, d), lambda t: (tile_of(t), 0, 0)),
            pl.BlockSpec(wg.shape, const),
            pl.BlockSpec(wu.shape, const),
            pl.BlockSpec(wd.shape, const),
        ],
        out_specs=pl.BlockSpec((gt, SUBLANES, d), lambda t: (t, 0, 0)),
        compiler_params=_cparams(("arbitrary",)),
        name="ffn_dense",
    )(x3, hff, gate, wg, wu, wd)


def _moe_kernel(x_ref, h_ref, lg_ref, gate_ref, wg_ref, wu_ref, wd_ref, o_ref, acc_scr, g_scr):
    e = pl.program_id(1)
    tm = h_ref.shape[0]

    @pl.when(e == 0)
    def _():
        lane = lax.broadcasted_iota(I32, (tm, LANES), 1)
        lg = jnp.where(lane < N_EXPERTS, lg_ref[...], -jnp.inf)
        m1 = jnp.max(lg, axis=1, keepdims=True)
        i1 = jnp.min(jnp.where(lg == m1, lane, LANES), axis=1, keepdims=True)
        rest = jnp.where(lane == i1, -jnp.inf, lg)
        m2 = jnp.max(rest, axis=1, keepdims=True)
        i2 = jnp.min(jnp.where(rest == m2, lane, LANES), axis=1, keepdims=True)
        e2 = jnp.exp(m2 - m1)
        den = 1.0 + e2
        g_scr[...] = jnp.where(lane == i1, 1.0 / den, 0.0) + jnp.where(lane == i2, e2 / den, 0.0)
        acc_scr[...] = jnp.zeros(acc_scr.shape, F32)

    h = h_ref[...]
    a = _dot(h, wg_ref[...])
    u = _dot(h, wu_ref[...])
    y = _dot((_silu(a) * u).astype(BF16), wd_ref[...])
    lane = lax.broadcasted_iota(I32, (tm, LANES), 1)
    ge = jnp.sum(jnp.where(lane == e, g_scr[...], 0.0), axis=1, keepdims=True)
    acc_scr[...] += ge * y

    @pl.when(e == pl.num_programs(1) - 1)
    def _():
        x = x_ref[...]
        o_ref[...] = x + gate_ref[...] * acc_scr[...].reshape(x.shape)


def _moe(x3, hff, logits, gate, wg, wu, wd, tile_of, tm):
    ng, _, d = x3.shape
    rows = ng * SUBLANES
    gt = tm // SUBLANES
    ne, _, eff = wg.shape
    return pl.pallas_call(
        _moe_kernel,
        out_shape=jax.ShapeDtypeStruct(x3.shape, F32),
        grid=(rows // tm, ne),
        in_specs=[
            pl.BlockSpec((gt, SUBLANES, d), lambda t, e: (t, 0, 0)),
            pl.BlockSpec((tm, d), lambda t, e: (t, 0)),
            pl.BlockSpec((tm, LANES), lambda t, e: (t, 0)),
            pl.BlockSpec((gt, 1, d), lambda t, e: (tile_of(t), 0, 0)),
            pl.BlockSpec((None, d, eff), lambda t, e: (e, 0, 0)),
            pl.BlockSpec((None, d, eff), lambda t, e: (e, 0, 0)),
            pl.BlockSpec((None, eff, d), lambda t, e: (e, 0, 0)),
        ],
        out_specs=pl.BlockSpec((gt, SUBLANES, d), lambda t, e: (t, 0, 0)),
        scratch_shapes=[pltpu.VMEM((tm, d), F32), pltpu.VMEM((tm, LANES), F32)],
        compiler_params=_cparams(("arbitrary", "arbitrary")),
        name="ffn_moe",
    )(x3, hff, logits, gate, wg, wu, wd)


def _rope_tables(pos):
    pos = pos.astype(F32)[:, None]
    half = HEAD_DIM // 2
    ang = pos * (ROPE_THETA ** (-jnp.arange(half, dtype=F32) / half))[None, :]
    c, s = jnp.cos(ang), jnp.sin(ang)
    cq = jnp.concatenate([c, c], axis=1)
    sq = jnp.concatenate([-s, s], axis=1)
    half = IDX_DIM // 2
    ang = pos * (ROPE_THETA ** (-jnp.arange(half, dtype=F32) / half))[None, :]
    c, s = jnp.cos(ang), jnp.sin(ang)
    z = jnp.zeros_like(s)
    ci = jnp.concatenate([c, c, c, c], axis=1)
    sa = jnp.concatenate([-s, z, -s, z], axis=1)
    sb = jnp.concatenate([z, s, z, s], axis=1)
    return cq, sq, ci, sa, sb


def _lane_vec(vals, lane0):
    return jnp.zeros((1, LANES), F32).at[0, lane0:lane0 + vals.shape[0]].set(vals)


def kernel(x_prompt, x_sample, cache_k, cache_v, cache_kidx, state_gdn, state_conv, page_table, c_prompt, c_sample, mod_w, mod_b, norm_mix_g, norm_ffn_g, w_in, q_norm_g, k_norm_g, conv_w, a_log, dt_bias, gdn_norm_g, w_out, ffn_w_gate, ffn_w_up, ffn_w_down, router_w, router_b, moe_w_gate, moe_w_up, moe_w_down):
    batch, seq, d = x_prompt.shape
    db, t_len, _ = x_sample.shape
    n_layers = mod_w.shape[0]
    n_pages = page_table.shape[1]
    past = n_pages * PAGE
    rows_p, rows_s = batch * seq, db * t_len
    rows = rows_p + rows_s
    assert t_len == SUBLANES and seq % LANES == 0
    tm = min(256, rows_s, seq)
    assert seq % tm == 0 and rows_s % tm == 0
    gt = tm // SUBLANES
    tiles_per_batch = seq // tm
    n_ptiles = rows_p // tm

    def tile_of(t):
        return jnp.where(t < n_ptiles, t // tiles_per_batch, batch + (t - n_ptiles))

    def tab_of(t):
        return jnp.where(t < n_ptiles, t % tiles_per_batch, tiles_per_batch)

    def groups(m):
        mp = jnp.repeat(m[:batch], gt, axis=0)
        return jnp.concatenate([mp, m[batch:batch + db]], axis=0)[:, None, :]

    x3 = jnp.concatenate([x_prompt.reshape(rows_p, d), x_sample.reshape(rows_s, d)], axis=0)
    x3 = x3.reshape(rows // SUBLANES, SUBLANES, d)

    n_c = batch + db
    c_all = jnp.concatenate([c_prompt, c_sample, jnp.zeros((-n_c % SUBLANES, d), F32)], axis=0)
    mods = _mods(c_all, mod_w, mod_b)

    pos = jnp.concatenate([jnp.arange(seq), past + (jnp.arange(tm) % t_len)])
    tabs = _rope_tables(pos)

    kvd = KV_HEADS * HEAD_DIM
    cache_k4 = cache_k.reshape(cache_k.shape[0], cache_k.shape[1], PAGE, kvd)
    cache_v4 = cache_v.reshape(cache_v.shape[0], cache_v.shape[1], PAGE, kvd)
    n_sel_s = min(TOPK_MAX, (past + t_len) // 4)
    lp = (n_pages + 1) * PAGE

    outs = {name: [] for name in ("kp", "vp", "kip", "ks", "vs", "kis", "sp", "ss", "cp", "cs")}
    col = 0
    offs = []
    for size in (ATTN_HEADS * HEAD_DIM, kvd, kvd, IDX_HEADS * IDX_DIM, IDX_DIM, IDX_HEADS,
                 GDN_HEADS * 3 * HEAD_DIM, GDN_HEADS * HEAD_DIM, GDN_HEADS, GDN_HEADS):
        offs.append((col, col + size))
        col += size
    o_q, o_k, o_v, o_qi, o_ki, o_wi, o_gq, o_gz, o_ga, o_gb = offs

    for l in range(n_layers):
        m6 = [groups(mods[l, :, j * d:(j + 1) * d]) for j in range(6)]
        w = w_in[l]
        wa = w[:, o_q[0]:o_v[1]].astype(BF16)
        qi_w = w[:, o_qi[0]:o_qi[1]].reshape(d, IDX_HEADS, 1, IDX_DIM)
        qi_w = jnp.broadcast_to(qi_w, (d, IDX_HEADS, 2, IDX_DIM)).reshape(d, IDX_HEADS * LANES)
        ki_w = w[:, o_ki[0]:o_ki[1]]
        misc_w = jnp.concatenate([w[:, o_wi[0]:o_wi[1]], w[:, o_ga[0]:o_ga[1]], w[:, o_gb[0]:o_gb[1]],
                                  jnp.zeros((d, LANES - IDX_HEADS - 2 * GDN_HEADS), F32)], axis=1)
        wi_f = jnp.concatenate([qi_w, ki_w, ki_w, misc_w], axis=1)
        wih = wi_f.astype(BF16)
        wil = (wi_f - wih.astype(F32)).astype(BF16)
        wg = w[:, o_gq[0]:o_gz[1]].astype(BF16)

        (q_bf, k_f, v_f, k_bf, v_bf, qi3, ki_f, ki3, small, gq, gz) = _inproj(
            x3, m6[0], m6[1], norm_mix_g[l][None, :], wa, wih, wil, wg,
            q_norm_g[l][None, :], k_norm_g[l][None, :], tabs, tile_of, tab_of, tm)

        small_t = jnp.transpose(small[:, GA_LANE:GA_LANE + 2 * GDN_HEADS])
        alog_l = _lane_vec(a_log[l], GA_LANE)
        dtb_l = _lane_vec(dt_bias[l], GA_LANE)
        pad4 = jnp.zeros((SUBLANES - GDN_HEADS,), F32)
        alog_c = jnp.concatenate([a_log[l], pad4])[:, None]
        dtb_c = jnp.concatenate([dt_bias[l], pad4])[:, None]
        ng = gdn_norm_g[l][None, :]

        att_p = _dsa_prompt(qi3, small, q_bf, ki3, k_bf, v_bf, batch, seq)
        gdn_p, s_p = _gdn_prompt(gq, gz, small, small_t, conv_w[l], alog_l, dtb_l, alog_c, dtb_c, ng,
                                 batch, seq)

        scores = _sample_scores(page_table, qi3, small, ki3, cache_kidx, l, rows_p, t_len)
        bias = _sample_select(scores.reshape(rows_s, lp), n_sel_s)
        att_s = _sample_attend(page_table, q_bf, bias, k_bf, v_bf, cache_k4, cache_v4, l, rows_p, t_len)
        cstate = jnp.pad(state_conv[:, l], ((0, 0), (SUBLANES - (CONV_W - 1), 0), (0, 0)))
        cstate = cstate.reshape(db * SUBLANES, cstate.shape[2])
        gdn_s, s_s = _gdn_sample(gq, gz, small, small_t, cstate, state_gdn, l, conv_w[l], alog_l, dtb_l,
                                 alog_c, dtb_c, ng, rows_p, db, t_len)

        att = jnp.concatenate([att_p, att_s], axis=0)
        gdn = jnp.concatenate([gdn_p, gdn_s], axis=0)
        i = l // 2
        router = None
        if l % 2 == 1:
            rw = jnp.concatenate([router_w[i], jnp.zeros((d, LANES - N_EXPERTS), F32)], axis=1)
            rb = jnp.concatenate([router_b[i], jnp.zeros((LANES - N_EXPERTS,), F32)])[None, :]
            router = (rw, rb)
        res = _outproj(x3, att, gdn, m6[2], m6[3], m6[4], norm_ffn_g[l][None, :], w_out[l].astype(BF16),
                       router, tile_of, tm)
        if l % 2 == 0:
            x3, hff = res
            x3 = _ffn(x3, hff, m6[5], ffn_w_gate[i].astype(BF16), ffn_w_up[i].astype(BF16),
                      ffn_w_down[i].astype(BF16), tile_of, tm)
        else:
            x3, hff, logits = res
            x3 = _moe(x3, hff, logits, m6[5], moe_w_gate[i].astype(BF16), moe_w_up[i].astype(BF16),
                      moe_w_down[i].astype(BF16), tile_of, tm)

        outs["kp"].append(k_f[:rows_p].reshape(batch, seq, KV_HEADS, HEAD_DIM))
        outs["vp"].append(v_f[:rows_p].reshape(batch, seq, KV_HEADS, HEAD_DIM))
        outs["kip"].append(ki_f[:rows_p].reshape(batch, seq, IDX_DIM))
        outs["ks"].append(k_f[rows_p:].reshape(db, t_len, KV_HEADS, HEAD_DIM))
        outs["vs"].append(v_f[rows_p:].reshape(db, t_len, KV_HEADS, HEAD_DIM))
        outs["kis"].append(ki_f[rows_p:].reshape(db, t_len, IDX_DIM))
        outs["sp"].append(s_p)
        outs["ss"].append(s_s)
        gq_p = gq[:rows_p].reshape(batch, seq, -1)
        gq_s = gq[rows_p:].reshape(db, t_len, -1)
        outs["cp"].append(gq_p[:, seq - (CONV_W - 1):])
        outs["cs"].append(gq_s[:, t_len - (CONV_W - 1):])

    x2 = x3.reshape(rows, d)
    st = lambda name: jnp.stack(outs[name], axis=1)
    return (x2[:rows_p].reshape(batch, seq, d), x2[rows_p:].reshape(db, t_len, d),
            st("kp"), st("vp"), st("kip"), st("ks"), st("vs"), st("kis"),
            st("sp"), st("ss"), st("cp"), st("cs"))
```

```python
import functools
import math

import jax
import jax.numpy as jnp
from jax import lax
from jax.experimental import pallas as pl
from jax.experimental.pallas import tpu as pltpu

F32 = jnp.float32
BF16 = jnp.bfloat16
I32 = jnp.int32

HEAD_DIM = 128
ATTN_HEADS = 4
KV_HEADS = 2
GROUP = ATTN_HEADS // KV_HEADS
IDX_HEADS = 8
IDX_DIM = 64
TOPK_MAX = 256
GDN_HEADS = 4
GDN_CHUNK = 64
CONV_W = 4
N_EXPERTS = 8
PAGE = 128
ROPE_THETA = 10000.0
EPS = 1e-6

LANES = 128
SUBLANES = 8
VMEM_LIMIT = 56 * 1024 * 1024
NEG = -1e30
INT_MIN = -2147483648
WI_LANE = 0
GA_LANE = 8
GB_LANE = 12


def _cparams(sem):
    return pltpu.CompilerParams(dimension_semantics=sem, vmem_limit_bytes=VMEM_LIMIT)


def _dot(a, b):
    return jnp.dot(a, b, preferred_element_type=F32)


def _dot_nt(a, b):
    return lax.dot_general(a, b, (((1,), (1,)), ((), ())), preferred_element_type=F32)


def _dot_tn(a, b):
    return lax.dot_general(a, b, (((0,), (0,)), ((), ())), preferred_element_type=F32)


def _split(x):
    hi = x.astype(BF16)
    lo = (x - hi.astype(F32)).astype(BF16)
    return hi, lo


def _mm3(a, b, dot=_dot):
    ah, al = _split(a)
    bh, bl = _split(b)
    return dot(ah, bh) + (dot(ah, bl) + dot(al, bh))


def _split_three(a):
    a1 = a.astype(BF16)
    r1 = a - a1.astype(F32)
    a2 = r1.astype(BF16)
    a3 = (r1 - a2.astype(F32)).astype(BF16)
    return a1, a2, a3


def _mm_exact_rhs(a, b_bf16):
    a1, a2, a3 = _split_three(a)
    return _dot(a1, b_bf16) + (_dot(a2, b_bf16) + _dot(a3, b_bf16))


def _mm_exact_lhs(m_bf16, a):
    a1, a2, a3 = _split_three(a)
    return _dot(m_bf16, a1) + (_dot(m_bf16, a2) + _dot(m_bf16, a3))


def _silu(x):
    return x * jax.nn.sigmoid(x)


def _softplus(x):
    return jnp.maximum(x, 0.0) + jnp.log1p(jnp.exp(-jnp.abs(x)))


def _mods_kernel(c_ref, w_ref, b_ref, o_ref):
    o_ref[...] = _mm3(_silu(c_ref[...]), w_ref[...]) + b_ref[...]


def _mods(c_all, mod_w, mod_b):
    n_layers, d, n6 = mod_w.shape
    rows = c_all.shape[0]
    tn = n6 // 4
    return pl.pallas_call(
        _mods_kernel,
        out_shape=jax.ShapeDtypeStruct((n_layers, rows, n6), F32),
        grid=(n_layers, n6 // tn),
        in_specs=[
            pl.BlockSpec((rows, d), lambda l, j: (0, 0)),
            pl.BlockSpec((None, d, tn), lambda l, j: (l, 0, j)),
            pl.BlockSpec((None, 1, tn), lambda l, j: (l, 0, j)),
        ],
        out_specs=pl.BlockSpec((None, rows, tn), lambda l, j: (l, 0, j)),
        compiler_params=_cparams(("arbitrary", "arbitrary")),
        name="ada_mods",
    )(c_all, mod_w, mod_b.reshape(n_layers, 1, n6))


def _rms(x):
    return x * lax.rsqrt(jnp.mean(x * x, axis=-1, keepdims=True) + EPS)


def _inproj_kernel(x_ref, sh_ref, sc_ref, g_ref, wa_ref, wih_ref, wil_ref, wg_ref, qg_ref, kg_ref,
                   cq_ref, sq_ref, ci_ref, sa_ref, sb_ref,
                   q_o, k_o, v_o, kb_o, vb_o, qi3_o, ki_o, ki3_o, small_o, gq_o, gz_o):
    x = x_ref[...]
    h = (_rms(x) * g_ref[...]) * (1.0 + sc_ref[...]) + sh_ref[...]
    tm = x.shape[0] * x.shape[1]
    h2 = h.reshape(tm, x.shape[2])
    hb, hl = _split(h2)
    za = _dot(hb, wa_ref[...])
    wih = wih_ref[...]
    zi = _dot(hb, wih) + (_dot(hb, wil_ref[...]) + _dot(hl, wih))
    zg = _dot(hb, wg_ref[...])

    cq, sq = cq_ref[...], sq_ref[...]
    qg, kg = qg_ref[...], kg_ref[...]
    for hh in range(ATTN_HEADS):
        qn = _rms(za[:, hh * HEAD_DIM:(hh + 1) * HEAD_DIM]) * qg
        qr = qn * cq + pltpu.roll(qn, HEAD_DIM // 2, 1) * sq
        q_o[:, hh * HEAD_DIM:(hh + 1) * HEAD_DIM] = (qr * (HEAD_DIM ** -0.5)).astype(BF16)
    koff = ATTN_HEADS * HEAD_DIM
    for hh in range(KV_HEADS):
        kn = _rms(za[:, koff + hh * HEAD_DIM:koff + (hh + 1) * HEAD_DIM]) * kg
        kr = kn * cq + pltpu.roll(kn, HEAD_DIM // 2, 1) * sq
        k_o[:, hh * HEAD_DIM:(hh + 1) * HEAD_DIM] = kr
        kb_o[:, hh * HEAD_DIM:(hh + 1) * HEAD_DIM] = kr.astype(BF16)
    voff = koff + KV_HEADS * HEAD_DIM
    v = za[:, voff:voff + KV_HEADS * HEAD_DIM]
    v_o[...] = v
    vb_o[...] = v.astype(BF16)

    ci, sa, sb = ci_ref[...], sa_ref[...], sb_ref[...]
    first = lax.broadcasted_iota(I32, (tm, LANES), 1) < IDX_DIM

    def rope64(t):
        return t * ci + pltpu.roll(t, LANES - IDX_DIM // 2, 1) * sa + pltpu.roll(t, IDX_DIM // 2, 1) * sb

    for hh in range(IDX_HEADS):
        r = rope64(zi[:, hh * LANES:(hh + 1) * LANES]) * (IDX_DIM ** -0.5)
        hi = r.astype(BF16).astype(F32)
        qi3_o[hh, :, 0:LANES] = jnp.where(first, hi, r - hi).astype(BF16)
        qi3_o[hh, :, LANES:2 * LANES] = jnp.where(first, hi, 0.0).astype(BF16)
    r = rope64(zi[:, IDX_HEADS * LANES:(IDX_HEADS + 1) * LANES])
    ki_o[...] = r[:, :IDX_DIM]
    hi = r.astype(BF16).astype(F32)
    ki3_o[:, 0:LANES] = hi.astype(BF16)
    ki3_o[:, LANES:2 * LANES] = jnp.where(first, r - hi, 0.0).astype(BF16)
    misc = zi[:, (IDX_HEADS + 1) * LANES:(IDX_HEADS + 2) * LANES]
    lane = lax.broadcasted_iota(I32, (tm, LANES), 1)
    small_o[...] = jnp.where(lane < IDX_HEADS, misc * (IDX_HEADS ** -0.5), misc)
    gdim = gq_o.shape[1]
    gq_o[...] = zg[:, :gdim]
    gz_o[...] = zg[:, gdim:]


def _inproj(x3, sh, sc, g, wa, wih, wil, wg, qg, kg, tabs, tile_of, tab_of, tm):
    ng, _, d = x3.shape
    rows = ng * SUBLANES
    gt = tm // SUBLANES
    nt = rows // tm
    gdim = GDN_HEADS * 3 * HEAD_DIM
    zdim = GDN_HEADS * HEAD_DIM
    const = lambda t: (0, 0)
    row = lambda t: (t, 0)
    tab = lambda t: (tab_of(t), 0)
    in_specs = [
        pl.BlockSpec((gt, SUBLANES, d), lambda t: (t, 0, 0)),
        pl.BlockSpec((gt, 1, d), lambda t: (tile_of(t), 0, 0)),
        pl.BlockSpec((gt, 1, d), lambda t: (tile_of(t), 0, 0)),
        pl.BlockSpec((1, d), const),
        pl.BlockSpec(wa.shape, const),
        pl.BlockSpec(wih.shape, const),
        pl.BlockSpec(wil.shape, const),
        pl.BlockSpec(wg.shape, const),
        pl.BlockSpec((1, HEAD_DIM), const),
        pl.BlockSpec((1, HEAD_DIM), const),
    ] + [pl.BlockSpec((tm, LANES), tab)] * 5
    kvd = KV_HEADS * HEAD_DIM
    out_shape = [
        jax.ShapeDtypeStruct((rows, ATTN_HEADS * HEAD_DIM), BF16),
        jax.ShapeDtypeStruct((rows, kvd), F32),
        jax.ShapeDtypeStruct((rows, kvd), F32),
        jax.ShapeDtypeStruct((rows, kvd), BF16),
        jax.ShapeDtypeStruct((rows, kvd), BF16),
        jax.ShapeDtypeStruct((IDX_HEADS, rows, 2 * LANES), BF16),
        jax.ShapeDtypeStruct((rows, IDX_DIM), F32),
        jax.ShapeDtypeStruct((rows, 2 * LANES), BF16),
        jax.ShapeDtypeStruct((rows, LANES), F32),
        jax.ShapeDtypeStruct((rows, gdim), F32),
        jax.ShapeDtypeStruct((rows, zdim), F32),
    ]
    out_specs = [
        pl.BlockSpec((tm, ATTN_HEADS * HEAD_DIM), row),
        pl.BlockSpec((tm, kvd), row),
        pl.BlockSpec((tm, kvd), row),
        pl.BlockSpec((tm, kvd), row),
        pl.BlockSpec((tm, kvd), row),
        pl.BlockSpec((IDX_HEADS, tm, 2 * LANES), lambda t: (0, t, 0)),
        pl.BlockSpec((tm, IDX_DIM), row),
        pl.BlockSpec((tm, 2 * LANES), row),
        pl.BlockSpec((tm, LANES), row),
        pl.BlockSpec((tm, gdim), row),
        pl.BlockSpec((tm, zdim), row),
    ]
    return pl.pallas_call(
        _inproj_kernel,
        out_shape=out_shape,
        grid=(nt,),
        in_specs=in_specs,
        out_specs=out_specs,
        compiler_params=_cparams(("arbitrary",)),
        name="in_proj",
    )(x3, sh, sc, g, wa, wih, wil, wg, qg, kg, *tabs)


FLT_BIG = 3.0e38
SEARCH_STEPS = 20


def _sort_key(score):
    bits = pltpu.bitcast(score, I32)
    return jnp.where(bits < 0, bits ^ jnp.int32(0x7FFFFFFF), bits)


def _count(sc_scr, nkc, kc, rows, n, preds):
    def body(c, accs):
        base = pl.multiple_of(c * kc, kc)
        accs = list(accs)
        for j in range(kc // LANES):
            sc = sc_scr[:, pl.ds(base + j * LANES, LANES)]
            ps = preds(sc, base + j * LANES)
            for i in range(n):
                accs[i] = accs[i] + jnp.where(ps[i], 1.0, 0.0)
        return tuple(accs)

    accs = lax.fori_loop(0, nkc, body, tuple(jnp.zeros((rows, LANES), F32) for _ in range(n)))
    return [jnp.sum(a, axis=1, keepdims=True) for a in accs]


def _exact_threshold(sc_scr, nkc, kc, rows, n_sel, idx_bits, active):
    n_sel_f = float(n_sel)

    def bit_step(b, acc):
        cand = acc | (jnp.int32(1) << (31 - b))
        cand_s = jnp.broadcast_to(cand ^ jnp.int32(INT_MIN), (rows, LANES))
        cnt, = _count(sc_scr, nkc, kc, rows, 1, lambda sc, c0: (_sort_key(sc) >= cand_s,))
        return jnp.where(cnt >= n_sel_f, cand, acc)

    acc = lax.fori_loop(0, 32, bit_step, jnp.zeros((rows, 1), I32))
    thr = acc ^ jnp.int32(INT_MIN)
    thr_b = jnp.broadcast_to(thr, (rows, LANES))

    def gt_eq(sc, c0):
        kk = _sort_key(sc)
        return kk > thr_b, kk == thr_b

    cnt_gt, cnt_eq = _count(sc_scr, nkc, kc, rows, 2, gt_eq)
    need = n_sel_f - cnt_gt
    excess = active & (cnt_eq > need)
    any_excess = jnp.max(jnp.where(excess, 1.0, 0.0)) > 0.0
    lane = lax.broadcasted_iota(I32, (rows, LANES), 1)

    @pl.when(any_excess)
    def _():
        def idx_step(b, p):
            cand = p | (jnp.int32(1) << (idx_bits - 1 - b))
            cand_b = jnp.broadcast_to(cand, (rows, LANES))
            cnt, = _count(sc_scr, nkc, kc, rows, 1,
                          lambda sc, c0: ((_sort_key(sc) == thr_b) & ((lane + c0) < cand_b),))
            return jnp.where(cnt < need, cand, p)

        cut = lax.fori_loop(0, idx_bits, idx_step, jnp.zeros((rows, 1), I32))
        cut_b = jnp.broadcast_to(cut, (rows, LANES))
        drop_row = jnp.broadcast_to(excess, (rows, LANES))

        def drop(c, carry):
            base = pl.multiple_of(c * kc, kc)
            for j in range(kc // LANES):
                sl = pl.ds(base + j * LANES, LANES)
                sc = sc_scr[:, sl]
                kill = drop_row & (_sort_key(sc) == thr_b) & ((lane + (base + j * LANES)) > cut_b)
                sc_scr[:, sl] = jnp.where(kill, -jnp.inf, sc)
            return carry

        lax.fori_loop(0, nkc, drop, 0)

    bits = jnp.where(thr < 0, thr ^ jnp.int32(0x7FFFFFFF), thr)
    return pltpu.bitcast(bits, F32)


def _upper_normal_quantile(p):
    pp = jnp.clip(jnp.minimum(p, 1.0 - p), 1e-6, 0.5)
    t = jnp.sqrt(-2.0 * jnp.log(pp))
    z = t - (2.515517 + 0.802853 * t + 0.010328 * t * t) / (
        1.0 + 1.432788 * t + 0.189269 * t * t + 0.001308 * t * t * t)
    return jnp.where(p > 0.5, -z, z)


def _select_threshold(sc_scr, nkc, kc, rows, n_sel, idx_bits, mx, mn, mean, std, nv):
    n_f = float(n_sel)

    def probe(mid, st):
        lo, hi, c_lo, c_hi, thr, done = st
        mid_b = jnp.broadcast_to(mid, (rows, LANES))
        cnt, = _count(sc_scr, nkc, kc, rows, 1, lambda sc, c0: (sc >= mid_b,))
        hit = (cnt == n_f) & (done < 0.5)
        up_lo = (cnt > n_f) & (mid > lo)
        up_hi = (cnt < n_f) & (mid < hi)
        return (jnp.where(up_lo, mid, lo), jnp.where(up_hi, mid, hi), jnp.where(up_lo, cnt, c_lo),
                jnp.where(up_hi, cnt, c_hi), jnp.where(hit, mid, thr), jnp.where(hit, 1.0, done))

    st = (mn, mx, nv, jnp.zeros((rows, 1), F32), jnp.full((rows, 1), -FLT_BIG, F32),
          jnp.where(nv <= n_f, 1.0, 0.0))
    guess = mean + std * _upper_normal_quantile(n_f / jnp.maximum(nv, 1.0))
    st = probe(guess, st)
    lo, hi, c_lo, c_hi = st[:4]
    c_near = jnp.where(lo == guess, c_lo, jnp.maximum(c_hi, 0.5))
    step = std * (1.5 * jnp.abs(jnp.log(c_near / n_f)) + 0.01)
    second = jnp.where(lo == guess, lo + step, hi - step)
    second = jnp.where((second > lo) & (second < hi), second, 0.5 * (lo + hi))
    st = probe(second, st)

    def cond(c):
        return jnp.logical_and(c[0] < SEARCH_STEPS, jnp.min(c[1][5]) < 0.5)

    def body(c):
        it, st = c
        lo, hi, c_lo, c_hi = st[:4]
        frac = jnp.clip((c_lo - n_f + 0.5) / jnp.maximum(c_lo - c_hi, 1.0), 1.0 / 32, 31.0 / 32)
        frac = jnp.where(it % 3 == 2, 0.5, frac)
        return it + 1, probe(lo + (hi - lo) * frac, st)

    _, st = lax.while_loop(cond, body, (jnp.int32(0), st))
    thr, done = st[4], st[5]
    open_rows = done < 0.5

    def exact():
        return _exact_threshold(sc_scr, nkc, kc, rows, n_sel, idx_bits, open_rows)

    thr_x = lax.cond(jnp.min(done) < 0.5, exact, lambda: jnp.zeros((rows, 1), F32))
    return jnp.where(open_rows, thr_x, thr)


def _dsa_prompt_kernel(qi3_ref, small_ref, q_ref, ki3_ref, k_ref, v_ref, o_ref, sc_scr, w_scr,
                       *, tq, kc, n_sel, idx_bits):
    i = pl.program_id(1)
    nkc = (i * tq + tq + kc - 1) // kc
    q3 = qi3_ref[...].reshape(IDX_HEADS * tq, 2 * LANES)
    wi = small_ref[:, WI_LANE:WI_LANE + IDX_HEADS]
    for h in range(IDX_HEADS):
        w_scr[h] = jnp.broadcast_to(wi[:, h:h + 1], (tq, LANES))
    row = i * tq + lax.broadcasted_iota(I32, (tq, LANES), 0)
    lane = lax.broadcasted_iota(I32, (tq, LANES), 1)

    def score_chunk(c, stats):
        base = pl.multiple_of(c * kc, kc)
        s = _dot_nt(q3, ki3_ref[pl.ds(base, kc), :])
        smax, smin, ssum, ssq = stats
        for j in range(kc // LANES):
            acc = None
            for h in range(IDX_HEADS):
                t = w_scr[h] * jnp.maximum(s[h * tq:(h + 1) * tq, j * LANES:(j + 1) * LANES], 0.0)
                acc = t if acc is None else acc + t
            valid = (lane + (base + j * LANES)) <= row
            sc_scr[:, pl.ds(base + j * LANES, LANES)] = jnp.where(valid, acc, -jnp.inf)
            smax = jnp.maximum(smax, jnp.where(valid, acc, -FLT_BIG))
            smin = jnp.minimum(smin, jnp.where(valid, acc, FLT_BIG))
            az = jnp.where(valid, acc, 0.0)
            ssum = ssum + az
            ssq = ssq + az * az
        return smax, smin, ssum, ssq

    z = jnp.zeros((tq, LANES), F32)
    smax, smin, ssum, ssq = lax.fori_loop(0, nkc, score_chunk, (z - FLT_BIG, z + FLT_BIG, z, z))
    nv = (row[:, 0:1] + 1).astype(F32)
    mx = jnp.max(smax, axis=1, keepdims=True)
    mn = jnp.min(smin, axis=1, keepdims=True)
    mean = jnp.sum(ssum, axis=1, keepdims=True) / nv
    var = jnp.maximum(jnp.sum(ssq, axis=1, keepdims=True) / nv - mean * mean, 0.0)
    thr = _select_threshold(sc_scr, nkc, kc, tq, n_sel, idx_bits, mx, mn, mean, jnp.sqrt(var), nv)
    thr_b = jnp.broadcast_to(thr, (tq, LANES))

    qs = []
    for g in range(KV_HEADS):
        qs.append(jnp.concatenate(
            [q_ref[:, (g * GROUP + a) * HEAD_DIM:(g * GROUP + a + 1) * HEAD_DIM] for a in range(GROUP)], axis=0))

    def attend_chunk(c, carry):
        base = pl.multiple_of(c * kc, kc)
        biases = []
        for j in range(kc // LANES):
            sc = sc_scr[:, pl.ds(base + j * LANES, LANES)]
            biases.append(jnp.where(sc >= thr_b, 0.0, NEG))
        bias = jnp.concatenate(biases, axis=1)
        bias = jnp.concatenate([bias] * GROUP, axis=0)
        new = []
        for g in range(KV_HEADS):
            m, l, acc = carry[g]
            kg = k_ref[pl.ds(base, kc), g * HEAD_DIM:(g + 1) * HEAD_DIM]
            vg = v_ref[pl.ds(base, kc), g * HEAD_DIM:(g + 1) * HEAD_DIM]
            s = _dot_nt(qs[g], kg) + bias
            m_new = jnp.maximum(m, jnp.max(s, axis=1, keepdims=True))
            alpha = jnp.exp(m - m_new)
            p = jnp.exp(s - m_new)
            l = alpha * l + jnp.sum(p, axis=1, keepdims=True)
            acc = alpha * acc + _dot(p.astype(BF16), vg)
            new.append((m_new, l, acc))
        return tuple(new)

    init = tuple((jnp.full((GROUP * tq, 1), NEG, F32), jnp.zeros((GROUP * tq, 1), F32),
                  jnp.zeros((GROUP * tq, HEAD_DIM), F32)) for _ in range(KV_HEADS))
    res = lax.fori_loop(0, nkc, attend_chunk, init)
    for g in range(KV_HEADS):
        _, l, acc = res[g]
        o = acc / l
        for a in range(GROUP):
            hh = g * GROUP + a
            o_ref[:, hh * HEAD_DIM:(hh + 1) * HEAD_DIM] = o[a * tq:(a + 1) * tq].astype(BF16)


def _dsa_prompt(qi3, small, q_bf, ki3, k_bf, v_bf, batch, seq):
    tq = LANES
    kc = min(512, seq)
    nq = seq // tq
    n_sel = min(TOPK_MAX, seq // 4)
    idx_bits = max(1, (seq - 1).bit_length())
    kvd = KV_HEADS * HEAD_DIM
    qrow = lambda b, i: (b * nq + i, 0)
    kern = functools.partial(_dsa_prompt_kernel, tq=tq, kc=kc, n_sel=n_sel, idx_bits=idx_bits)
    return pl.pallas_call(
        kern,
        out_shape=jax.ShapeDtypeStruct((batch * seq, ATTN_HEADS * HEAD_DIM), BF16),
        grid=(batch, nq),
        in_specs=[
            pl.BlockSpec((IDX_HEADS, tq, 2 * LANES), lambda b, i: (0, b * nq + i, 0)),
            pl.BlockSpec((tq, LANES), qrow),
            pl.BlockSpec((tq, ATTN_HEADS * HEAD_DIM), qrow),
            pl.BlockSpec((seq, 2 * LANES), lambda b, i: (b, 0)),
            pl.BlockSpec((seq, kvd), lambda b, i: (b, 0)),
            pl.BlockSpec((seq, kvd), lambda b, i: (b, 0)),
        ],
        out_specs=pl.BlockSpec((tq, ATTN_HEADS * HEAD_DIM), qrow),
        scratch_shapes=[pltpu.VMEM((tq, seq), F32), pltpu.VMEM((IDX_HEADS, tq, LANES), F32)],
        compiler_params=_cparams(("arbitrary", "arbitrary")),
        name="dsa_prompt",
    )(qi3, small, q_bf, ki3, k_bf, v_bf)


SEQ_PER_STEP = 2


def _sample_scores_kernel(pt_ref, qi3_ref, small_ref, ki3n_ref, *rest, n_pages, t_len):
    pages = rest[:SEQ_PER_STEP * n_pages]
    s_o = rest[SEQ_PER_STEP * n_pages]
    nr = SEQ_PER_STEP * t_len
    q3 = qi3_ref[...].reshape(IDX_HEADS * nr, 2 * LANES)
    wi = small_ref[:, WI_LANE:WI_LANE + IDX_HEADS]
    knew = jnp.concatenate([ki3n_ref[...], jnp.zeros((PAGE - nr, 2 * LANES), BF16)], axis=0)
    zpad = jnp.zeros((2 * LANES - 3 * IDX_DIM, PAGE), F32)
    lane = lax.broadcasted_iota(I32, (t_len, LANES), 1)
    trow = lax.broadcasted_iota(I32, (t_len, LANES), 0)

    def head_sum(s, j):
        acc = None
        for h in range(IDX_HEADS):
            r0 = h * nr + j * t_len
            t = wi[j * t_len:(j + 1) * t_len, h:h + 1] * jnp.maximum(s[r0:r0 + t_len], 0.0)
            acc = t if acc is None else acc + t
        return acc

    for j in range(SEQ_PER_STEP):
        for p in range(n_pages):
            kp = pages[j * n_pages + p][...]
            hi = kp.astype(BF16).astype(F32)
            k3 = jnp.concatenate([hi, hi, kp - hi, zpad], axis=0).astype(BF16)
            s_o[j, :, p * PAGE:(p + 1) * PAGE] = head_sum(_dot(q3, k3), j)
        sn = head_sum(_dot_nt(q3, knew), j)
        ok = (lane >= j * t_len) & (lane - j * t_len <= trow)
        s_o[j, :, n_pages * PAGE:(n_pages + 1) * PAGE] = jnp.where(ok, sn, -jnp.inf)


def _sample_scores(page_table, qi3, small, ki3, cache_kidx_t, layer, rows_p, t_len):
    db, n_pages = page_table.shape
    nr = SEQ_PER_STEP * t_len
    base = rows_p // nr
    lp = (n_pages + 1) * PAGE
    in_specs = [
        pl.BlockSpec((IDX_HEADS, nr, 2 * LANES), lambda n, pt: (0, base + n, 0)),
        pl.BlockSpec((nr, LANES), lambda n, pt: (base + n, 0)),
        pl.BlockSpec((nr, 2 * LANES), lambda n, pt: (base + n, 0)),
    ]
    for j in range(SEQ_PER_STEP):
        for p in range(n_pages):
            in_specs.append(pl.BlockSpec(
                (None, None, IDX_DIM, PAGE),
                lambda n, pt, j=j, p=p: (pt[n * SEQ_PER_STEP + j, p], layer, 0, 0)))
    kern = functools.partial(_sample_scores_kernel, n_pages=n_pages, t_len=t_len)
    return pl.pallas_call(
        kern,
        out_shape=jax.ShapeDtypeStruct((db, t_len, lp), F32),
        grid_spec=pltpu.PrefetchScalarGridSpec(
            num_scalar_prefetch=1,
            grid=(db // SEQ_PER_STEP,),
            in_specs=in_specs,
            out_specs=pl.BlockSpec((SEQ_PER_STEP, t_len, lp), lambda n, pt: (n, 0, 0)),
        ),
        compiler_params=_cparams(("arbitrary",)),
        name="dsa_sample_scores",
    )(page_table, qi3, small, ki3, *([cache_kidx_t] * (SEQ_PER_STEP * n_pages)))


def _sample_select_kernel(s_ref, b_ref, sc_scr, *, n_sel, idx_bits):
    rows, lp = s_ref.shape
    nkc = lp // LANES
    z = jnp.zeros((rows, LANES), F32)
    smax, smin, ssum, ssq, cntv = z - FLT_BIG, z + FLT_BIG, z, z, z
    for c in range(nkc):
        sc = s_ref[:, c * LANES:(c + 1) * LANES]
        sc_scr[:, c * LANES:(c + 1) * LANES] = sc
        valid = sc > -jnp.inf
        smax = jnp.maximum(smax, jnp.where(valid, sc, -FLT_BIG))
        smin = jnp.minimum(smin, jnp.where(valid, sc, FLT_BIG))
        az = jnp.where(valid, sc, 0.0)
        ssum, ssq, cntv = ssum + az, ssq + az * az, cntv + jnp.where(valid, 1.0, 0.0)
    nv = jnp.sum(cntv, axis=1, keepdims=True)
    nvs = jnp.maximum(nv, 1.0)
    mean = jnp.sum(ssum, axis=1, keepdims=True) / nvs
    var = jnp.maximum(jnp.sum(ssq, axis=1, keepdims=True) / nvs - mean * mean, 0.0)
    thr = _select_threshold(sc_scr, nkc, LANES, rows, n_sel, idx_bits,
                            jnp.max(smax, axis=1, keepdims=True), jnp.min(smin, axis=1, keepdims=True),
                            mean, jnp.sqrt(var), nv)
    thr_b = jnp.broadcast_to(thr, (rows, LANES))
    for c in range(nkc):
        b_ref[:, c * LANES:(c + 1) * LANES] = jnp.where(sc_scr[:, c * LANES:(c + 1) * LANES] >= thr_b, 0.0, NEG)


def _sample_select(scores2d, n_sel):
    rows, lp = scores2d.shape
    tr = min(LANES, rows)
    kern = functools.partial(_sample_select_kernel, n_sel=n_sel, idx_bits=max(1, (lp - 1).bit_length()))
    return pl.pallas_call(
        kern,
        out_shape=jax.ShapeDtypeStruct((rows, lp), F32),
        grid=(rows // tr,),
        in_specs=[pl.BlockSpec((tr, lp), lambda r: (r, 0))],
        out_specs=pl.BlockSpec((tr, lp), lambda r: (r, 0)),
        scratch_shapes=[pltpu.VMEM((tr, lp), F32)],
        compiler_params=_cparams(("arbitrary",)),
        name="dsa_sample_select",
    )(scores2d)


def _sample_attend_kernel(pt_ref, q_ref, bias_ref, kn_ref, vn_ref, *rest, n_pages, t_len):
    npg = SEQ_PER_STEP * n_pages
    kpages, vpages = rest[:npg], rest[npg:2 * npg]
    o_ref = rest[2 * npg]
    kc_scr, vc_scr = rest[2 * npg + 1], rest[2 * npg + 2]
    nr = SEQ_PER_STEP * t_len
    lp = (n_pages + 1) * PAGE
    kvd = KV_HEADS * HEAD_DIM
    bias = jnp.concatenate([bias_ref[...]] * GROUP, axis=0)
    pad = jnp.zeros((PAGE - nr, kvd), BF16)
    kc_scr[n_pages * PAGE:lp, :] = jnp.concatenate([kn_ref[...], pad], axis=0)
    vc_scr[n_pages * PAGE:lp, :] = jnp.concatenate([vn_ref[...], pad], axis=0)
    for j in range(SEQ_PER_STEP):
        for p in range(n_pages):
            for g in range(KV_HEADS):
                head_rows = pl.ds(g, PAGE, stride=KV_HEADS)
                cols = slice(g * HEAD_DIM, (g + 1) * HEAD_DIM)
                kc_scr[p * PAGE:(p + 1) * PAGE, cols] = kpages[j * n_pages + p][head_rows, :].astype(BF16)
                vc_scr[p * PAGE:(p + 1) * PAGE, cols] = vpages[j * n_pages + p][head_rows, :].astype(BF16)
        for g in range(KV_HEADS):
            qs = jnp.concatenate(
                [q_ref[:, (g * GROUP + a) * HEAD_DIM:(g * GROUP + a + 1) * HEAD_DIM] for a in range(GROUP)],
                axis=0)
            s = _dot_nt(qs, kc_scr[:, g * HEAD_DIM:(g + 1) * HEAD_DIM]) + bias
            m = jnp.max(s, axis=1, keepdims=True)
            p_ = jnp.exp(s - m)
            l = jnp.sum(p_, axis=1, keepdims=True)
            o = _dot(p_.astype(BF16), vc_scr[:, g * HEAD_DIM:(g + 1) * HEAD_DIM]) / l
            for a in range(GROUP):
                hh = g * GROUP + a
                r0 = a * nr + j * t_len
                o_ref[j * t_len:(j + 1) * t_len, hh * HEAD_DIM:(hh + 1) * HEAD_DIM] = (
                    o[r0:r0 + t_len].astype(BF16))


def _sample_attend(page_table, q_bf, bias2d, k_bf, v_bf, cache_k2, cache_v2, depth, layer, rows_p, t_len):
    db, n_pages = page_table.shape
    nr = SEQ_PER_STEP * t_len
    base = rows_p // nr
    lp = (n_pages + 1) * PAGE
    kvd = KV_HEADS * HEAD_DIM
    in_specs = [
        pl.BlockSpec((nr, ATTN_HEADS * HEAD_DIM), lambda n, pt: (base + n, 0)),
        pl.BlockSpec((nr, lp), lambda n, pt: (n, 0)),
        pl.BlockSpec((nr, kvd), lambda n, pt: (base + n, 0)),
        pl.BlockSpec((nr, kvd), lambda n, pt: (base + n, 0)),
    ]
    for _ in range(2):
        for j in range(SEQ_PER_STEP):
            for p in range(n_pages):
                in_specs.append(pl.BlockSpec(
                    (PAGE * KV_HEADS, HEAD_DIM),
                    lambda n, pt, j=j, p=p: (pt[n * SEQ_PER_STEP + j, p] * depth + layer, 0)))
    kern = functools.partial(_sample_attend_kernel, n_pages=n_pages, t_len=t_len)
    npg = SEQ_PER_STEP * n_pages
    return pl.pallas_call(
        kern,
        out_shape=jax.ShapeDtypeStruct((db * t_len, ATTN_HEADS * HEAD_DIM), BF16),
        grid_spec=pltpu.PrefetchScalarGridSpec(
            num_scalar_prefetch=1,
            grid=(db // SEQ_PER_STEP,),
            in_specs=in_specs,
            out_specs=pl.BlockSpec((nr, ATTN_HEADS * HEAD_DIM), lambda n, pt: (n, 0)),
            scratch_shapes=[pltpu.VMEM((lp, kvd), BF16), pltpu.VMEM((lp, kvd), BF16)],
        ),
        compiler_params=_cparams(("arbitrary",)),
        name="dsa_sample_attend",
    )(page_table, q_bf, bias2d, k_bf, v_bf, *([cache_k2] * npg), *([cache_v2] * npg))


def _chunk_masks(n, chunk):
    ri = lax.broadcasted_iota(I32, (n, n), 0)
    ci = lax.broadcasted_iota(I32, (n, n), 1)

    def same(size):
        sh = size.bit_length() - 1
        return (ri >> sh) == (ci >> sh)

    same_c = same(chunk)
    incl = same_c & (ci <= ri)
    strict = same_c & (ci < ri)
    base = min(SUBLANES, chunk)
    levels = []
    s = base
    while s < chunk:
        levels.append(same(2 * s) & jnp.logical_not(same(s)))
        s *= 2
    return incl, strict, same(base), levels, (ri == ci)


def _unit_lower_inverse(a, same_base, levels, eye):
    ident = jnp.where(eye, 1.0, 0.0)
    ad = jnp.where(same_base, a, 0.0)
    a2 = _mm3(ad, ad)
    a4 = _mm3(a2, a2)
    t = _mm3(_mm3(ident - ad, ident + a2), ident + a4)
    for lv in levels:
        off = jnp.where(lv, a, 0.0)
        t = t - _mm3(_mm3(t, off), t)
    return t


def _gdn_intra(q, k, v, beta_c, gc_c, gc_r, masks):
    incl, strict, same_base, levels, eye = masks
    decay = jnp.exp(jnp.where(incl, gc_c - gc_r, -jnp.inf))
    kb = k * beta_c
    a = jnp.where(strict, _mm3(kb, k, _dot_nt) * decay, 0.0)
    t = _unit_lower_inverse(a, same_base, levels, eye)
    uw = _mm3(t, jnp.concatenate([v * beta_c, kb * jnp.exp(gc_c)], axis=1))
    qk = _mm3(q, k, _dot_nt) * decay
    return uw[:, :HEAD_DIM], uw[:, HEAD_DIM:], qk


def _l2(x):
    return x * lax.rsqrt(jnp.sum(x * x, axis=-1, keepdims=True) + EPS)


def _cum_matrices(n, chunk):
    ri = lax.broadcasted_iota(I32, (n, n), 0)
    ci = lax.broadcasted_iota(I32, (n, n), 1)
    sh = chunk.bit_length() - 1
    same = (ri >> sh) == (ci >> sh)
    lower = jnp.where(same & (ci <= ri), 1.0, 0.0).astype(BF16)
    upper = jnp.where(same & (ri <= ci), 1.0, 0.0).astype(BF16)
    return lower, upper


def _gdn_gates(sm, smt, alog_l, dtb_l, alog_c, dtb_c, chunk):
    n = sm.shape[0]
    lower, upper = _cum_matrices(n, chunk)
    g_tile = -jnp.exp(alog_l) * _softplus(sm + dtb_l)
    beta_tile = jax.nn.sigmoid(sm)
    gc_cols = _mm_exact_lhs(lower, g_tile)
    g_rows = -jnp.exp(alog_c) * _softplus(smt + dtb_c)
    gc_rows = _mm_exact_rhs(g_rows, upper)
    return beta_tile, gc_cols, gc_rows


def _gdn_prompt_kernel(gq_ref, gz_ref, sm_ref, smt_ref, cw_ref, alog_l, dtb_l, alog_c, dtb_c, ng_ref,
                       o_ref, s_o_ref, stage, s_scr, *, tt, chunk):
    t_idx = pl.program_id(1)
    hd = HEAD_DIM
    nh = GDN_HEADS

    @pl.when(t_idx == 0)
    def _():
        stage[0:SUBLANES, :] = jnp.zeros((SUBLANES, stage.shape[1]), F32)
        s_scr[...] = jnp.zeros(s_scr.shape, F32)

    x = gq_ref[...]
    stage[SUBLANES:SUBLANES + tt, :] = x
    y = None
    for j in range(CONV_W):
        term = stage[pl.ds(SUBLANES - (CONV_W - 1) + j, tt), :] * cw_ref[j:j + 1, :]
        y = term if y is None else y + term
    stage[0:SUBLANES, :] = x[tt - SUBLANES:tt, :]
    y = _silu(y)

    beta_tile, gc_cols, gc_rows = _gdn_gates(sm_ref[...], smt_ref[...], alog_l[...], dtb_l[...],
                                             alog_c[...], dtb_c[...], chunk)
    masks = _chunk_masks(tt, chunk)
    incl = masks[0]
    ng = ng_ref[...]
    nchunks = tt // chunk
    for h in range(nh):
        q = _l2(y[:, h * hd:(h + 1) * hd]) * (hd ** -0.5)
        k = _l2(y[:, (nh + h) * hd:(nh + h + 1) * hd])
        v = y[:, (2 * nh + h) * hd:(2 * nh + h + 1) * hd]
        beta_c = beta_tile[:, GB_LANE + h:GB_LANE + h + 1]
        gc_c = gc_cols[:, GA_LANE + h:GA_LANE + h + 1]
        gc_r = gc_rows[h:h + 1, :]
        u, w, qk = _gdn_intra(q, k, v, beta_c, gc_c, gc_r, masks)
        qg = q * jnp.exp(gc_c)
        s = s_scr[h]
        vnew, ointer = [], []
        for c in range(nchunks):
            r = slice(c * chunk, (c + 1) * chunk)
            vn = u[r] - _mm3(w[r], s)
            ointer.append(_mm3(qg[r], s))
            g_last = gc_c[(c + 1) * chunk - 1:(c + 1) * chunk, :]
            kdec = k[r] * jnp.exp(g_last - gc_c[r])
            s = s * jnp.exp(g_last) + _mm3(kdec, vn, _dot_tn)
            vnew.append(vn)
        s_scr[h] = s
        o = jnp.concatenate(ointer, axis=0) + _mm3(jnp.where(incl, qk, 0.0), jnp.concatenate(vnew, axis=0))
        o = _rms(o) * ng * _silu(gz_ref[:, h * hd:(h + 1) * hd])
        o_ref[:, h * hd:(h + 1) * hd] = o.astype(BF16)

    @pl.when(t_idx == pl.num_programs(1) - 1)
    def _():
        s_o_ref[...] = s_scr[...]


def _gdn_prompt(gq, gz, small, small_t, conv_w_l, alog_l, dtb_l, alog_c, dtb_c, ng, batch, seq):
    tt = min(256, seq)
    chunk = min(GDN_CHUNK, seq)
    nt = seq // tt
    gdim = gq.shape[1]
    zdim = gz.shape[1]
    row = lambda b, t: (b * nt + t, 0)
    const = lambda b, t: (0, 0)
    kern = functools.partial(_gdn_prompt_kernel, tt=tt, chunk=chunk)
    return pl.pallas_call(
        kern,
        out_shape=[jax.ShapeDtypeStruct((batch * seq, zdim), BF16),
                   jax.ShapeDtypeStruct((batch, GDN_HEADS, HEAD_DIM, HEAD_DIM), F32)],
        grid=(batch, nt),
        in_specs=[
            pl.BlockSpec((tt, gdim), row),
            pl.BlockSpec((tt, zdim), row),
            pl.BlockSpec((tt, LANES), row),
            pl.BlockSpec((SUBLANES, tt), lambda b, t: (0, b * nt + t)),
            pl.BlockSpec((CONV_W, gdim), const),
            pl.BlockSpec((1, LANES), const),
            pl.BlockSpec((1, LANES), const),
            pl.BlockSpec((SUBLANES, 1), const),
            pl.BlockSpec((SUBLANES, 1), const),
            pl.BlockSpec((1, HEAD_DIM), const),
        ],
        out_specs=[pl.BlockSpec((tt, zdim), row),
                   pl.BlockSpec((None, GDN_HEADS, HEAD_DIM, HEAD_DIM), lambda b, t: (b, 0, 0, 0))],
        scratch_shapes=[pltpu.VMEM((tt + SUBLANES, gdim), F32),
                        pltpu.VMEM((GDN_HEADS, HEAD_DIM, HEAD_DIM), F32)],
        compiler_params=_cparams(("arbitrary", "arbitrary")),
        name="gdn_prompt",
    )(gq, gz, small, small_t, conv_w_l, alog_l, dtb_l, alog_c, dtb_c, ng)


def _gdn_sample_kernel(gq_ref, gz_ref, sm_ref, smt_ref, cst_ref, s0_ref, cw_ref, alog_l, dtb_l, alog_c,
                       dtb_c, ng_ref, o_ref, s_o_ref, stage, uw_scr, vn_scr, oi_scr, *, nb, t_len):
    hd = HEAD_DIM
    nh = GDN_HEADS
    n = nb * t_len
    gdim = gq_ref.shape[1]
    stage[:, 0:SUBLANES, :] = cst_ref[...].reshape(nb, SUBLANES, gdim)
    stage[:, SUBLANES:SUBLANES + t_len, :] = gq_ref[...].reshape(nb, t_len, gdim)
    y = None
    for j in range(CONV_W):
        term = stage[:, pl.ds(SUBLANES - (CONV_W - 1) + j, t_len), :] * cw_ref[j:j + 1, :]
        y = term if y is None else y + term
    y = _silu(y).reshape(n, gdim)

    beta_tile, gc_cols, gc_rows = _gdn_gates(sm_ref[...], smt_ref[...], alog_l[...], dtb_l[...],
                                             alog_c[...], dtb_c[...], t_len)
    masks = _chunk_masks(n, t_len)
    incl = masks[0]
    ng = ng_ref[...]
    ri = lax.broadcasted_iota(I32, (n, n), 0)
    ci = lax.broadcasted_iota(I32, (n, n), 1)
    sh = t_len.bit_length() - 1
    pick_last = jnp.where(((ri >> sh) == (ci >> sh)) & ((ci & (t_len - 1)) == t_len - 1), 1.0, 0.0).astype(BF16)
    g_last_cols = _mm_exact_lhs(pick_last, gc_cols)
    for h in range(nh):
        q = _l2(y[:, h * hd:(h + 1) * hd]) * (hd ** -0.5)
        k = _l2(y[:, (nh + h) * hd:(nh + h + 1) * hd])
        v = y[:, (2 * nh + h) * hd:(2 * nh + h + 1) * hd]
        beta_c = beta_tile[:, GB_LANE + h:GB_LANE + h + 1]
        gc_c = gc_cols[:, GA_LANE + h:GA_LANE + h + 1]
        gc_r = gc_rows[h:h + 1, :]
        u, w, qk = _gdn_intra(q, k, v, beta_c, gc_c, gc_r, masks)
        g_last_c = g_last_cols[:, GA_LANE + h:GA_LANE + h + 1]
        uw_scr[0] = u
        uw_scr[1] = w
        uw_scr[2] = q * jnp.exp(gc_c)
        uw_scr[3] = k * jnp.exp(g_last_c - gc_c)
        uw_scr[4] = jnp.broadcast_to(jnp.exp(g_last_c), (n, hd))

        def seq_step(i, carry):
            r0 = pl.multiple_of(i * t_len, t_len)
            rows = pl.ds(r0, t_len)
            s = s0_ref[i, h]
            vn = uw_scr[0, rows, :] - _mm3(uw_scr[1, rows, :], s)
            oi_scr[rows, :] = _mm3(uw_scr[2, rows, :], s)
            vn_scr[rows, :] = vn
            dec = uw_scr[4, pl.ds(r0, 1), :]
            s_o_ref[i, h] = s * dec + _mm3(uw_scr[3, rows, :], vn, _dot_tn)
            return carry

        lax.fori_loop(0, nb, seq_step, 0)
        o = oi_scr[...] + _mm3(jnp.where(incl, qk, 0.0), vn_scr[...])
        o = _rms(o) * ng * _silu(gz_ref[:, h * hd:(h + 1) * hd])
        o_ref[:, h * hd:(h + 1) * hd] = o.astype(BF16)


def _gdn_sample(gq, gz, small, small_t, cstate, state_gdn, layer, conv_w_l, alog_l, dtb_l, alog_c, dtb_c, ng,
                rows_p, db, t_len):
    nb = min(16, db)
    n = nb * t_len
    base = rows_p // n
    gdim = gq.shape[1]
    zdim = gz.shape[1]
    row = lambda i: (base + i, 0)
    const = lambda i: (0, 0)
    kern = functools.partial(_gdn_sample_kernel, nb=nb, t_len=t_len)
    return pl.pallas_call(
        kern,
        out_shape=[jax.ShapeDtypeStruct((db * t_len, zdim), BF16),
                   jax.ShapeDtypeStruct((db, GDN_HEADS, HEAD_DIM, HEAD_DIM), F32)],
        grid=(db // nb,),
        in_specs=[
            pl.BlockSpec((n, gdim), row),
            pl.BlockSpec((n, zdim), row),
            pl.BlockSpec((n, LANES), row),
            pl.BlockSpec((SUBLANES, n), lambda i: (0, base + i)),
            pl.BlockSpec((n, gdim), lambda i: (i, 0)),
            pl.BlockSpec((nb, None, GDN_HEADS, HEAD_DIM, HEAD_DIM), lambda i: (i, layer, 0, 0, 0)),
            pl.BlockSpec((CONV_W, gdim), const),
            pl.BlockSpec((1, LANES), const),
            pl.BlockSpec((1, LANES), const),
            pl.BlockSpec((SUBLANES, 1), const),
            pl.BlockSpec((SUBLANES, 1), const),
            pl.BlockSpec((1, HEAD_DIM), const),
        ],
        out_specs=[pl.BlockSpec((n, zdim), lambda i: (i, 0)),
                   pl.BlockSpec((nb, GDN_HEADS, HEAD_DIM, HEAD_DIM), lambda i: (i, 0, 0, 0))],
        scratch_shapes=[pltpu.VMEM((nb, 2 * SUBLANES, gdim), F32),
                        pltpu.VMEM((5, n, HEAD_DIM), F32),
                        pltpu.VMEM((n, HEAD_DIM), F32),
                        pltpu.VMEM((n, HEAD_DIM), F32)],
        compiler_params=_cparams(("arbitrary",)),
        name="gdn_sample",
    )(gq, gz, small, small_t, cstate, state_gdn, conv_w_l, alog_l, dtb_l, alog_c, dtb_c, ng)


def _outproj_kernel(x_ref, att_ref, gdn_ref, gate_ref, sh_ref, sc_ref, g_ref, wo_ref, *rest, moe):
    if moe:
        rw_ref, rb_ref, x_o, h_o, lg_o = rest
    else:
        x_o, h_o = rest
    x = x_ref[...]
    adim = att_ref.shape[1]
    y = _dot(att_ref[...], wo_ref[0:adim, :]) + _dot(gdn_ref[...], wo_ref[adim:, :])
    xn = x + gate_ref[...] * y.reshape(x.shape)
    x_o[...] = xn
    h = (_rms(xn) * g_ref[...]) * (1.0 + sc_ref[...]) + sh_ref[...]
    h2 = h.reshape(y.shape)
    h_o[...] = h2.astype(BF16)
    if moe:
        lg_o[...] = _mm3(h2, rw_ref[...]) + rb_ref[...]


def _outproj(x3, att, gdn, gate, sh, sc, g, wo, router, tile_of, tm):
    ng, _, d = x3.shape
    rows = ng * SUBLANES
    gt = tm // SUBLANES
    moe = router is not None
    const = lambda t: (0, 0)
    row = lambda t: (t, 0)
    modspec = pl.BlockSpec((gt, 1, d), lambda t: (tile_of(t), 0, 0))
    in_specs = [
        pl.BlockSpec((gt, SUBLANES, d), lambda t: (t, 0, 0)),
        pl.BlockSpec((tm, att.shape[1]), row),
        pl.BlockSpec((tm, gdn.shape[1]), row),
        modspec, modspec, modspec,
        pl.BlockSpec((1, d), const),
        pl.BlockSpec(wo.shape, const),
    ]
    out_shape = [jax.ShapeDtypeStruct(x3.shape, F32), jax.ShapeDtypeStruct((rows, d), BF16)]
    out_specs = [pl.BlockSpec((gt, SUBLANES, d), lambda t: (t, 0, 0)), pl.BlockSpec((tm, d), row)]
    args = [x3, att, gdn, gate, sh, sc, g, wo]
    if moe:
        in_specs += [pl.BlockSpec((d, LANES), const), pl.BlockSpec((1, LANES), const)]
        out_shape.append(jax.ShapeDtypeStruct((rows, LANES), F32))
        out_specs.append(pl.BlockSpec((tm, LANES), row))
        args += list(router)
    return pl.pallas_call(
        functools.partial(_outproj_kernel, moe=moe),
        out_shape=out_shape,
        grid=(rows // tm,),
        in_specs=in_specs,
        out_specs=out_specs,
        compiler_params=_cparams(("arbitrary",)),
        name="out_proj",
    )(*args)


def _ffn_kernel(x_ref, h_ref, gate_ref, wg_ref, wu_ref, wd_ref, o_ref, *, fc):
    h = h_ref[...]
    dff = wg_ref.shape[1]
    acc = None
    for c in range(dff // fc):
        a = _dot(h, wg_ref[:, c * fc:(c + 1) * fc])
        u = _dot(h, wu_ref[:, c * fc:(c + 1) * fc])
        t = _dot((_silu(a) * u).astype(BF16), wd_ref[c * fc:(c + 1) * fc, :])
        acc = t if acc is None else acc + t
    x = x_ref[...]
    o_ref[...] = x + gate_ref[...] * acc.reshape(x.shape)


def _ffn(x3, hff, gate, wg, wu, wd, tile_of, tm):
    ng, _, d = x3.shape
    rows = ng * SUBLANES
    gt = tm // SUBLANES
    dff = wg.shape[1]
    fc = dff
    for cand in (1408, 1024, 768, 512, 256, 128):
        if dff % cand == 0:
            fc = cand
            break
    const = lambda t: (0, 0)
    return pl.pallas_call(
        functools.partial(_ffn_kernel, fc=fc),
        out_shape=jax.ShapeDtypeStruct(x3.shape, F32),
        grid=(rows // tm,),
        in_specs=[
            pl.BlockSpec((gt, SUBLANES, d), lambda t: (t, 0, 0)),
            pl.BlockSpec((tm, d), lambda t: (t, 0)),
            pl.BlockSpec((gt, 1, d), lambda t: (tile_of(t), 0, 0)),
            pl.BlockSpec(wg.shape, const),
            pl.BlockSpec(wu.shape, const),
            pl.BlockSpec(wd.shape, const),
        ],
        out_specs=pl.BlockSpec((gt, SUBLANES, d), lambda t: (t, 0, 0)),
        compiler_params=_cparams(("arbitrary",)),
        name="ffn_dense",
    )(x3, hff, gate, wg, wu, wd)


def _moe_kernel(x_ref, h_ref, lg_ref, gate_ref, wg_ref, wu_ref, wd_ref, o_ref, acc_scr, g_scr):
    e = pl.program_id(1)
    tm = h_ref.shape[0]

    @pl.when(e == 0)
    def _():
        lane = lax.broadcasted_iota(I32, (tm, LANES), 1)
        lg = jnp.where(lane < N_EXPERTS, lg_ref[...], -jnp.inf)
        m1 = jnp.max(lg, axis=1, keepdims=True)
        i1 = jnp.min(jnp.where(lg == m1, lane, LANES), axis=1, keepdims=True)
        rest = jnp.where(lane == i1, -jnp.inf, lg)
        m2 = jnp.max(rest, axis=1, keepdims=True)
        i2 = jnp.min(jnp.where(rest == m2, lane, LANES), axis=1, keepdims=True)
        e2 = jnp.exp(m2 - m1)
        den = 1.0 + e2
        g_scr[...] = jnp.where(lane == i1, 1.0 / den, 0.0) + jnp.where(lane == i2, e2 / den, 0.0)
        acc_scr[...] = jnp.zeros(acc_scr.shape, F32)

    h = h_ref[...]
    a = _dot(h, wg_ref[...])
    u = _dot(h, wu_ref[...])
    y = _dot((_silu(a) * u).astype(BF16), wd_ref[...])
    lane = lax.broadcasted_iota(I32, (tm, LANES), 1)
    ge = jnp.sum(jnp.where(lane == e, g_scr[...], 0.0), axis=1, keepdims=True)
    acc_scr[...] += ge * y

    @pl.when(e == pl.num_programs(1) - 1)
    def _():
        x = x_ref[...]
        o_ref[...] = x + gate_ref[...] * acc_scr[...].reshape(x.shape)


def _moe(x3, hff, logits, gate, wg, wu, wd, tile_of, tm):
    ng, _, d = x3.shape
    rows = ng * SUBLANES
    gt = tm // SUBLANES
    ne, _, eff = wg.shape
    return pl.pallas_call(
        _moe_kernel,
        out_shape=jax.ShapeDtypeStruct(x3.shape, F32),
        grid=(rows // tm, ne),
        in_specs=[
            pl.BlockSpec((gt, SUBLANES, d), lambda t, e: (t, 0, 0)),
            pl.BlockSpec((tm, d), lambda t, e: (t, 0)),
            pl.BlockSpec((tm, LANES), lambda t, e: (t, 0)),
            pl.BlockSpec((gt, 1, d), lambda t, e: (tile_of(t), 0, 0)),
            pl.BlockSpec((None, d, eff), lambda t, e: (e, 0, 0)),
            pl.BlockSpec((None, d, eff), lambda t, e: (e, 0, 0)),
            pl.BlockSpec((None, eff, d), lambda t, e: (e, 0, 0)),
        ],
        out_specs=pl.BlockSpec((gt, SUBLANES, d), lambda t, e: (t, 0, 0)),
        scratch_shapes=[pltpu.VMEM((tm, d), F32), pltpu.VMEM((tm, LANES), F32)],
        compiler_params=_cparams(("arbitrary", "arbitrary")),
        name="ffn_moe",
    )(x3, hff, logits, gate, wg, wu, wd)


def _rope_tables(pos):
    pos = pos.astype(F32)[:, None]
    half = HEAD_DIM // 2
    ang = pos * (ROPE_THETA ** (-jnp.arange(half, dtype=F32) / half))[None, :]
    c, s = jnp.cos(ang), jnp.sin(ang)
    cq = jnp.concatenate([c, c], axis=1)
    sq = jnp.concatenate([-s, s], axis=1)
    half = IDX_DIM // 2
    ang = pos * (ROPE_THETA ** (-jnp.arange(half, dtype=F32) / half))[None, :]
    c, s = jnp.cos(ang), jnp.sin(ang)
    z = jnp.zeros_like(s)
    ci = jnp.concatenate([c, c, c, c], axis=1)
    sa = jnp.concatenate([-s, z, -s, z], axis=1)
    sb = jnp.concatenate([z, s, z, s], axis=1)
    return cq, sq, ci, sa, sb


def _lane_vec(vals, lane0):
    return jnp.zeros((1, LANES), F32).at[0, lane0:lane0 + vals.shape[0]].set(vals)


def kernel(x_prompt, x_sample, cache_k, cache_v, cache_kidx, state_gdn, state_conv, page_table, c_prompt, c_sample, mod_w, mod_b, norm_mix_g, norm_ffn_g, w_in, q_norm_g, k_norm_g, conv_w, a_log, dt_bias, gdn_norm_g, w_out, ffn_w_gate, ffn_w_up, ffn_w_down, router_w, router_b, moe_w_gate, moe_w_up, moe_w_down):
    batch, seq, d = x_prompt.shape
    db, t_len, _ = x_sample.shape
    n_layers = mod_w.shape[0]
    n_pages = page_table.shape[1]
    past = n_pages * PAGE
    rows_p, rows_s = batch * seq, db * t_len
    rows = rows_p + rows_s
    assert t_len == SUBLANES and seq % LANES == 0
    tm = min(256, rows_s, seq)
    assert seq % tm == 0 and rows_s % tm == 0
    gt = tm // SUBLANES
    tiles_per_batch = seq // tm
    n_ptiles = rows_p // tm

    def tile_of(t):
        return jnp.where(t < n_ptiles, t // tiles_per_batch, batch + (t - n_ptiles))

    def tab_of(t):
        return jnp.where(t < n_ptiles, t % tiles_per_batch, tiles_per_batch)

    def groups(m):
        mp = jnp.repeat(m[:batch], gt, axis=0)
        return jnp.concatenate([mp, m[batch:batch + db]], axis=0)[:, None, :]

    x3 = jnp.concatenate([x_prompt.reshape(rows_p, d), x_sample.reshape(rows_s, d)], axis=0)
    x3 = x3.reshape(rows // SUBLANES, SUBLANES, d)

    n_c = batch + db
    c_all = jnp.concatenate([c_prompt, c_sample, jnp.zeros((-n_c % SUBLANES, d), F32)], axis=0)
    mods = _mods(c_all, mod_w, mod_b)

    pos = jnp.concatenate([jnp.arange(seq), past + (jnp.arange(tm) % t_len)])
    tabs = _rope_tables(pos)

    kvd = KV_HEADS * HEAD_DIM
    depth = cache_k.shape[1]
    cache_k2 = cache_k.reshape(-1, HEAD_DIM)
    cache_v2 = cache_v.reshape(-1, HEAD_DIM)
    cache_kidx_t = jnp.swapaxes(cache_kidx, 2, 3)
    n_sel_s = min(TOPK_MAX, (past + t_len) // 4)
    lp = (n_pages + 1) * PAGE

    outs = {name: [] for name in ("kp", "vp", "kip", "ks", "vs", "kis", "sp", "ss", "cp", "cs")}
    col = 0
    offs = []
    for size in (ATTN_HEADS * HEAD_DIM, kvd, kvd, IDX_HEADS * IDX_DIM, IDX_DIM, IDX_HEADS,
                 GDN_HEADS * 3 * HEAD_DIM, GDN_HEADS * HEAD_DIM, GDN_HEADS, GDN_HEADS):
        offs.append((col, col + size))
        col += size
    o_q, o_k, o_v, o_qi, o_ki, o_wi, o_gq, o_gz, o_ga, o_gb = offs

    for l in range(n_layers):
        m6 = [groups(mods[l, :, j * d:(j + 1) * d]) for j in range(6)]
        w = w_in[l]
        wa = w[:, o_q[0]:o_v[1]].astype(BF16)
        qi_w = w[:, o_qi[0]:o_qi[1]].reshape(d, IDX_HEADS, 1, IDX_DIM)
        qi_w = jnp.broadcast_to(qi_w, (d, IDX_HEADS, 2, IDX_DIM)).reshape(d, IDX_HEADS * LANES)
        ki_w = w[:, o_ki[0]:o_ki[1]]
        misc_w = jnp.concatenate([w[:, o_wi[0]:o_wi[1]], w[:, o_ga[0]:o_ga[1]], w[:, o_gb[0]:o_gb[1]],
                                  jnp.zeros((d, LANES - IDX_HEADS - 2 * GDN_HEADS), F32)], axis=1)
        wi_f = jnp.concatenate([qi_w, ki_w, ki_w, misc_w], axis=1)
        wih = wi_f.astype(BF16)
        wil = (wi_f - wih.astype(F32)).astype(BF16)
        wg = w[:, o_gq[0]:o_gz[1]].astype(BF16)

        (q_bf, k_f, v_f, k_bf, v_bf, qi3, ki_f, ki3, small, gq, gz) = _inproj(
            x3, m6[0], m6[1], norm_mix_g[l][None, :], wa, wih, wil, wg,
            q_norm_g[l][None, :], k_norm_g[l][None, :], tabs, tile_of, tab_of, tm)

        small_t = jnp.transpose(small[:, GA_LANE:GA_LANE + 2 * GDN_HEADS])
        alog_l = _lane_vec(a_log[l], GA_LANE)
        dtb_l = _lane_vec(dt_bias[l], GA_LANE)
        pad4 = jnp.zeros((SUBLANES - GDN_HEADS,), F32)
        alog_c = jnp.concatenate([a_log[l], pad4])[:, None]
        dtb_c = jnp.concatenate([dt_bias[l], pad4])[:, None]
        ng = gdn_norm_g[l][None, :]

        att_p = _dsa_prompt(qi3, small, q_bf, ki3, k_bf, v_bf, batch, seq)
        gdn_p, s_p = _gdn_prompt(gq, gz, small, small_t, conv_w[l], alog_l, dtb_l, alog_c, dtb_c, ng,
                                 batch, seq)

        scores = _sample_scores(page_table, qi3, small, ki3, cache_kidx_t, l, rows_p, t_len)
        bias = _sample_select(scores.reshape(rows_s, lp), n_sel_s)
        att_s = _sample_attend(page_table, q_bf, bias, k_bf, v_bf, cache_k2, cache_v2, depth, l, rows_p, t_len)
        cstate = jnp.pad(state_conv[:, l], ((0, 0), (SUBLANES - (CONV_W - 1), 0), (0, 0)))
        cstate = cstate.reshape(db * SUBLANES, cstate.shape[2])
        gdn_s, s_s = _gdn_sample(gq, gz, small, small_t, cstate, state_gdn, l, conv_w[l], alog_l, dtb_l,
                                 alog_c, dtb_c, ng, rows_p, db, t_len)

        att = jnp.concatenate([att_p, att_s], axis=0)
        gdn = jnp.concatenate([gdn_p, gdn_s], axis=0)
        i = l // 2
        router = None
        if l % 2 == 1:
            rw = jnp.concatenate([router_w[i], jnp.zeros((d, LANES - N_EXPERTS), F32)], axis=1)
            rb = jnp.concatenate([router_b[i], jnp.zeros((LANES - N_EXPERTS,), F32)])[None, :]
            router = (rw, rb)
        res = _outproj(x3, att, gdn, m6[2], m6[3], m6[4], norm_ffn_g[l][None, :], w_out[l].astype(BF16),
                       router, tile_of, tm)
        if l % 2 == 0:
            x3, hff = res
            x3 = _ffn(x3, hff, m6[5], ffn_w_gate[i].astype(BF16), ffn_w_up[i].astype(BF16),
                      ffn_w_down[i].astype(BF16), tile_of, tm)
        else:
            x3, hff, logits = res
            x3 = _moe(x3, hff, logits, m6[5], moe_w_gate[i].astype(BF16), moe_w_up[i].astype(BF16),
                      moe_w_down[i].astype(BF16), tile_of, tm)

        outs["kp"].append(k_f[:rows_p].reshape(batch, seq, KV_HEADS, HEAD_DIM))
        outs["vp"].append(v_f[:rows_p].reshape(batch, seq, KV_HEADS, HEAD_DIM))
        outs["kip"].append(ki_f[:rows_p].reshape(batch, seq, IDX_DIM))
        outs["ks"].append(k_f[rows_p:].reshape(db, t_len, KV_HEADS, HEAD_DIM))
        outs["vs"].append(v_f[rows_p:].reshape(db, t_len, KV_HEADS, HEAD_DIM))
        outs["kis"].append(ki_f[rows_p:].reshape(db, t_len, IDX_DIM))
        outs["sp"].append(s_p)
        outs["ss"].append(s_s)
        gq_p = gq[:rows_p].reshape(batch, seq, -1)
        gq_s = gq[rows_p:].reshape(db, t_len, -1)
        outs["cp"].append(gq_p[:, seq - (CONV_W - 1):])
        outs["cs"].append(gq_s[:, t_len - (CONV_W - 1):])

    x2 = x3.reshape(rows, d)
    st = lambda name: jnp.stack(outs[name], axis=1)
    return (x2[:rows_p].reshape(batch, seq, d), x2[rows_p:].reshape(db, t_len, d),
            st("kp"), st("vp"), st("kip"), st("ks"), st("vs"), st("kis"),
            st("sp"), st("ss"), st("cp"), st("cs"))
```

```python
import functools
import math

import jax
import jax.numpy as jnp
from jax import lax
from jax.experimental import pallas as pl
from jax.experimental.pallas import tpu as pltpu

F32 = jnp.float32
BF16 = jnp.bfloat16
I32 = jnp.int32

HEAD_DIM = 128
ATTN_HEADS = 4
KV_HEADS = 2
GROUP = ATTN_HEADS // KV_HEADS
IDX_HEADS = 8
IDX_DIM = 64
TOPK_MAX = 256
GDN_HEADS = 4
GDN_CHUNK = 64
CONV_W = 4
N_EXPERTS = 8
PAGE = 128
ROPE_THETA = 10000.0
EPS = 1e-6

LANES = 128
SUBLANES = 8
VMEM_LIMIT = 56 * 1024 * 1024
NEG = -1e30
INT_MIN = -2147483648
WI_LANE = 0
GA_LANE = 8
GB_LANE = 12


def _cparams(sem):
    return pltpu.CompilerParams(dimension_semantics=sem, vmem_limit_bytes=VMEM_LIMIT)


def _dot(a, b):
    return jnp.dot(a, b, preferred_element_type=F32)


def _dot_nt(a, b):
    return lax.dot_general(a, b, (((1,), (1,)), ((), ())), preferred_element_type=F32)


def _dot_tn(a, b):
    return lax.dot_general(a, b, (((0,), (0,)), ((), ())), preferred_element_type=F32)


def _split(x):
    hi = x.astype(BF16)
    lo = (x - hi.astype(F32)).astype(BF16)
    return hi, lo


def _mm3(a, b, dot=_dot):
    ah, al = _split(a)
    bh, bl = _split(b)
    return dot(ah, bh) + (dot(ah, bl) + dot(al, bh))


def _split_three(a):
    a1 = a.astype(BF16)
    r1 = a - a1.astype(F32)
    a2 = r1.astype(BF16)
    a3 = (r1 - a2.astype(F32)).astype(BF16)
    return a1, a2, a3


def _mm_exact_rhs(a, b_bf16):
    a1, a2, a3 = _split_three(a)
    return _dot(a1, b_bf16) + (_dot(a2, b_bf16) + _dot(a3, b_bf16))


def _mm_exact_lhs(m_bf16, a):
    a1, a2, a3 = _split_three(a)
    return _dot(m_bf16, a1) + (_dot(m_bf16, a2) + _dot(m_bf16, a3))


def _silu(x):
    return x * jax.nn.sigmoid(x)


def _softplus(x):
    return jnp.maximum(x, 0.0) + jnp.log1p(jnp.exp(-jnp.abs(x)))


def _mods_kernel(c_ref, w_ref, b_ref, o_ref):
    o_ref[...] = _mm3(_silu(c_ref[...]), w_ref[...]) + b_ref[...]


def _mods(c_all, mod_w, mod_b):
    n_layers, d, n6 = mod_w.shape
    rows = c_all.shape[0]
    tn = n6 // 4
    return pl.pallas_call(
        _mods_kernel,
        out_shape=jax.ShapeDtypeStruct((n_layers, rows, n6), F32),
        grid=(n_layers, n6 // tn),
        in_specs=[
            pl.BlockSpec((rows, d), lambda l, j: (0, 0)),
            pl.BlockSpec((None, d, tn), lambda l, j: (l, 0, j)),
            pl.BlockSpec((None, 1, tn), lambda l, j: (l, 0, j)),
        ],
        out_specs=pl.BlockSpec((None, rows, tn), lambda l, j: (l, 0, j)),
        compiler_params=_cparams(("arbitrary", "arbitrary")),
        name="ada_mods",
    )(c_all, mod_w, mod_b.reshape(n_layers, 1, n6))


def _rms(x):
    return x * lax.rsqrt(jnp.mean(x * x, axis=-1, keepdims=True) + EPS)


def _inproj_kernel(x_ref, sh_ref, sc_ref, g_ref, wa_ref, wih_ref, wil_ref, wg_ref, qg_ref, kg_ref,
                   cq_ref, sq_ref, ci_ref, sa_ref, sb_ref,
                   q_o, k_o, v_o, kb_o, vb_o, qi3_o, ki_o, ki3_o, small_o, gq_o, gz_o):
    x = x_ref[...]
    h = (_rms(x) * g_ref[...]) * (1.0 + sc_ref[...]) + sh_ref[...]
    tm = x.shape[0] * x.shape[1]
    h2 = h.reshape(tm, x.shape[2])
    hb, hl = _split(h2)
    za = _dot(hb, wa_ref[...])
    wih = wih_ref[...]
    zi = _dot(hb, wih) + (_dot(hb, wil_ref[...]) + _dot(hl, wih))
    zg = _dot(hb, wg_ref[...])

    cq, sq = cq_ref[...], sq_ref[...]
    qg, kg = qg_ref[...], kg_ref[...]
    for hh in range(ATTN_HEADS):
        qn = _rms(za[:, hh * HEAD_DIM:(hh + 1) * HEAD_DIM]) * qg
        qr = qn * cq + pltpu.roll(qn, HEAD_DIM // 2, 1) * sq
        q_o[:, hh * HEAD_DIM:(hh + 1) * HEAD_DIM] = (qr * (HEAD_DIM ** -0.5)).astype(BF16)
    koff = ATTN_HEADS * HEAD_DIM
    for hh in range(KV_HEADS):
        kn = _rms(za[:, koff + hh * HEAD_DIM:koff + (hh + 1) * HEAD_DIM]) * kg
        kr = kn * cq + pltpu.roll(kn, HEAD_DIM // 2, 1) * sq
        k_o[:, hh * HEAD_DIM:(hh + 1) * HEAD_DIM] = kr
        kb_o[:, hh * HEAD_DIM:(hh + 1) * HEAD_DIM] = kr.astype(BF16)
    voff = koff + KV_HEADS * HEAD_DIM
    v = za[:, voff:voff + KV_HEADS * HEAD_DIM]
    v_o[...] = v
    vb_o[...] = v.astype(BF16)

    ci, sa, sb = ci_ref[...], sa_ref[...], sb_ref[...]
    first = lax.broadcasted_iota(I32, (tm, LANES), 1) < IDX_DIM

    def rope64(t):
        return t * ci + pltpu.roll(t, LANES - IDX_DIM // 2, 1) * sa + pltpu.roll(t, IDX_DIM // 2, 1) * sb

    for hh in range(IDX_HEADS):
        r = rope64(zi[:, hh * LANES:(hh + 1) * LANES]) * (IDX_DIM ** -0.5)
        hi = r.astype(BF16).astype(F32)
        qi3_o[hh, :, 0:LANES] = jnp.where(first, hi, r - hi).astype(BF16)
        qi3_o[hh, :, LANES:2 * LANES] = jnp.where(first, hi, 0.0).astype(BF16)
    r = rope64(zi[:, IDX_HEADS * LANES:(IDX_HEADS + 1) * LANES])
    ki_o[...] = r[:, :IDX_DIM]
    hi = r.astype(BF16).astype(F32)
    ki3_o[:, 0:LANES] = hi.astype(BF16)
    ki3_o[:, LANES:2 * LANES] = jnp.where(first, r - hi, 0.0).astype(BF16)
    misc = zi[:, (IDX_HEADS + 1) * LANES:(IDX_HEADS + 2) * LANES]
    lane = lax.broadcasted_iota(I32, (tm, LANES), 1)
    small_o[...] = jnp.where(lane < IDX_HEADS, misc * (IDX_HEADS ** -0.5), misc)
    gdim = gq_o.shape[1]
    gq_o[...] = zg[:, :gdim]
    gz_o[...] = zg[:, gdim:]


def _inproj(x3, sh, sc, g, wa, wih, wil, wg, qg, kg, tabs, tile_of, tab_of, tm):
    ng, _, d = x3.shape
    rows = ng * SUBLANES
    gt = tm // SUBLANES
    nt = rows // tm
    gdim = GDN_HEADS * 3 * HEAD_DIM
    zdim = GDN_HEADS * HEAD_DIM
    const = lambda t: (0, 0)
    row = lambda t: (t, 0)
    tab = lambda t: (tab_of(t), 0)
    in_specs = [
        pl.BlockSpec((gt, SUBLANES, d), lambda t: (t, 0, 0)),
        pl.BlockSpec((gt, 1, d), lambda t: (tile_of(t), 0, 0)),
        pl.BlockSpec((gt, 1, d), lambda t: (tile_of(t), 0, 0)),
        pl.BlockSpec((1, d), const),
        pl.BlockSpec(wa.shape, const),
        pl.BlockSpec(wih.shape, const),
        pl.BlockSpec(wil.shape, const),
        pl.BlockSpec(wg.shape, const),
        pl.BlockSpec((1, HEAD_DIM), const),
        pl.BlockSpec((1, HEAD_DIM), const),
    ] + [pl.BlockSpec((tm, LANES), tab)] * 5
    kvd = KV_HEADS * HEAD_DIM
    out_shape = [
        jax.ShapeDtypeStruct((rows, ATTN_HEADS * HEAD_DIM), BF16),
        jax.ShapeDtypeStruct((rows, kvd), F32),
        jax.ShapeDtypeStruct((rows, kvd), F32),
        jax.ShapeDtypeStruct((rows, kvd), BF16),
        jax.ShapeDtypeStruct((rows, kvd), BF16),
        jax.ShapeDtypeStruct((IDX_HEADS, rows, 2 * LANES), BF16),
        jax.ShapeDtypeStruct((rows, IDX_DIM), F32),
        jax.ShapeDtypeStruct((rows, 2 * LANES), BF16),
        jax.ShapeDtypeStruct((rows, LANES), F32),
        jax.ShapeDtypeStruct((rows, gdim), F32),
        jax.ShapeDtypeStruct((rows, zdim), F32),
    ]
    out_specs = [
        pl.BlockSpec((tm, ATTN_HEADS * HEAD_DIM), row),
        pl.BlockSpec((tm, kvd), row),
        pl.BlockSpec((tm, kvd), row),
        pl.BlockSpec((tm, kvd), row),
        pl.BlockSpec((tm, kvd), row),
        pl.BlockSpec((IDX_HEADS, tm, 2 * LANES), lambda t: (0, t, 0)),
        pl.BlockSpec((tm, IDX_DIM), row),
        pl.BlockSpec((tm, 2 * LANES), row),
        pl.BlockSpec((tm, LANES), row),
        pl.BlockSpec((tm, gdim), row),
        pl.BlockSpec((tm, zdim), row),
    ]
    return pl.pallas_call(
        _inproj_kernel,
        out_shape=out_shape,
        grid=(nt,),
        in_specs=in_specs,
        out_specs=out_specs,
        compiler_params=_cparams(("arbitrary",)),
        name="in_proj",
    )(x3, sh, sc, g, wa, wih, wil, wg, qg, kg, *tabs)


def _sort_key(score):
    bits = pltpu.bitcast(score, I32)
    return jnp.where(bits < 0, bits ^ jnp.int32(0x7FFFFFFF), bits)


def _count(key_scr, nkc, kc, rows, n, preds):
    def body(c, accs):
        base = pl.multiple_of(c * kc, kc)
        accs = list(accs)
        for j in range(kc // LANES):
            sc = key_scr[:, pl.ds(base + j * LANES, LANES)]
            ps = preds(sc, base + j * LANES)
            for i in range(n):
                accs[i] = accs[i] + jnp.where(ps[i], 1.0, 0.0)
        return tuple(accs)

    accs = lax.fori_loop(0, nkc, body, tuple(jnp.zeros((rows, LANES), F32) for _ in range(n)))
    return [jnp.sum(a, axis=1, keepdims=True) for a in accs]


def _select_threshold(key_scr, nkc, kc, rows, n_sel, idx_bits):
    n_sel_f = float(n_sel)

    def bit_step(b, acc):
        cand = acc | (jnp.int32(1) << (31 - b))
        cand_s = jnp.broadcast_to(cand ^ jnp.int32(INT_MIN), (rows, LANES))
        cnt, = _count(key_scr, nkc, kc, rows, 1, lambda kk, c0: (kk >= cand_s,))
        return jnp.where(cnt >= n_sel_f, cand, acc)

    acc = lax.fori_loop(0, 32, bit_step, jnp.zeros((rows, 1), I32))
    thr = acc ^ jnp.int32(INT_MIN)
    thr_b = jnp.broadcast_to(thr, (rows, LANES))
    cnt_gt, cnt_eq = _count(key_scr, nkc, kc, rows, 2, lambda kk, c0: (kk > thr_b, kk == thr_b))
    need = n_sel_f - cnt_gt
    excess = (acc != 0) & (cnt_eq > need)
    any_excess = jnp.max(jnp.where(excess, 1.0, 0.0)) > 0.0
    lane = lax.broadcasted_iota(I32, (rows, LANES), 1)

    @pl.when(any_excess)
    def _():
        def idx_step(b, p):
            cand = p | (jnp.int32(1) << (idx_bits - 1 - b))
            cand_b = jnp.broadcast_to(cand, (rows, LANES))
            cnt, = _count(key_scr, nkc, kc, rows, 1,
                          lambda kk, c0: ((kk == thr_b) & ((lane + c0) < cand_b),))
            return jnp.where(cnt < need, cand, p)

        cut = lax.fori_loop(0, idx_bits, idx_step, jnp.zeros((rows, 1), I32))
        cut_b = jnp.broadcast_to(cut, (rows, LANES))
        drop_row = jnp.broadcast_to(excess, (rows, LANES))

        def drop(c, carry):
            base = pl.multiple_of(c * kc, kc)
            for j in range(kc // LANES):
                sl = pl.ds(base + j * LANES, LANES)
                kk = key_scr[:, sl]
                kill = drop_row & (kk == thr_b) & ((lane + (base + j * LANES)) > cut_b)
                key_scr[:, sl] = jnp.where(kill, jnp.int32(INT_MIN), kk)
            return carry

        lax.fori_loop(0, nkc, drop, 0)

    return jnp.maximum(thr, jnp.int32(INT_MIN + 1))


def _dsa_prompt_kernel(qi3_ref, small_ref, q_ref, ki3_ref, k_ref, v_ref, o_ref, key_scr, w_scr,
                       *, tq, kc, n_sel, idx_bits):
    i = pl.program_id(1)
    nkc = (i * tq + tq + kc - 1) // kc
    q3 = qi3_ref[...].reshape(IDX_HEADS * tq, 2 * LANES)
    wi = small_ref[:, WI_LANE:WI_LANE + IDX_HEADS]
    for h in range(IDX_HEADS):
        w_scr[h] = jnp.broadcast_to(wi[:, h:h + 1], (tq, LANES))
    row = i * tq + lax.broadcasted_iota(I32, (tq, LANES), 0)
    lane = lax.broadcasted_iota(I32, (tq, LANES), 1)

    def score_chunk(c, carry):
        base = pl.multiple_of(c * kc, kc)
        s = _dot_nt(q3, ki3_ref[pl.ds(base, kc), :])
        for j in range(kc // LANES):
            acc = None
            for h in range(IDX_HEADS):
                t = w_scr[h] * jnp.maximum(s[h * tq:(h + 1) * tq, j * LANES:(j + 1) * LANES], 0.0)
                acc = t if acc is None else acc + t
            valid = (lane + (base + j * LANES)) <= row
            key_scr[:, pl.ds(base + j * LANES, LANES)] = jnp.where(valid, _sort_key(acc), jnp.int32(INT_MIN))
        return carry

    lax.fori_loop(0, nkc, score_chunk, 0)
    thr = _select_threshold(key_scr, nkc, kc, tq, n_sel, idx_bits)
    thr_b = jnp.broadcast_to(thr, (tq, LANES))

    qs = []
    for g in range(KV_HEADS):
        qs.append(jnp.concatenate(
            [q_ref[:, (g * GROUP + a) * HEAD_DIM:(g * GROUP + a + 1) * HEAD_DIM] for a in range(GROUP)], axis=0))

    def attend_chunk(c, carry):
        base = pl.multiple_of(c * kc, kc)
        biases = []
        for j in range(kc // LANES):
            sc = key_scr[:, pl.ds(base + j * LANES, LANES)]
            biases.append(jnp.where(sc >= thr_b, 0.0, NEG))
        bias = jnp.concatenate(biases, axis=1)
        bias = jnp.concatenate([bias] * GROUP, axis=0)
        new = []
        for g in range(KV_HEADS):
            m, l, acc = carry[g]
            kg = k_ref[pl.ds(base, kc), g * HEAD_DIM:(g + 1) * HEAD_DIM]
            vg = v_ref[pl.ds(base, kc), g * HEAD_DIM:(g + 1) * HEAD_DIM]
            s = _dot_nt(qs[g], kg) + bias
            m_new = jnp.maximum(m, jnp.max(s, axis=1, keepdims=True))
            alpha = jnp.exp(m - m_new)
            p = jnp.exp(s - m_new)
            l = alpha * l + jnp.sum(p, axis=1, keepdims=True)
            acc = alpha * acc + _dot(p.astype(BF16), vg)
            new.append((m_new, l, acc))
        return tuple(new)

    init = tuple((jnp.full((GROUP * tq, 1), NEG, F32), jnp.zeros((GROUP * tq, 1), F32),
                  jnp.zeros((GROUP * tq, HEAD_DIM), F32)) for _ in range(KV_HEADS))
    res = lax.fori_loop(0, nkc, attend_chunk, init)
    for g in range(KV_HEADS):
        _, l, acc = res[g]
        o = acc / l
        for a in range(GROUP):
            hh = g * GROUP + a
            o_ref[:, hh * HEAD_DIM:(hh + 1) * HEAD_DIM] = o[a * tq:(a + 1) * tq].astype(BF16)


def _dsa_prompt(qi3, small, q_bf, ki3, k_bf, v_bf, batch, seq):
    tq = LANES
    kc = min(512, seq)
    nq = seq // tq
    n_sel = min(TOPK_MAX, seq // 4)
    idx_bits = max(1, (seq - 1).bit_length())
    kvd = KV_HEADS * HEAD_DIM
    qrow = lambda b, i: (b * nq + i, 0)
    kern = functools.partial(_dsa_prompt_kernel, tq=tq, kc=kc, n_sel=n_sel, idx_bits=idx_bits)
    return pl.pallas_call(
        kern,
        out_shape=jax.ShapeDtypeStruct((batch * seq, ATTN_HEADS * HEAD_DIM), BF16),
        grid=(batch, nq),
        in_specs=[
            pl.BlockSpec((IDX_HEADS, tq, 2 * LANES), lambda b, i: (0, b * nq + i, 0)),
            pl.BlockSpec((tq, LANES), qrow),
            pl.BlockSpec((tq, ATTN_HEADS * HEAD_DIM), qrow),
            pl.BlockSpec((seq, 2 * LANES), lambda b, i: (b, 0)),
            pl.BlockSpec((seq, kvd), lambda b, i: (b, 0)),
            pl.BlockSpec((seq, kvd), lambda b, i: (b, 0)),
        ],
        out_specs=pl.BlockSpec((tq, ATTN_HEADS * HEAD_DIM), qrow),
        scratch_shapes=[pltpu.VMEM((tq, seq), I32), pltpu.VMEM((IDX_HEADS, tq, LANES), F32)],
        compiler_params=_cparams(("arbitrary", "arbitrary")),
        name="dsa_prompt",
    )(qi3, small, q_bf, ki3, k_bf, v_bf)


SEQ_PER_STEP = 2


def _sample_scores_kernel(pt_ref, qi3_ref, small_ref, ki3n_ref, *rest, n_pages, t_len):
    pages = rest[:SEQ_PER_STEP * n_pages]
    s_o = rest[SEQ_PER_STEP * n_pages]
    nr = SEQ_PER_STEP * t_len
    q3 = qi3_ref[...].reshape(IDX_HEADS * nr, 2 * LANES)
    wi = small_ref[:, WI_LANE:WI_LANE + IDX_HEADS]
    knew = jnp.concatenate([ki3n_ref[...], jnp.zeros((PAGE - nr, 2 * LANES), BF16)], axis=0)
    zpad = jnp.zeros((2 * LANES - 3 * IDX_DIM, PAGE), F32)
    lane = lax.broadcasted_iota(I32, (t_len, LANES), 1)
    trow = lax.broadcasted_iota(I32, (t_len, LANES), 0)

    def head_sum(s, j):
        acc = None
        for h in range(IDX_HEADS):
            r0 = h * nr + j * t_len
            t = wi[j * t_len:(j + 1) * t_len, h:h + 1] * jnp.maximum(s[r0:r0 + t_len], 0.0)
            acc = t if acc is None else acc + t
        return acc

    for j in range(SEQ_PER_STEP):
        for p in range(n_pages):
            kp = pages[j * n_pages + p][...]
            hi = kp.astype(BF16).astype(F32)
            k3 = jnp.concatenate([hi, hi, kp - hi, zpad], axis=0).astype(BF16)
            s_o[j, :, p * PAGE:(p + 1) * PAGE] = head_sum(_dot(q3, k3), j)
        sn = head_sum(_dot_nt(q3, knew), j)
        ok = (lane >= j * t_len) & (lane - j * t_len <= trow)
        s_o[j, :, n_pages * PAGE:(n_pages + 1) * PAGE] = jnp.where(ok, sn, -jnp.inf)


def _sample_scores(page_table, qi3, small, ki3, cache_kidx_t, layer, rows_p, t_len):
    db, n_pages = page_table.shape
    nr = SEQ_PER_STEP * t_len
    base = rows_p // nr
    lp = (n_pages + 1) * PAGE
    in_specs = [
        pl.BlockSpec((IDX_HEADS, nr, 2 * LANES), lambda n, pt: (0, base + n, 0)),
        pl.BlockSpec((nr, LANES), lambda n, pt: (base + n, 0)),
        pl.BlockSpec((nr, 2 * LANES), lambda n, pt: (base + n, 0)),
    ]
    for j in range(SEQ_PER_STEP):
        for p in range(n_pages):
            in_specs.append(pl.BlockSpec(
                (None, None, IDX_DIM, PAGE),
                lambda n, pt, j=j, p=p: (pt[n * SEQ_PER_STEP + j, p], layer, 0, 0)))
    kern = functools.partial(_sample_scores_kernel, n_pages=n_pages, t_len=t_len)
    return pl.pallas_call(
        kern,
        out_shape=jax.ShapeDtypeStruct((db, t_len, lp), F32),
        grid_spec=pltpu.PrefetchScalarGridSpec(
            num_scalar_prefetch=1,
            grid=(db // SEQ_PER_STEP,),
            in_specs=in_specs,
            out_specs=pl.BlockSpec((SEQ_PER_STEP, t_len, lp), lambda n, pt: (n, 0, 0)),
        ),
        compiler_params=_cparams(("arbitrary",)),
        name="dsa_sample_scores",
    )(page_table, qi3, small, ki3, *([cache_kidx_t] * (SEQ_PER_STEP * n_pages)))


def _sample_select_kernel(s_ref, b_ref, key_scr, *, n_sel, idx_bits):
    rows, lp = s_ref.shape
    nkc = lp // LANES
    for c in range(nkc):
        sc = s_ref[:, c * LANES:(c + 1) * LANES]
        key_scr[:, c * LANES:(c + 1) * LANES] = jnp.where(sc > -jnp.inf, _sort_key(sc), jnp.int32(INT_MIN))
    thr = _select_threshold(key_scr, nkc, LANES, rows, n_sel, idx_bits)
    thr_b = jnp.broadcast_to(thr, (rows, LANES))
    for c in range(nkc):
        b_ref[:, c * LANES:(c + 1) * LANES] = jnp.where(key_scr[:, c * LANES:(c + 1) * LANES] >= thr_b, 0.0, NEG)


def _sample_select(scores2d, n_sel):
    rows, lp = scores2d.shape
    tr = min(LANES, rows)
    kern = functools.partial(_sample_select_kernel, n_sel=n_sel, idx_bits=max(1, (lp - 1).bit_length()))
    return pl.pallas_call(
        kern,
        out_shape=jax.ShapeDtypeStruct((rows, lp), F32),
        grid=(rows // tr,),
        in_specs=[pl.BlockSpec((tr, lp), lambda r: (r, 0))],
        out_specs=pl.BlockSpec((tr, lp), lambda r: (r, 0)),
        scratch_shapes=[pltpu.VMEM((tr, lp), I32)],
        compiler_params=_cparams(("arbitrary",)),
        name="dsa_sample_select",
    )(scores2d)


def _sample_attend_kernel(pt_ref, q_ref, bias_ref, kn_ref, vn_ref, *rest, n_pages, t_len):
    npg = SEQ_PER_STEP * n_pages
    kpages, vpages = rest[:npg], rest[npg:2 * npg]
    o_ref = rest[2 * npg]
    kc_scr, vc_scr = rest[2 * npg + 1], rest[2 * npg + 2]
    nr = SEQ_PER_STEP * t_len
    lp = (n_pages + 1) * PAGE
    kvd = KV_HEADS * HEAD_DIM
    bias = jnp.concatenate([bias_ref[...]] * GROUP, axis=0)
    pad = jnp.zeros((PAGE - nr, kvd), BF16)
    kc_scr[n_pages * PAGE:lp, :] = jnp.concatenate([kn_ref[...], pad], axis=0)
    vc_scr[n_pages * PAGE:lp, :] = jnp.concatenate([vn_ref[...], pad], axis=0)
    for j in range(SEQ_PER_STEP):
        for p in range(n_pages):
            for g in range(KV_HEADS):
                head_rows = pl.ds(g, PAGE, stride=KV_HEADS)
                cols = slice(g * HEAD_DIM, (g + 1) * HEAD_DIM)
                kc_scr[p * PAGE:(p + 1) * PAGE, cols] = kpages[j * n_pages + p][head_rows, :].astype(BF16)
                vc_scr[p * PAGE:(p + 1) * PAGE, cols] = vpages[j * n_pages + p][head_rows, :].astype(BF16)
        for g in range(KV_HEADS):
            qs = jnp.concatenate(
                [q_ref[:, (g * GROUP + a) * HEAD_DIM:(g * GROUP + a + 1) * HEAD_DIM] for a in range(GROUP)],
                axis=0)
            s = _dot_nt(qs, kc_scr[:, g * HEAD_DIM:(g + 1) * HEAD_DIM]) + bias
            m = jnp.max(s, axis=1, keepdims=True)
            p_ = jnp.exp(s - m)
            l = jnp.sum(p_, axis=1, keepdims=True)
            o = _dot(p_.astype(BF16), vc_scr[:, g * HEAD_DIM:(g + 1) * HEAD_DIM]) / l
            for a in range(GROUP):
                hh = g * GROUP + a
                r0 = a * nr + j * t_len
                o_ref[j * t_len:(j + 1) * t_len, hh * HEAD_DIM:(hh + 1) * HEAD_DIM] = (
                    o[r0:r0 + t_len].astype(BF16))


def _sample_attend(page_table, q_bf, bias2d, k_bf, v_bf, cache_k2, cache_v2, depth, layer, rows_p, t_len):
    db, n_pages = page_table.shape
    nr = SEQ_PER_STEP * t_len
    base = rows_p // nr
    lp = (n_pages + 1) * PAGE
    kvd = KV_HEADS * HEAD_DIM
    in_specs = [
        pl.BlockSpec((nr, ATTN_HEADS * HEAD_DIM), lambda n, pt: (base + n, 0)),
        pl.BlockSpec((nr, lp), lambda n, pt: (n, 0)),
        pl.BlockSpec((nr, kvd), lambda n, pt: (base + n, 0)),
        pl.BlockSpec((nr, kvd), lambda n, pt: (base + n, 0)),
    ]
    for _ in range(2):
        for j in range(SEQ_PER_STEP):
            for p in range(n_pages):
                in_specs.append(pl.BlockSpec(
                    (PAGE * KV_HEADS, HEAD_DIM),
                    lambda n, pt, j=j, p=p: (pt[n * SEQ_PER_STEP + j, p] * depth + layer, 0)))
    kern = functools.partial(_sample_attend_kernel, n_pages=n_pages, t_len=t_len)
    npg = SEQ_PER_STEP * n_pages
    return pl.pallas_call(
        kern,
        out_shape=jax.ShapeDtypeStruct((db * t_len, ATTN_HEADS * HEAD_DIM), BF16),
        grid_spec=pltpu.PrefetchScalarGridSpec(
            num_scalar_prefetch=1,
            grid=(db // SEQ_PER_STEP,),
            in_specs=in_specs,
            out_specs=pl.BlockSpec((nr, ATTN_HEADS * HEAD_DIM), lambda n, pt: (n, 0)),
            scratch_shapes=[pltpu.VMEM((lp, kvd), BF16), pltpu.VMEM((lp, kvd), BF16)],
        ),
        compiler_params=_cparams(("arbitrary",)),
        name="dsa_sample_attend",
    )(page_table, q_bf, bias2d, k_bf, v_bf, *([cache_k2] * npg), *([cache_v2] * npg))


def _chunk_masks(n, chunk):
    ri = lax.broadcasted_iota(I32, (n, n), 0)
    ci = lax.broadcasted_iota(I32, (n, n), 1)

    def same(size):
        sh = size.bit_length() - 1
        return (ri >> sh) == (ci >> sh)

    same_c = same(chunk)
    incl = same_c & (ci <= ri)
    strict = same_c & (ci < ri)
    base = min(SUBLANES, chunk)
    levels = []
    s = base
    while s < chunk:
        levels.append(same(2 * s) & jnp.logical_not(same(s)))
        s *= 2
    return incl, strict, same(base), levels, (ri == ci)


def _unit_lower_inverse(a_heads, same_base, levels, eye):
    ident = jnp.where(eye, 1.0, 0.0)
    ad = [jnp.where(same_base, a, 0.0) for a in a_heads]
    a2 = [_mm3(x, x) for x in ad]
    a4 = [_mm3(x, x) for x in a2]
    t = [_mm3(ident - x, ident + y) for x, y in zip(ad, a2)]
    t = [_mm3(x, ident + y) for x, y in zip(t, a4)]
    for lv in levels:
        to = [_mm3(x, jnp.where(lv, a, 0.0)) for x, a in zip(t, a_heads)]
        t = [x - _mm3(y, x) for x, y in zip(t, to)]
    return t


def _mm1(a, b, dot=_dot):
    return dot(a.astype(BF16), b.astype(BF16))


def _gdn_intra(q, k, v, beta_c, gc_c, gc_r, masks):
    incl, strict, same_base, levels, eye = masks
    heads = range(len(q))
    decay = [jnp.exp(jnp.where(incl, gc_c[h] - gc_r[h], -jnp.inf)) for h in heads]
    kb = [k[h] * beta_c[h] for h in heads]
    a = [jnp.where(strict, _mm1(kb[h], k[h], _dot_nt) * decay[h], 0.0) for h in heads]
    t = _unit_lower_inverse(a, same_base, levels, eye)
    uw = [_mm3(t[h], jnp.concatenate([v[h] * beta_c[h], kb[h] * jnp.exp(gc_c[h])], axis=1)) for h in heads]
    qk = [_mm1(q[h], k[h], _dot_nt) * decay[h] for h in heads]
    return [x[:, :HEAD_DIM] for x in uw], [x[:, HEAD_DIM:] for x in uw], qk


def _l2(x):
    return x * lax.rsqrt(jnp.sum(x * x, axis=-1, keepdims=True) + EPS)


def _cum_matrices(n, chunk):
    ri = lax.broadcasted_iota(I32, (n, n), 0)
    ci = lax.broadcasted_iota(I32, (n, n), 1)
    sh = chunk.bit_length() - 1
    same = (ri >> sh) == (ci >> sh)
    lower = jnp.where(same & (ci <= ri), 1.0, 0.0).astype(BF16)
    upper = jnp.where(same & (ri <= ci), 1.0, 0.0).astype(BF16)
    return lower, upper


def _gdn_gates(sm, smt, alog_l, dtb_l, alog_c, dtb_c, chunk):
    n = sm.shape[0]
    lower, upper = _cum_matrices(n, chunk)
    g_tile = -jnp.exp(alog_l) * _softplus(sm + dtb_l)
    beta_tile = jax.nn.sigmoid(sm)
    gc_cols = _mm_exact_lhs(lower, g_tile)
    g_rows = -jnp.exp(alog_c) * _softplus(smt + dtb_c)
    gc_rows = _mm_exact_rhs(g_rows, upper)
    return beta_tile, gc_cols, gc_rows


def _gdn_prompt_kernel(gq_ref, gz_ref, sm_ref, smt_ref, cw_ref, alog_l, dtb_l, alog_c, dtb_c, ng_ref,
                       o_ref, s_o_ref, stage, s_scr, *, tt, chunk):
    t_idx = pl.program_id(1)
    hd = HEAD_DIM
    nh = GDN_HEADS

    @pl.when(t_idx == 0)
    def _():
        stage[0:SUBLANES, :] = jnp.zeros((SUBLANES, stage.shape[1]), F32)
        s_scr[...] = jnp.zeros(s_scr.shape, F32)

    x = gq_ref[...]
    stage[SUBLANES:SUBLANES + tt, :] = x
    y = None
    for j in range(CONV_W):
        term = stage[pl.ds(SUBLANES - (CONV_W - 1) + j, tt), :] * cw_ref[j:j + 1, :]
        y = term if y is None else y + term
    stage[0:SUBLANES, :] = x[tt - SUBLANES:tt, :]
    y = _silu(y)

    beta_tile, gc_cols, gc_rows = _gdn_gates(sm_ref[...], smt_ref[...], alog_l[...], dtb_l[...],
                                             alog_c[...], dtb_c[...], chunk)
    masks = _chunk_masks(tt, chunk)
    ng = ng_ref[...]
    heads = range(nh)
    q = [_l2(y[:, h * hd:(h + 1) * hd]) * (hd ** -0.5) for h in heads]
    k = [_l2(y[:, (nh + h) * hd:(nh + h + 1) * hd]) for h in heads]
    v = [y[:, (2 * nh + h) * hd:(2 * nh + h + 1) * hd] for h in heads]
    beta_c = [beta_tile[:, GB_LANE + h:GB_LANE + h + 1] for h in heads]
    gc_c = [gc_cols[:, GA_LANE + h:GA_LANE + h + 1] for h in heads]
    gc_r = [gc_rows[h:h + 1, :] for h in heads]
    u, w, qk = _gdn_intra(q, k, v, beta_c, gc_c, gc_r, masks)
    qg = [q[h] * jnp.exp(gc_c[h]) for h in heads]
    s = [s_scr[h] for h in heads]
    vnew = [[] for _ in heads]
    ointer = [[] for _ in heads]
    for c in range(tt // chunk):
        r = slice(c * chunk, (c + 1) * chunk)
        for h in heads:
            vn = u[h][r] - _mm1(w[h][r], s[h])
            ointer[h].append(_mm1(qg[h][r], s[h]))
            g_last = gc_c[h][(c + 1) * chunk - 1:(c + 1) * chunk, :]
            kdec = k[h][r] * jnp.exp(g_last - gc_c[h][r])
            s[h] = s[h] * jnp.exp(g_last) + _mm1(kdec, vn, _dot_tn)
            vnew[h].append(vn)
    for h in heads:
        s_scr[h] = s[h]
        o = jnp.concatenate(ointer[h], axis=0) + _mm1(qk[h], jnp.concatenate(vnew[h], axis=0))
        o = _rms(o) * ng * _silu(gz_ref[:, h * hd:(h + 1) * hd])
        o_ref[:, h * hd:(h + 1) * hd] = o.astype(BF16)

    @pl.when(t_idx == pl.num_programs(1) - 1)
    def _():
        s_o_ref[...] = s_scr[...]


def _gdn_prompt(gq, gz, small, small_t, conv_w_l, alog_l, dtb_l, alog_c, dtb_c, ng, batch, seq):
    tt = min(256, seq)
    chunk = min(GDN_CHUNK, seq)
    nt = seq // tt
    gdim = gq.shape[1]
    zdim = gz.shape[1]
    row = lambda b, t: (b * nt + t, 0)
    const = lambda b, t: (0, 0)
    kern = functools.partial(_gdn_prompt_kernel, tt=tt, chunk=chunk)
    return pl.pallas_call(
        kern,
        out_shape=[jax.ShapeDtypeStruct((batch * seq, zdim), BF16),
                   jax.ShapeDtypeStruct((batch, GDN_HEADS, HEAD_DIM, HEAD_DIM), F32)],
        grid=(batch, nt),
        in_specs=[
            pl.BlockSpec((tt, gdim), row),
            pl.BlockSpec((tt, zdim), row),
            pl.BlockSpec((tt, LANES), row),
            pl.BlockSpec((SUBLANES, tt), lambda b, t: (0, b * nt + t)),
            pl.BlockSpec((CONV_W, gdim), const),
            pl.BlockSpec((1, LANES), const),
            pl.BlockSpec((1, LANES), const),
            pl.BlockSpec((SUBLANES, 1), const),
            pl.BlockSpec((SUBLANES, 1), const),
            pl.BlockSpec((1, HEAD_DIM), const),
        ],
        out_specs=[pl.BlockSpec((tt, zdim), row),
                   pl.BlockSpec((None, GDN_HEADS, HEAD_DIM, HEAD_DIM), lambda b, t: (b, 0, 0, 0))],
        scratch_shapes=[pltpu.VMEM((tt + SUBLANES, gdim), F32),
                        pltpu.VMEM((GDN_HEADS, HEAD_DIM, HEAD_DIM), F32)],
        compiler_params=_cparams(("arbitrary", "arbitrary")),
        name="gdn_prompt",
    )(gq, gz, small, small_t, conv_w_l, alog_l, dtb_l, alog_c, dtb_c, ng)


def _gdn_sample_kernel(gq_ref, gz_ref, sm_ref, smt_ref, cst_ref, s0_ref, cw_ref, alog_l, dtb_l, alog_c,
                       dtb_c, ng_ref, o_ref, s_o_ref, stage, uw_scr, vn_scr, oi_scr, *, nb, t_len):
    hd = HEAD_DIM
    nh = GDN_HEADS
    n = nb * t_len
    gdim = gq_ref.shape[1]
    stage[:, 0:SUBLANES, :] = cst_ref[...].reshape(nb, SUBLANES, gdim)
    stage[:, SUBLANES:SUBLANES + t_len, :] = gq_ref[...].reshape(nb, t_len, gdim)
    y = None
    for j in range(CONV_W):
        term = stage[:, pl.ds(SUBLANES - (CONV_W - 1) + j, t_len), :] * cw_ref[j:j + 1, :]
        y = term if y is None else y + term
    y = _silu(y).reshape(n, gdim)

    beta_tile, gc_cols, gc_rows = _gdn_gates(sm_ref[...], smt_ref[...], alog_l[...], dtb_l[...],
                                             alog_c[...], dtb_c[...], t_len)
    masks = _chunk_masks(n, t_len)
    ng = ng_ref[...]
    ri = lax.broadcasted_iota(I32, (n, n), 0)
    ci = lax.broadcasted_iota(I32, (n, n), 1)
    sh = t_len.bit_length() - 1
    pick_last = jnp.where(((ri >> sh) == (ci >> sh)) & ((ci & (t_len - 1)) == t_len - 1), 1.0, 0.0).astype(BF16)
    g_last_cols = _mm_exact_lhs(pick_last, gc_cols)
    heads = range(nh)
    q = [_l2(y[:, h * hd:(h + 1) * hd]) * (hd ** -0.5) for h in heads]
    k = [_l2(y[:, (nh + h) * hd:(nh + h + 1) * hd]) for h in heads]
    v = [y[:, (2 * nh + h) * hd:(2 * nh + h + 1) * hd] for h in heads]
    beta_c = [beta_tile[:, GB_LANE + h:GB_LANE + h + 1] for h in heads]
    gc_c = [gc_cols[:, GA_LANE + h:GA_LANE + h + 1] for h in heads]
    gc_r = [gc_rows[h:h + 1, :] for h in heads]
    u, w, qk = _gdn_intra(q, k, v, beta_c, gc_c, gc_r, masks)
    for h in heads:
        g_last_c = g_last_cols[:, GA_LANE + h:GA_LANE + h + 1]
        uw_scr[h, 0] = u[h]
        uw_scr[h, 1] = w[h]
        uw_scr[h, 2] = q[h] * jnp.exp(gc_c[h])
        uw_scr[h, 3] = k[h] * jnp.exp(g_last_c - gc_c[h])
        uw_scr[h, 4] = jnp.broadcast_to(jnp.exp(g_last_c), (n, hd))

    def seq_step(i, carry):
        r0 = pl.multiple_of(i * t_len, t_len)
        rows = pl.ds(r0, t_len)
        for h in heads:
            s = s0_ref[i, h]
            vn = uw_scr[h, 0, rows, :] - _mm1(uw_scr[h, 1, rows, :], s)
            oi_scr[h, rows, :] = _mm1(uw_scr[h, 2, rows, :], s)
            vn_scr[h, rows, :] = vn
            dec = uw_scr[h, 4, pl.ds(r0, 1), :]
            s_o_ref[i, h] = s * dec + _mm1(uw_scr[h, 3, rows, :], vn, _dot_tn)
        return carry

    lax.fori_loop(0, nb, seq_step, 0)
    for h in heads:
        o = oi_scr[h] + _mm1(qk[h], vn_scr[h])
        o = _rms(o) * ng * _silu(gz_ref[:, h * hd:(h + 1) * hd])
        o_ref[:, h * hd:(h + 1) * hd] = o.astype(BF16)


def _gdn_sample(gq, gz, small, small_t, cstate, state_gdn, layer, conv_w_l, alog_l, dtb_l, alog_c, dtb_c, ng,
                rows_p, db, t_len):
    nb = min(16, db)
    n = nb * t_len
    base = rows_p // n
    gdim = gq.shape[1]
    zdim = gz.shape[1]
    row = lambda i: (base + i, 0)
    const = lambda i: (0, 0)
    kern = functools.partial(_gdn_sample_kernel, nb=nb, t_len=t_len)
    return pl.pallas_call(
        kern,
        out_shape=[jax.ShapeDtypeStruct((db * t_len, zdim), BF16),
                   jax.ShapeDtypeStruct((db, GDN_HEADS, HEAD_DIM, HEAD_DIM), F32)],
        grid=(db // nb,),
        in_specs=[
            pl.BlockSpec((n, gdim), row),
            pl.BlockSpec((n, zdim), row),
            pl.BlockSpec((n, LANES), row),
            pl.BlockSpec((SUBLANES, n), lambda i: (0, base + i)),
            pl.BlockSpec((n, gdim), lambda i: (i, 0)),
            pl.BlockSpec((nb, None, GDN_HEADS, HEAD_DIM, HEAD_DIM), lambda i: (i, layer, 0, 0, 0)),
            pl.BlockSpec((CONV_W, gdim), const),
            pl.BlockSpec((1, LANES), const),
            pl.BlockSpec((1, LANES), const),
            pl.BlockSpec((SUBLANES, 1), const),
            pl.BlockSpec((SUBLANES, 1), const),
            pl.BlockSpec((1, HEAD_DIM), const),
        ],
        out_specs=[pl.BlockSpec((n, zdim), lambda i: (i, 0)),
                   pl.BlockSpec((nb, GDN_HEADS, HEAD_DIM, HEAD_DIM), lambda i: (i, 0, 0, 0))],
        scratch_shapes=[pltpu.VMEM((nb, 2 * SUBLANES, gdim), F32),
                        pltpu.VMEM((GDN_HEADS, 5, n, HEAD_DIM), F32),
                        pltpu.VMEM((GDN_HEADS, n, HEAD_DIM), F32),
                        pltpu.VMEM((GDN_HEADS, n, HEAD_DIM), F32)],
        compiler_params=_cparams(("arbitrary",)),
        name="gdn_sample",
    )(gq, gz, small, small_t, cstate, state_gdn, conv_w_l, alog_l, dtb_l, alog_c, dtb_c, ng)


def _outproj_kernel(x_ref, att_ref, gdn_ref, gate_ref, sh_ref, sc_ref, g_ref, wo_ref, *rest, moe):
    if moe:
        rw_ref, rb_ref, x_o, h_o, lg_o = rest
    else:
        x_o, h_o = rest
    x = x_ref[...]
    adim = att_ref.shape[1]
    y = _dot(att_ref[...], wo_ref[0:adim, :]) + _dot(gdn_ref[...], wo_ref[adim:, :])
    xn = x + gate_ref[...] * y.reshape(x.shape)
    x_o[...] = xn
    h = (_rms(xn) * g_ref[...]) * (1.0 + sc_ref[...]) + sh_ref[...]
    h2 = h.reshape(y.shape)
    h_o[...] = h2.astype(BF16)
    if moe:
        lg_o[...] = _mm3(h2, rw_ref[...]) + rb_ref[...]


def _outproj(x3, att, gdn, gate, sh, sc, g, wo, router, tile_of, tm):
    ng, _, d = x3.shape
    rows = ng * SUBLANES
    gt = tm // SUBLANES
    moe = router is not None
    const = lambda t: (0, 0)
    row = lambda t: (t, 0)
    modspec = pl.BlockSpec((gt, 1, d), lambda t: (tile_of(t), 0, 0))
    in_specs = [
        pl.BlockSpec((gt, SUBLANES, d), lambda t: (t, 0, 0)),
        pl.BlockSpec((tm, att.shape[1]), row),
        pl.BlockSpec((tm, gdn.shape[1]), row),
        modspec, modspec, modspec,
        pl.BlockSpec((1, d), const),
        pl.BlockSpec(wo.shape, const),
    ]
    out_shape = [jax.ShapeDtypeStruct(x3.shape, F32), jax.ShapeDtypeStruct((rows, d), BF16)]
    out_specs = [pl.BlockSpec((gt, SUBLANES, d), lambda t: (t, 0, 0)), pl.BlockSpec((tm, d), row)]
    args = [x3, att, gdn, gate, sh, sc, g, wo]
    if moe:
        in_specs += [pl.BlockSpec((d, LANES), const), pl.BlockSpec((1, LANES), const)]
        out_shape.append(jax.ShapeDtypeStruct((rows, LANES), F32))
        out_specs.append(pl.BlockSpec((tm, LANES), row))
        args += list(router)
    return pl.pallas_call(
        functools.partial(_outproj_kernel, moe=moe),
        out_shape=out_shape,
        grid=(rows // tm,),
        in_specs=in_specs,
        out_specs=out_specs,
        compiler_params=_cparams(("arbitrary",)),
        name="out_proj",
    )(*args)


def _ffn_kernel(x_ref, h_ref, gate_ref, wg_ref, wu_ref, wd_ref, o_ref, *, fc):
    h = h_ref[...]
    dff = wg_ref.shape[1]
    acc = None
    for c in range(dff // fc):
        a = _dot(h, wg_ref[:, c * fc:(c + 1) * fc])
        u = _dot(h, wu_ref[:, c * fc:(c + 1) * fc])
        t = _dot((_silu(a) * u).astype(BF16), wd_ref[c * fc:(c + 1) * fc, :])
        acc = t if acc is None else acc + t
    x = x_ref[...]
    o_ref[...] = x + gate_ref[...] * acc.reshape(x.shape)


def _ffn(x3, hff, gate, wg, wu, wd, tile_of, tm):
    ng, _, d = x3.shape
    rows = ng * SUBLANES
    gt = tm // SUBLANES
    dff = wg.shape[1]
    fc = dff
    for cand in (1408, 1024, 768, 512, 256, 128):
        if dff % cand == 0:
            fc = cand
            break
    const = lambda t: (0, 0)
    return pl.pallas_call(
        functools.partial(_ffn_kernel, fc=fc),
        out_shape=jax.ShapeDtypeStruct(x3.shape, F32),
        grid=(rows // tm,),
        in_specs=[
            pl.BlockSpec((gt, SUBLANES, d), lambda t: (t, 0, 0)),
            pl.BlockSpec((tm, d), lambda t: (t, 0)),
            pl.BlockSpec((gt, 1, d), lambda t: (tile_of(t), 0, 0)),
            pl.BlockSpec(wg.shape, const),
            pl.BlockSpec(wu.shape, const),
            pl.BlockSpec(wd.shape, const),
        ],
        out_specs=pl.BlockSpec((gt, SUBLANES, d), lambda t: (t, 0, 0)),
        compiler_params=_cparams(("arbitrary",)),
        name="ffn_dense",
    )(x3, hff, gate, wg, wu, wd)


def _moe_kernel(x_ref, h_ref, lg_ref, gate_ref, wg_ref, wu_ref, wd_ref, o_ref, acc_scr, g_scr):
    e = pl.program_id(1)
    tm = h_ref.shape[0]

    @pl.when(e == 0)
    def _():
        lane = lax.broadcasted_iota(I32, (tm, LANES), 1)
        lg = jnp.where(lane < N_EXPERTS, lg_ref[...], -jnp.inf)
        m1 = jnp.max(lg, axis=1, keepdims=True)
        i1 = jnp.min(jnp.where(lg == m1, lane, LANES), axis=1, keepdims=True)
        rest = jnp.where(lane == i1, -jnp.inf, lg)
        m2 = jnp.max(rest, axis=1, keepdims=True)
        i2 = jnp.min(jnp.where(rest == m2, lane, LANES), axis=1, keepdims=True)
        e2 = jnp.exp(m2 - m1)
        den = 1.0 + e2
        g_scr[...] = jnp.where(lane == i1, 1.0 / den, 0.0) + jnp.where(lane == i2, e2 / den, 0.0)
        acc_scr[...] = jnp.zeros(acc_scr.shape, F32)

    h = h_ref[...]
    a = _dot(h, wg_ref[...])
    u = _dot(h, wu_ref[...])
    y = _dot((_silu(a) * u).astype(BF16), wd_ref[...])
    lane = lax.broadcasted_iota(I32, (tm, LANES), 1)
    ge = jnp.sum(jnp.where(lane == e, g_scr[...], 0.0), axis=1, keepdims=True)
    acc_scr[...] += ge * y

    @pl.when(e == pl.num_programs(1) - 1)
    def _():
        x = x_ref[...]
        o_ref[...] = x + gate_ref[...] * acc_scr[...].reshape(x.shape)


def _moe(x3, hff, logits, gate, wg, wu, wd, tile_of, tm):
    ng, _, d = x3.shape
    rows = ng * SUBLANES
    gt = tm // SUBLANES
    ne, _, eff = wg.shape
    return pl.pallas_call(
        _moe_kernel,
        out_shape=jax.ShapeDtypeStruct(x3.shape, F32),
        grid=(rows // tm, ne),
        in_specs=[
            pl.BlockSpec((gt, SUBLANES, d), lambda t, e: (t, 0, 0)),
            pl.BlockSpec((tm, d), lambda t, e: (t, 0)),
            pl.BlockSpec((tm, LANES), lambda t, e: (t, 0)),
            pl.BlockSpec((gt, 1, d), lambda t, e: (tile_of(t), 0, 0)),
            pl.BlockSpec((None, d, eff), lambda t, e: (e, 0, 0)),
            pl.BlockSpec((None, d, eff), lambda t, e: (e, 0, 0)),
            pl.BlockSpec((None, eff, d), lambda t, e: (e, 0, 0)),
        ],
        out_specs=pl.BlockSpec((gt, SUBLANES, d), lambda t, e: (t, 0, 0)),
        scratch_shapes=[pltpu.VMEM((tm, d), F32), pltpu.VMEM((tm, LANES), F32)],
        compiler_params=_cparams(("arbitrary", "arbitrary")),
        name="ffn_moe",
    )(x3, hff, logits, gate, wg, wu, wd)


def _rope_tables(pos):
    pos = pos.astype(F32)[:, None]
    half = HEAD_DIM // 2
    ang = pos * (ROPE_THETA ** (-jnp.arange(half, dtype=F32) / half))[None, :]
    c, s = jnp.cos(ang), jnp.sin(ang)
    cq = jnp.concatenate([c, c], axis=1)
    sq = jnp.concatenate([-s, s], axis=1)
    half = IDX_DIM // 2
    ang = pos * (ROPE_THETA ** (-jnp.arange(half, dtype=F32) / half))[None, :]
    c, s = jnp.cos(ang), jnp.sin(ang)
    z = jnp.zeros_like(s)
    ci = jnp.concatenate([c, c, c, c], axis=1)
    sa = jnp.concatenate([-s, z, -s, z], axis=1)
    sb = jnp.concatenate([z, s, z, s], axis=1)
    return cq, sq, ci, sa, sb


def _lane_vec(vals, lane0):
    return jnp.zeros((1, LANES), F32).at[0, lane0:lane0 + vals.shape[0]].set(vals)


def kernel(x_prompt, x_sample, cache_k, cache_v, cache_kidx, state_gdn, state_conv, page_table, c_prompt, c_sample, mod_w, mod_b, norm_mix_g, norm_ffn_g, w_in, q_norm_g, k_norm_g, conv_w, a_log, dt_bias, gdn_norm_g, w_out, ffn_w_gate, ffn_w_up, ffn_w_down, router_w, router_b, moe_w_gate, moe_w_up, moe_w_down):
    batch, seq, d = x_prompt.shape
    db, t_len, _ = x_sample.shape
    n_layers = mod_w.shape[0]
    n_pages = page_table.shape[1]
    past = n_pages * PAGE
    rows_p, rows_s = batch * seq, db * t_len
    rows = rows_p + rows_s
    assert t_len == SUBLANES and seq % LANES == 0
    tm = min(256, rows_s, seq)
    assert seq % tm == 0 and rows_s % tm == 0
    gt = tm // SUBLANES
    tiles_per_batch = seq // tm
    n_ptiles = rows_p // tm

    def tile_of(t):
        return jnp.where(t < n_ptiles, t // tiles_per_batch, batch + (t - n_ptiles))

    def tab_of(t):
        return jnp.where(t < n_ptiles, t % tiles_per_batch, tiles_per_batch)

    def groups(m):
        mp = jnp.repeat(m[:batch], gt, axis=0)
        return jnp.concatenate([mp, m[batch:batch + db]], axis=0)[:, None, :]

    x3 = jnp.concatenate([x_prompt.reshape(rows_p, d), x_sample.reshape(rows_s, d)], axis=0)
    x3 = x3.reshape(rows // SUBLANES, SUBLANES, d)

    n_c = batch + db
    c_all = jnp.concatenate([c_prompt, c_sample, jnp.zeros((-n_c % SUBLANES, d), F32)], axis=0)
    mods = _mods(c_all, mod_w, mod_b)

    pos = jnp.concatenate([jnp.arange(seq), past + (jnp.arange(tm) % t_len)])
    tabs = _rope_tables(pos)

    kvd = KV_HEADS * HEAD_DIM
    depth = cache_k.shape[1]
    cache_k2 = cache_k.reshape(-1, HEAD_DIM)
    cache_v2 = cache_v.reshape(-1, HEAD_DIM)
    cache_kidx_t = jnp.swapaxes(cache_kidx, 2, 3)
    n_sel_s = min(TOPK_MAX, (past + t_len) // 4)
    lp = (n_pages + 1) * PAGE

    outs = {name: [] for name in ("kp", "vp", "kip", "ks", "vs", "kis", "sp", "ss", "cp", "cs")}
    col = 0
    offs = []
    for size in (ATTN_HEADS * HEAD_DIM, kvd, kvd, IDX_HEADS * IDX_DIM, IDX_DIM, IDX_HEADS,
                 GDN_HEADS * 3 * HEAD_DIM, GDN_HEADS * HEAD_DIM, GDN_HEADS, GDN_HEADS):
        offs.append((col, col + size))
        col += size
    o_q, o_k, o_v, o_qi, o_ki, o_wi, o_gq, o_gz, o_ga, o_gb = offs

    for l in range(n_layers):
        m6 = [groups(mods[l, :, j * d:(j + 1) * d]) for j in range(6)]
        w = w_in[l]
        wa = w[:, o_q[0]:o_v[1]].astype(BF16)
        qi_w = w[:, o_qi[0]:o_qi[1]].reshape(d, IDX_HEADS, 1, IDX_DIM)
        qi_w = jnp.broadcast_to(qi_w, (d, IDX_HEADS, 2, IDX_DIM)).reshape(d, IDX_HEADS * LANES)
        ki_w = w[:, o_ki[0]:o_ki[1]]
        misc_w = jnp.concatenate([w[:, o_wi[0]:o_wi[1]], w[:, o_ga[0]:o_ga[1]], w[:, o_gb[0]:o_gb[1]],
                                  jnp.zeros((d, LANES - IDX_HEADS - 2 * GDN_HEADS), F32)], axis=1)
        wi_f = jnp.concatenate([qi_w, ki_w, ki_w, misc_w], axis=1)
        wih = wi_f.astype(BF16)
        wil = (wi_f - wih.astype(F32)).astype(BF16)
        wg = w[:, o_gq[0]:o_gz[1]].astype(BF16)

        (q_bf, k_f, v_f, k_bf, v_bf, qi3, ki_f, ki3, small, gq, gz) = _inproj(
            x3, m6[0], m6[1], norm_mix_g[l][None, :], wa, wih, wil, wg,
            q_norm_g[l][None, :], k_norm_g[l][None, :], tabs, tile_of, tab_of, tm)

        small_t = jnp.transpose(small[:, GA_LANE:GA_LANE + 2 * GDN_HEADS])
        alog_l = _lane_vec(a_log[l], GA_LANE)
        dtb_l = _lane_vec(dt_bias[l], GA_LANE)
        pad4 = jnp.zeros((SUBLANES - GDN_HEADS,), F32)
        alog_c = jnp.concatenate([a_log[l], pad4])[:, None]
        dtb_c = jnp.concatenate([dt_bias[l], pad4])[:, None]
        ng = gdn_norm_g[l][None, :]

        att_p = _dsa_prompt(qi3, small, q_bf, ki3, k_bf, v_bf, batch, seq)
        gdn_p, s_p = _gdn_prompt(gq, gz, small, small_t, conv_w[l], alog_l, dtb_l, alog_c, dtb_c, ng,
                                 batch, seq)

        scores = _sample_scores(page_table, qi3, small, ki3, cache_kidx_t, l, rows_p, t_len)
        bias = _sample_select(scores.reshape(rows_s, lp), n_sel_s)
        att_s = _sample_attend(page_table, q_bf, bias, k_bf, v_bf, cache_k2, cache_v2, depth, l, rows_p, t_len)
        cstate = jnp.pad(state_conv[:, l], ((0, 0), (SUBLANES - (CONV_W - 1), 0), (0, 0)))
        cstate = cstate.reshape(db * SUBLANES, cstate.shape[2])
        gdn_s, s_s = _gdn_sample(gq, gz, small, small_t, cstate, state_gdn, l, conv_w[l], alog_l, dtb_l,
                                 alog_c, dtb_c, ng, rows_p, db, t_len)

        att = jnp.concatenate([att_p, att_s], axis=0)
        gdn = jnp.concatenate([gdn_p, gdn_s], axis=0)
        i = l // 2
        router = None
        if l % 2 == 1:
            rw = jnp.concatenate([router_w[i], jnp.zeros((d, LANES - N_EXPERTS), F32)], axis=1)
            rb = jnp.concatenate([router_b[i], jnp.zeros((LANES - N_EXPERTS,), F32)])[None, :]
            router = (rw, rb)
        res = _outproj(x3, att, gdn, m6[2], m6[3], m6[4], norm_ffn_g[l][None, :], w_out[l].astype(BF16),
                       router, tile_of, tm)
        if l % 2 == 0:
            x3, hff = res
            x3 = _ffn(x3, hff, m6[5], ffn_w_gate[i].astype(BF16), ffn_w_up[i].astype(BF16),
                      ffn_w_down[i].astype(BF16), tile_of, tm)
        else:
            x3, hff, logits = res
            x3 = _moe(x3, hff, logits, m6[5], moe_w_gate[i].astype(BF16), moe_w_up[i].astype(BF16),
                      moe_w_down[i].astype(BF16), tile_of, tm)

        outs["kp"].append(k_f[:rows_p].reshape(batch, seq, KV_HEADS, HEAD_DIM))
        outs["vp"].append(v_f[:rows_p].reshape(batch, seq, KV_HEADS, HEAD_DIM))
        outs["kip"].append(ki_f[:rows_p].reshape(batch, seq, IDX_DIM))
        outs["ks"].append(k_f[rows_p:].reshape(db, t_len, KV_HEADS, HEAD_DIM))
        outs["vs"].append(v_f[rows_p:].reshape(db, t_len, KV_HEADS, HEAD_DIM))
        outs["kis"].append(ki_f[rows_p:].reshape(db, t_len, IDX_DIM))
        outs["sp"].append(s_p)
        outs["ss"].append(s_s)
        gq_p = gq[:rows_p].reshape(batch, seq, -1)
        gq_s = gq[rows_p:].reshape(db, t_len, -1)
        outs["cp"].append(gq_p[:, seq - (CONV_W - 1):])
        outs["cs"].append(gq_s[:, t_len - (CONV_W - 1):])

    x2 = x3.reshape(rows, d)
    st = lambda name: jnp.stack(outs[name], axis=1)
    return (x2[:rows_p].reshape(batch, seq, d), x2[rows_p:].reshape(db, t_len, d),
            st("kp"), st("vp"), st("kip"), st("ks"), st("vs"), st("kis"),
            st("sp"), st("ss"), st("cp"), st("cs"))
```

```python
import functools
import math

import jax
import jax.numpy as jnp
from jax import lax
from jax.experimental import pallas as pl
from jax.experimental.pallas import tpu as pltpu

F32 = jnp.float32
BF16 = jnp.bfloat16
I32 = jnp.int32

HEAD_DIM = 128
ATTN_HEADS = 4
KV_HEADS = 2
GROUP = ATTN_HEADS // KV_HEADS
IDX_HEADS = 8
IDX_DIM = 64
TOPK_MAX = 256
GDN_HEADS = 4
GDN_CHUNK = 64
CONV_W = 4
N_EXPERTS = 8
PAGE = 128
ROPE_THETA = 10000.0
EPS = 1e-6

LANES = 128
SUBLANES = 8
VMEM_LIMIT = 56 * 1024 * 1024
NEG = -1e30
INT_MIN = -2147483648
WI_LANE = 0
GA_LANE = 8
GB_LANE = 12


def _cparams(sem):
    return pltpu.CompilerParams(dimension_semantics=sem, vmem_limit_bytes=VMEM_LIMIT)


def _dot(a, b):
    return jnp.dot(a, b, preferred_element_type=F32)


def _dot_nt(a, b):
    return lax.dot_general(a, b, (((1,), (1,)), ((), ())), preferred_element_type=F32)


def _dot_tn(a, b):
    return lax.dot_general(a, b, (((0,), (0,)), ((), ())), preferred_element_type=F32)


def _split(x):
    hi = x.astype(BF16)
    lo = (x - hi.astype(F32)).astype(BF16)
    return hi, lo


def _mm3(a, b, dot=_dot):
    ah, al = _split(a)
    bh, bl = _split(b)
    return dot(ah, bh) + (dot(ah, bl) + dot(al, bh))


def _split_three(a):
    a1 = a.astype(BF16)
    r1 = a - a1.astype(F32)
    a2 = r1.astype(BF16)
    a3 = (r1 - a2.astype(F32)).astype(BF16)
    return a1, a2, a3


def _mm_exact_rhs(a, b_bf16):
    a1, a2, a3 = _split_three(a)
    return _dot(a1, b_bf16) + (_dot(a2, b_bf16) + _dot(a3, b_bf16))


def _mm_exact_lhs(m_bf16, a):
    a1, a2, a3 = _split_three(a)
    return _dot(m_bf16, a1) + (_dot(m_bf16, a2) + _dot(m_bf16, a3))


def _silu(x):
    return x * jax.nn.sigmoid(x)


def _softplus(x):
    return jnp.maximum(x, 0.0) + jnp.log1p(jnp.exp(-jnp.abs(x)))


def _mods_kernel(c_ref, w_ref, b_ref, o_ref):
    o_ref[...] = _mm3(_silu(c_ref[...]), w_ref[...]) + b_ref[...]


def _mods(c_all, mod_w, mod_b):
    n_layers, d, n6 = mod_w.shape
    rows = c_all.shape[0]
    tn = n6 // 4
    return pl.pallas_call(
        _mods_kernel,
        out_shape=jax.ShapeDtypeStruct((n_layers, rows, n6), F32),
        grid=(n_layers, n6 // tn),
        in_specs=[
            pl.BlockSpec((rows, d), lambda l, j: (0, 0)),
            pl.BlockSpec((None, d, tn), lambda l, j: (l, 0, j)),
            pl.BlockSpec((None, 1, tn), lambda l, j: (l, 0, j)),
        ],
        out_specs=pl.BlockSpec((None, rows, tn), lambda l, j: (l, 0, j)),
        compiler_params=_cparams(("arbitrary", "arbitrary")),
        name="ada_mods",
    )(c_all, mod_w, mod_b.reshape(n_layers, 1, n6))


def _rms(x):
    return x * lax.rsqrt(jnp.mean(x * x, axis=-1, keepdims=True) + EPS)


def _inproj_kernel(x_ref, sh_ref, sc_ref, g_ref, wa_ref, wih_ref, wil_ref, wg_ref, qg_ref, kg_ref,
                   cq_ref, sq_ref, ci_ref, sa_ref, sb_ref,
                   q_o, k_o, v_o, kb_o, vb_o, qi3_o, ki_o, ki3_o, small_o, gq_o, gz_o):
    x = x_ref[...]
    h = (_rms(x) * g_ref[...]) * (1.0 + sc_ref[...]) + sh_ref[...]
    tm = x.shape[0] * x.shape[1]
    h2 = h.reshape(tm, x.shape[2])
    hb, hl = _split(h2)
    za = _dot(hb, wa_ref[...])
    wih = wih_ref[...]
    zi = _dot(hb, wih) + (_dot(hb, wil_ref[...]) + _dot(hl, wih))
    zg = _dot(hb, wg_ref[...])

    cq, sq = cq_ref[...], sq_ref[...]
    qg, kg = qg_ref[...], kg_ref[...]
    for hh in range(ATTN_HEADS):
        qn = _rms(za[:, hh * HEAD_DIM:(hh + 1) * HEAD_DIM]) * qg
        qr = qn * cq + pltpu.roll(qn, HEAD_DIM // 2, 1) * sq
        q_o[:, hh * HEAD_DIM:(hh + 1) * HEAD_DIM] = (qr * (HEAD_DIM ** -0.5)).astype(BF16)
    koff = ATTN_HEADS * HEAD_DIM
    for hh in range(KV_HEADS):
        kn = _rms(za[:, koff + hh * HEAD_DIM:koff + (hh + 1) * HEAD_DIM]) * kg
        kr = kn * cq + pltpu.roll(kn, HEAD_DIM // 2, 1) * sq
        k_o[:, hh * HEAD_DIM:(hh + 1) * HEAD_DIM] = kr
        kb_o[:, hh * HEAD_DIM:(hh + 1) * HEAD_DIM] = kr.astype(BF16)
    voff = koff + KV_HEADS * HEAD_DIM
    v = za[:, voff:voff + KV_HEADS * HEAD_DIM]
    v_o[...] = v
    vb_o[...] = v.astype(BF16)

    ci, sa, sb = ci_ref[...], sa_ref[...], sb_ref[...]
    first = lax.broadcasted_iota(I32, (tm, LANES), 1) < IDX_DIM

    def rope64(t):
        return t * ci + pltpu.roll(t, LANES - IDX_DIM // 2, 1) * sa + pltpu.roll(t, IDX_DIM // 2, 1) * sb

    for hh in range(IDX_HEADS):
        r = rope64(zi[:, hh * LANES:(hh + 1) * LANES]) * (IDX_DIM ** -0.5)
        hi = r.astype(BF16).astype(F32)
        qi3_o[hh, :, 0:LANES] = jnp.where(first, hi, r - hi).astype(BF16)
        qi3_o[hh, :, LANES:2 * LANES] = jnp.where(first, hi, 0.0).astype(BF16)
    r = rope64(zi[:, IDX_HEADS * LANES:(IDX_HEADS + 1) * LANES])
    ki_o[...] = r[:, :IDX_DIM]
    hi = r.astype(BF16).astype(F32)
    ki3_o[:, 0:LANES] = hi.astype(BF16)
    ki3_o[:, LANES:2 * LANES] = jnp.where(first, r - hi, 0.0).astype(BF16)
    misc = zi[:, (IDX_HEADS + 1) * LANES:(IDX_HEADS + 2) * LANES]
    lane = lax.broadcasted_iota(I32, (tm, LANES), 1)
    small_o[...] = jnp.where(lane < IDX_HEADS, misc * (IDX_HEADS ** -0.5), misc)
    gdim = gq_o.shape[1]
    gq_o[...] = zg[:, :gdim]
    gz_o[...] = zg[:, gdim:]


def _inproj(x3, sh, sc, g, wa, wih, wil, wg, qg, kg, tabs, tile_of, tab_of, tm):
    ng, _, d = x3.shape
    rows = ng * SUBLANES
    gt = tm // SUBLANES
    nt = rows // tm
    gdim = GDN_HEADS * 3 * HEAD_DIM
    zdim = GDN_HEADS * HEAD_DIM
    const = lambda t: (0, 0)
    row = lambda t: (t, 0)
    tab = lambda t: (tab_of(t), 0)
    in_specs = [
        pl.BlockSpec((gt, SUBLANES, d), lambda t: (t, 0, 0)),
        pl.BlockSpec((gt, 1, d), lambda t: (tile_of(t), 0, 0)),
        pl.BlockSpec((gt, 1, d), lambda t: (tile_of(t), 0, 0)),
        pl.BlockSpec((1, d), const),
        pl.BlockSpec(wa.shape, const),
        pl.BlockSpec(wih.shape, const),
        pl.BlockSpec(wil.shape, const),
        pl.BlockSpec(wg.shape, const),
        pl.BlockSpec((1, HEAD_DIM), const),
        pl.BlockSpec((1, HEAD_DIM), const),
    ] + [pl.BlockSpec((tm, LANES), tab)] * 5
    kvd = KV_HEADS * HEAD_DIM
    out_shape = [
        jax.ShapeDtypeStruct((rows, ATTN_HEADS * HEAD_DIM), BF16),
        jax.ShapeDtypeStruct((rows, kvd), F32),
        jax.ShapeDtypeStruct((rows, kvd), F32),
        jax.ShapeDtypeStruct((rows, kvd), BF16),
        jax.ShapeDtypeStruct((rows, kvd), BF16),
        jax.ShapeDtypeStruct((IDX_HEADS, rows, 2 * LANES), BF16),
        jax.ShapeDtypeStruct((rows, IDX_DIM), F32),
        jax.ShapeDtypeStruct((rows, 2 * LANES), BF16),
        jax.ShapeDtypeStruct((rows, LANES), F32),
        jax.ShapeDtypeStruct((rows, gdim), F32),
        jax.ShapeDtypeStruct((rows, zdim), F32),
    ]
    out_specs = [
        pl.BlockSpec((tm, ATTN_HEADS * HEAD_DIM), row),
        pl.BlockSpec((tm, kvd), row),
        pl.BlockSpec((tm, kvd), row),
        pl.BlockSpec((tm, kvd), row),
        pl.BlockSpec((tm, kvd), row),
        pl.BlockSpec((IDX_HEADS, tm, 2 * LANES), lambda t: (0, t, 0)),
        pl.BlockSpec((tm, IDX_DIM), row),
        pl.BlockSpec((tm, 2 * LANES), row),
        pl.BlockSpec((tm, LANES), row),
        pl.BlockSpec((tm, gdim), row),
        pl.BlockSpec((tm, zdim), row),
    ]
    return pl.pallas_call(
        _inproj_kernel,
        out_shape=out_shape,
        grid=(nt,),
        in_specs=in_specs,
        out_specs=out_specs,
        compiler_params=_cparams(("arbitrary",)),
        name="in_proj",
    )(x3, sh, sc, g, wa, wih, wil, wg, qg, kg, *tabs)


def _sort_key(score):
    bits = pltpu.bitcast(score, I32)
    return jnp.where(bits < 0, bits ^ jnp.int32(0x7FFFFFFF), bits)


def _count(key_scr, nkc, kc, rows, n, preds):
    def body(c, accs):
        base = pl.multiple_of(c * kc, kc)
        accs = list(accs)
        for j in range(kc // LANES):
            sc = key_scr[:, pl.ds(base + j * LANES, LANES)]
            ps = preds(sc, base + j * LANES)
            for i in range(n):
                accs[i] = accs[i] + jnp.where(ps[i], 1.0, 0.0)
        return tuple(accs)

    accs = lax.fori_loop(0, nkc, body, tuple(jnp.zeros((rows, LANES), F32) for _ in range(n)))
    return [jnp.sum(a, axis=1, keepdims=True) for a in accs]


def _select_threshold(key_scr, nkc, kc, rows, n_sel, idx_bits, resolve_ties=True):
    n_sel_f = float(n_sel)

    def bit_step(b, acc):
        cand = acc | (jnp.int32(1) << (31 - b))
        cand_s = jnp.broadcast_to(cand ^ jnp.int32(INT_MIN), (rows, LANES))
        cnt, = _count(key_scr, nkc, kc, rows, 1, lambda kk, c0: (kk >= cand_s,))
        return jnp.where(cnt >= n_sel_f, cand, acc)

    acc = lax.fori_loop(0, 32, bit_step, jnp.zeros((rows, 1), I32))
    thr = acc ^ jnp.int32(INT_MIN)
    thr_b = jnp.broadcast_to(thr, (rows, LANES))
    cnt_gt, cnt_eq = _count(key_scr, nkc, kc, rows, 2, lambda kk, c0: (kk > thr_b, kk == thr_b))
    need = n_sel_f - cnt_gt
    excess = (acc != 0) & (cnt_eq > need)
    any_excess = jnp.max(jnp.where(excess, 1.0, 0.0)) > 0.0
    lane = lax.broadcasted_iota(I32, (rows, LANES), 1)

    def resolve():
        def idx_step(b, p):
            cand = p | (jnp.int32(1) << (idx_bits - 1 - b))
            cand_b = jnp.broadcast_to(cand, (rows, LANES))
            cnt, = _count(key_scr, nkc, kc, rows, 1,
                          lambda kk, c0: ((kk == thr_b) & ((lane + c0) < cand_b),))
            return jnp.where(cnt < need, cand, p)

        cut = lax.fori_loop(0, idx_bits, idx_step, jnp.zeros((rows, 1), I32))
        cut_b = jnp.broadcast_to(cut, (rows, LANES))
        drop_row = jnp.broadcast_to(excess, (rows, LANES))

        def drop(c, carry):
            base = pl.multiple_of(c * kc, kc)
            for j in range(kc // LANES):
                sl = pl.ds(base + j * LANES, LANES)
                kk = key_scr[:, sl]
                kill = drop_row & (kk == thr_b) & ((lane + (base + j * LANES)) > cut_b)
                key_scr[:, sl] = jnp.where(kill, jnp.int32(INT_MIN), kk)
            return carry

        lax.fori_loop(0, nkc, drop, 0)

    if resolve_ties:
        pl.when(any_excess)(resolve)
    return jnp.maximum(thr, jnp.int32(INT_MIN + 1)), any_excess


LANE_TOP = 12
STREAMS = 2


def _dsa_prompt_kernel(qi3_ref, small_ref, q_ref, ki3_ref, k_ref, v_ref, o_ref, key_scr, w_scr, cand_scr,
                       m_scr, acc_scr, *, tq, kc, n_sel, idx_bits):
    i = pl.program_id(1)
    nkc = (i * tq + tq + kc - 1) // kc
    q3 = qi3_ref[...].reshape(IDX_HEADS * tq, 2 * LANES)
    wi = small_ref[:, WI_LANE:WI_LANE + IDX_HEADS]
    for h in range(IDX_HEADS):
        w_scr[h] = jnp.broadcast_to(wi[:, h:h + 1], (tq, LANES))
    row = i * tq + lax.broadcasted_iota(I32, (tq, LANES), 0)
    lane = lax.broadcasted_iota(I32, (tq, LANES), 1)
    cand_chunks = LANE_TOP * LANES // kc
    reduce_keys = nkc > cand_chunks
    cand_scr[...] = jnp.full(cand_scr.shape, INT_MIN, I32)

    def score_chunk(c, carry):
        base = pl.multiple_of(c * kc, kc)
        s = _dot_nt(q3, ki3_ref[pl.ds(base, kc), :])
        for j in range(kc // LANES):
            acc = None
            for h in range(IDX_HEADS):
                t = w_scr[h] * jnp.maximum(s[h * tq:(h + 1) * tq, j * LANES:(j + 1) * LANES], 0.0)
                acc = t if acc is None else acc + t
            valid = (lane + (base + j * LANES)) <= row
            key = jnp.where(valid, _sort_key(acc), jnp.int32(INT_MIN))
            key_scr[:, pl.ds(base + j * LANES, LANES)] = key
            for t in range(LANE_TOP):
                kept = cand_scr[:, t * LANES:(t + 1) * LANES]
                cand_scr[:, t * LANES:(t + 1) * LANES] = jnp.maximum(kept, key)
                key = jnp.minimum(kept, key)
        return carry

    lax.fori_loop(0, nkc, score_chunk, 0)

    def from_candidates():
        thr_c, tied = _select_threshold(cand_scr, cand_chunks, kc, tq, n_sel, idx_bits, resolve_ties=False)
        last = cand_scr[:, (LANE_TOP - 1) * LANES:LANE_TOP * LANES]
        dropped = jnp.max(jnp.where(last >= jnp.broadcast_to(thr_c, (tq, LANES)), 1.0, 0.0)) > 0.0
        return thr_c, jnp.logical_or(tied, dropped)

    thr_c, redo = lax.cond(reduce_keys, from_candidates,
                           lambda: (jnp.zeros((tq, 1), I32), jnp.bool_(True)))
    thr = lax.cond(redo, lambda: _select_threshold(key_scr, nkc, kc, tq, n_sel, idx_bits)[0], lambda: thr_c)
    thr_b = jnp.broadcast_to(thr, (tq, LANES))

    qs = []
    for g in range(KV_HEADS):
        qs.append(jnp.concatenate(
            [q_ref[:, (g * GROUP + a) * HEAD_DIM:(g * GROUP + a + 1) * HEAD_DIM] for a in range(GROUP)], axis=0))

    rq = GROUP * tq
    ones = jnp.ones((kc, HEAD_DIM), BF16)
    m_scr[...] = jnp.full(m_scr.shape, NEG, F32)
    acc_scr[...] = jnp.zeros(acc_scr.shape, F32)

    def attend_step(cc, carry):
        for st in range(STREAMS):
            c = cc * STREAMS + st
            live = c < nkc
            base = pl.multiple_of(jnp.minimum(c, nkc - 1) * kc, kc)
            biases = []
            for j in range(kc // LANES):
                sel = jnp.logical_and(key_scr[:, pl.ds(base + j * LANES, LANES)] >= thr_b, live)
                biases.append(jnp.where(sel, 0.0, NEG))
            bias = jnp.concatenate(biases, axis=1)
            bias = jnp.concatenate([bias] * GROUP, axis=0)
            for g in range(KV_HEADS):
                slot = st * KV_HEADS + g
                kg = k_ref[pl.ds(base, kc), g * HEAD_DIM:(g + 1) * HEAD_DIM]
                vg = jnp.concatenate([v_ref[pl.ds(base, kc), g * HEAD_DIM:(g + 1) * HEAD_DIM], ones], axis=1)
                s = _dot_nt(qs[g], kg) + bias
                m = m_scr[slot]
                m_new = jnp.maximum(m, jnp.broadcast_to(jnp.max(s, axis=1, keepdims=True), (rq, LANES)))
                alpha = jnp.exp(m - m_new)
                p = jnp.concatenate([jnp.exp(s[:, j * LANES:(j + 1) * LANES] - m_new)
                                     for j in range(kc // LANES)], axis=1)
                m_scr[slot] = m_new
                acc_scr[slot] = jnp.concatenate([alpha, alpha], axis=1) * acc_scr[slot] + _dot(p.astype(BF16), vg)
        return carry

    lax.fori_loop(0, (nkc + STREAMS - 1) // STREAMS, attend_step, 0)
    for g in range(KV_HEADS):
        m_all = m_scr[g]
        for st in range(1, STREAMS):
            m_all = jnp.maximum(m_all, m_scr[st * KV_HEADS + g])
        acc = None
        for st in range(STREAMS):
            w = jnp.exp(m_scr[st * KV_HEADS + g] - m_all)
            part = jnp.concatenate([w, w], axis=1) * acc_scr[st * KV_HEADS + g]
            acc = part if acc is None else acc + part
        o = acc[:, :HEAD_DIM] / acc[:, HEAD_DIM:]
        for a in range(GROUP):
            hh = g * GROUP + a
            o_ref[:, hh * HEAD_DIM:(hh + 1) * HEAD_DIM] = o[a * tq:(a + 1) * tq].astype(BF16)


def _dsa_prompt(qi3, small, q_bf, ki3, k_bf, v_bf, batch, seq):
    tq = LANES
    kc = min(512, seq)
    nq = seq // tq
    n_sel = min(TOPK_MAX, seq // 4)
    idx_bits = max(1, (seq - 1).bit_length())
    kvd = KV_HEADS * HEAD_DIM
    qrow = lambda b, i: (b * nq + i, 0)
    kern = functools.partial(_dsa_prompt_kernel, tq=tq, kc=kc, n_sel=n_sel, idx_bits=idx_bits)
    return pl.pallas_call(
        kern,
        out_shape=jax.ShapeDtypeStruct((batch * seq, ATTN_HEADS * HEAD_DIM), BF16),
        grid=(batch, nq),
        in_specs=[
            pl.BlockSpec((IDX_HEADS, tq, 2 * LANES), lambda b, i: (0, b * nq + i, 0)),
            pl.BlockSpec((tq, LANES), qrow),
            pl.BlockSpec((tq, ATTN_HEADS * HEAD_DIM), qrow),
            pl.BlockSpec((seq, 2 * LANES), lambda b, i: (b, 0)),
            pl.BlockSpec((seq, kvd), lambda b, i: (b, 0)),
            pl.BlockSpec((seq, kvd), lambda b, i: (b, 0)),
        ],
        out_specs=pl.BlockSpec((tq, ATTN_HEADS * HEAD_DIM), qrow),
        scratch_shapes=[pltpu.VMEM((tq, seq), I32), pltpu.VMEM((IDX_HEADS, tq, LANES), F32),
                        pltpu.VMEM((tq, LANE_TOP * LANES), I32),
                        pltpu.VMEM((STREAMS * KV_HEADS, GROUP * tq, LANES), F32),
                        pltpu.VMEM((STREAMS * KV_HEADS, GROUP * tq, 2 * HEAD_DIM), F32)],
        compiler_params=_cparams(("arbitrary", "arbitrary")),
        name="dsa_prompt",
    )(qi3, small, q_bf, ki3, k_bf, v_bf)


SEQ_PER_STEP = 2


def _sample_scores_kernel(pt_ref, qi3_ref, small_ref, ki3n_ref, *rest, n_pages, t_len):
    pages = rest[:SEQ_PER_STEP * n_pages]
    s_o = rest[SEQ_PER_STEP * n_pages]
    nr = SEQ_PER_STEP * t_len
    q3 = qi3_ref[...].reshape(IDX_HEADS * nr, 2 * LANES)
    wi = small_ref[:, WI_LANE:WI_LANE + IDX_HEADS]
    knew = jnp.concatenate([ki3n_ref[...], jnp.zeros((PAGE - nr, 2 * LANES), BF16)], axis=0)
    zpad = jnp.zeros((2 * LANES - 3 * IDX_DIM, PAGE), F32)
    lane = lax.broadcasted_iota(I32, (t_len, LANES), 1)
    trow = lax.broadcasted_iota(I32, (t_len, LANES), 0)

    def head_sum(s, j):
        acc = None
        for h in range(IDX_HEADS):
            r0 = h * nr + j * t_len
            t = wi[j * t_len:(j + 1) * t_len, h:h + 1] * jnp.maximum(s[r0:r0 + t_len], 0.0)
            acc = t if acc is None else acc + t
        return acc

    for j in range(SEQ_PER_STEP):
        for p in range(n_pages):
            kp = pages[j * n_pages + p][...]
            hi = kp.astype(BF16).astype(F32)
            k3 = jnp.concatenate([hi, hi, kp - hi, zpad], axis=0).astype(BF16)
            s_o[j, :, p * PAGE:(p + 1) * PAGE] = head_sum(_dot(q3, k3), j)
        sn = head_sum(_dot_nt(q3, knew), j)
        ok = (lane >= j * t_len) & (lane - j * t_len <= trow)
        s_o[j, :, n_pages * PAGE:(n_pages + 1) * PAGE] = jnp.where(ok, sn, -jnp.inf)


def _sample_scores(page_table, qi3, small, ki3, cache_kidx_t, layer, rows_p, t_len):
    db, n_pages = page_table.shape
    nr = SEQ_PER_STEP * t_len
    base = rows_p // nr
    lp = (n_pages + 1) * PAGE
    in_specs = [
        pl.BlockSpec((IDX_HEADS, nr, 2 * LANES), lambda n, pt: (0, base + n, 0)),
        pl.BlockSpec((nr, LANES), lambda n, pt: (base + n, 0)),
        pl.BlockSpec((nr, 2 * LANES), lambda n, pt: (base + n, 0)),
    ]
    for j in range(SEQ_PER_STEP):
        for p in range(n_pages):
            in_specs.append(pl.BlockSpec(
                (None, None, IDX_DIM, PAGE),
                lambda n, pt, j=j, p=p: (pt[n * SEQ_PER_STEP + j, p], layer, 0, 0)))
    kern = functools.partial(_sample_scores_kernel, n_pages=n_pages, t_len=t_len)
    return pl.pallas_call(
        kern,
        out_shape=jax.ShapeDtypeStruct((db, t_len, lp), F32),
        grid_spec=pltpu.PrefetchScalarGridSpec(
            num_scalar_prefetch=1,
            grid=(db // SEQ_PER_STEP,),
            in_specs=in_specs,
            out_specs=pl.BlockSpec((SEQ_PER_STEP, t_len, lp), lambda n, pt: (n, 0, 0)),
        ),
        compiler_params=_cparams(("arbitrary",)),
        name="dsa_sample_scores",
    )(page_table, qi3, small, ki3, *([cache_kidx_t] * (SEQ_PER_STEP * n_pages)))


def _sample_select_kernel(s_ref, b_ref, key_scr, *, n_sel, idx_bits):
    rows, lp = s_ref.shape
    nkc = lp // LANES
    for c in range(nkc):
        sc = s_ref[:, c * LANES:(c + 1) * LANES]
        key_scr[:, c * LANES:(c + 1) * LANES] = jnp.where(sc > -jnp.inf, _sort_key(sc), jnp.int32(INT_MIN))
    thr, _ = _select_threshold(key_scr, nkc, LANES, rows, n_sel, idx_bits)
    thr_b = jnp.broadcast_to(thr, (rows, LANES))
    for c in range(nkc):
        b_ref[:, c * LANES:(c + 1) * LANES] = jnp.where(key_scr[:, c * LANES:(c + 1) * LANES] >= thr_b, 0.0, NEG)


def _sample_select(scores2d, n_sel):
    rows, lp = scores2d.shape
    tr = min(LANES, rows)
    kern = functools.partial(_sample_select_kernel, n_sel=n_sel, idx_bits=max(1, (lp - 1).bit_length()))
    return pl.pallas_call(
        kern,
        out_shape=jax.ShapeDtypeStruct((rows, lp), F32),
        grid=(rows // tr,),
        in_specs=[pl.BlockSpec((tr, lp), lambda r: (r, 0))],
        out_specs=pl.BlockSpec((tr, lp), lambda r: (r, 0)),
        scratch_shapes=[pltpu.VMEM((tr, lp), I32)],
        compiler_params=_cparams(("arbitrary",)),
        name="dsa_sample_select",
    )(scores2d)


def _sample_attend_kernel(pt_ref, q_ref, bias_ref, kn_ref, vn_ref, *rest, n_pages, t_len):
    npg = SEQ_PER_STEP * n_pages
    kpages, vpages = rest[:npg], rest[npg:2 * npg]
    o_ref = rest[2 * npg]
    kc_scr, vc_scr = rest[2 * npg + 1], rest[2 * npg + 2]
    nr = SEQ_PER_STEP * t_len
    lp = (n_pages + 1) * PAGE
    kvd = KV_HEADS * HEAD_DIM
    bias = jnp.concatenate([bias_ref[...]] * GROUP, axis=0)
    pad = jnp.zeros((PAGE - nr, kvd), BF16)
    kc_scr[n_pages * PAGE:lp, :] = jnp.concatenate([kn_ref[...], pad], axis=0)
    vc_scr[n_pages * PAGE:lp, :] = jnp.concatenate([vn_ref[...], pad], axis=0)
    for j in range(SEQ_PER_STEP):
        for p in range(n_pages):
            for g in range(KV_HEADS):
                head_rows = pl.ds(g, PAGE, stride=KV_HEADS)
                cols = slice(g * HEAD_DIM, (g + 1) * HEAD_DIM)
                kc_scr[p * PAGE:(p + 1) * PAGE, cols] = kpages[j * n_pages + p][head_rows, :].astype(BF16)
                vc_scr[p * PAGE:(p + 1) * PAGE, cols] = vpages[j * n_pages + p][head_rows, :].astype(BF16)
        for g in range(KV_HEADS):
            qs = jnp.concatenate(
                [q_ref[:, (g * GROUP + a) * HEAD_DIM:(g * GROUP + a + 1) * HEAD_DIM] for a in range(GROUP)],
                axis=0)
            s = _dot_nt(qs, kc_scr[:, g * HEAD_DIM:(g + 1) * HEAD_DIM]) + bias
            m = jnp.max(s, axis=1, keepdims=True)
            p_ = jnp.exp(s - m)
            l = jnp.sum(p_, axis=1, keepdims=True)
            o = _dot(p_.astype(BF16), vc_scr[:, g * HEAD_DIM:(g + 1) * HEAD_DIM]) / l
            for a in range(GROUP):
                hh = g * GROUP + a
                r0 = a * nr + j * t_len
                o_ref[j * t_len:(j + 1) * t_len, hh * HEAD_DIM:(hh + 1) * HEAD_DIM] = (
                    o[r0:r0 + t_len].astype(BF16))


def _sample_attend(page_table, q_bf, bias2d, k_bf, v_bf, cache_k2, cache_v2, depth, layer, rows_p, t_len):
    db, n_pages = page_table.shape
    nr = SEQ_PER_STEP * t_len
    base = rows_p // nr
    lp = (n_pages + 1) * PAGE
    kvd = KV_HEADS * HEAD_DIM
    in_specs = [
        pl.BlockSpec((nr, ATTN_HEADS * HEAD_DIM), lambda n, pt: (base + n, 0)),
        pl.BlockSpec((nr, lp), lambda n, pt: (n, 0)),
        pl.BlockSpec((nr, kvd), lambda n, pt: (base + n, 0)),
        pl.BlockSpec((nr, kvd), lambda n, pt: (base + n, 0)),
    ]
    for _ in range(2):
        for j in range(SEQ_PER_STEP):
            for p in range(n_pages):
                in_specs.append(pl.BlockSpec(
                    (PAGE * KV_HEADS, HEAD_DIM),
                    lambda n, pt, j=j, p=p: (pt[n * SEQ_PER_STEP + j, p] * depth + layer, 0)))
    kern = functools.partial(_sample_attend_kernel, n_pages=n_pages, t_len=t_len)
    npg = SEQ_PER_STEP * n_pages
    return pl.pallas_call(
        kern,
        out_shape=jax.ShapeDtypeStruct((db * t_len, ATTN_HEADS * HEAD_DIM), BF16),
        grid_spec=pltpu.PrefetchScalarGridSpec(
            num_scalar_prefetch=1,
            grid=(db // SEQ_PER_STEP,),
            in_specs=in_specs,
            out_specs=pl.BlockSpec((nr, ATTN_HEADS * HEAD_DIM), lambda n, pt: (n, 0)),
            scratch_shapes=[pltpu.VMEM((lp, kvd), BF16), pltpu.VMEM((lp, kvd), BF16)],
        ),
        compiler_params=_cparams(("arbitrary",)),
        name="dsa_sample_attend",
    )(page_table, q_bf, bias2d, k_bf, v_bf, *([cache_k2] * npg), *([cache_v2] * npg))


def _chunk_masks(n, chunk):
    ri = lax.broadcasted_iota(I32, (n, n), 0)
    ci = lax.broadcasted_iota(I32, (n, n), 1)

    def same(size):
        sh = size.bit_length() - 1
        return (ri >> sh) == (ci >> sh)

    same_c = same(chunk)
    incl = same_c & (ci <= ri)
    strict = same_c & (ci < ri)
    base = min(SUBLANES, chunk)
    levels = []
    s = base
    while s < chunk:
        levels.append(same(2 * s) & jnp.logical_not(same(s)))
        s *= 2
    return incl, strict, same(base), levels, (ri == ci)


def _unit_lower_inverse(a_heads, same_base, levels, eye):
    ident = jnp.where(eye, 1.0, 0.0)
    ad = [jnp.where(same_base, a, 0.0) for a in a_heads]
    a2 = [_mm3(x, x) for x in ad]
    a4 = [_mm3(x, x) for x in a2]
    t = [_mm3(ident - x, ident + y) for x, y in zip(ad, a2)]
    t = [_mm3(x, ident + y) for x, y in zip(t, a4)]
    for lv in levels:
        to = [_mm3(x, jnp.where(lv, a, 0.0)) for x, a in zip(t, a_heads)]
        t = [x - _mm3(y, x) for x, y in zip(t, to)]
    return t


def _mm1(a, b, dot=_dot):
    return dot(a.astype(BF16), b.astype(BF16))


def _gdn_intra(q, k, v, beta_c, gc_c, gc_r, masks):
    incl, strict, same_base, levels, eye = masks
    heads = range(len(q))
    decay = [jnp.exp(jnp.where(incl, gc_c[h] - gc_r[h], -jnp.inf)) for h in heads]
    kb = [k[h] * beta_c[h] for h in heads]
    a = [jnp.where(strict, _mm1(kb[h], k[h], _dot_nt) * decay[h], 0.0) for h in heads]
    t = _unit_lower_inverse(a, same_base, levels, eye)
    uw = [_mm3(t[h], jnp.concatenate([v[h] * beta_c[h], kb[h] * jnp.exp(gc_c[h])], axis=1)) for h in heads]
    qk = [_mm1(q[h], k[h], _dot_nt) * decay[h] for h in heads]
    return [x[:, :HEAD_DIM] for x in uw], [x[:, HEAD_DIM:] for x in uw], qk


def _l2(x):
    return x * lax.rsqrt(jnp.sum(x * x, axis=-1, keepdims=True) + EPS)


def _cum_matrices(n, chunk):
    ri = lax.broadcasted_iota(I32, (n, n), 0)
    ci = lax.broadcasted_iota(I32, (n, n), 1)
    sh = chunk.bit_length() - 1
    same = (ri >> sh) == (ci >> sh)
    lower = jnp.where(same & (ci <= ri), 1.0, 0.0).astype(BF16)
    upper = jnp.where(same & (ri <= ci), 1.0, 0.0).astype(BF16)
    return lower, upper


def _gdn_gates(sm, smt, alog_l, dtb_l, alog_c, dtb_c, chunk):
    n = sm.shape[0]
    lower, upper = _cum_matrices(n, chunk)
    g_tile = -jnp.exp(alog_l) * _softplus(sm + dtb_l)
    beta_tile = jax.nn.sigmoid(sm)
    gc_cols = _mm_exact_lhs(lower, g_tile)
    g_rows = -jnp.exp(alog_c) * _softplus(smt + dtb_c)
    gc_rows = _mm_exact_rhs(g_rows, upper)
    return beta_tile, gc_cols, gc_rows


def _gdn_prompt_kernel(gq_ref, gz_ref, sm_ref, smt_ref, cw_ref, alog_l, dtb_l, alog_c, dtb_c, ng_ref,
                       o_ref, s_o_ref, stage, s_scr, *, tt, chunk):
    t_idx = pl.program_id(1)
    hd = HEAD_DIM
    nh = GDN_HEADS

    @pl.when(t_idx == 0)
    def _():
        stage[0:SUBLANES, :] = jnp.zeros((SUBLANES, stage.shape[1]), F32)
        s_scr[...] = jnp.zeros(s_scr.shape, F32)

    x = gq_ref[...]
    stage[SUBLANES:SUBLANES + tt, :] = x
    y = None
    for j in range(CONV_W):
        term = stage[pl.ds(SUBLANES - (CONV_W - 1) + j, tt), :] * cw_ref[j:j + 1, :]
        y = term if y is None else y + term
    stage[0:SUBLANES, :] = x[tt - SUBLANES:tt, :]
    y = _silu(y)

    beta_tile, gc_cols, gc_rows = _gdn_gates(sm_ref[...], smt_ref[...], alog_l[...], dtb_l[...],
                                             alog_c[...], dtb_c[...], chunk)
    masks = _chunk_masks(tt, chunk)
    ng = ng_ref[...]
    heads = range(nh)
    q = [_l2(y[:, h * hd:(h + 1) * hd]) * (hd ** -0.5) for h in heads]
    k = [_l2(y[:, (nh + h) * hd:(nh + h + 1) * hd]) for h in heads]
    v = [y[:, (2 * nh + h) * hd:(2 * nh + h + 1) * hd] for h in heads]
    beta_c = [beta_tile[:, GB_LANE + h:GB_LANE + h + 1] for h in heads]
    gc_c = [gc_cols[:, GA_LANE + h:GA_LANE + h + 1] for h in heads]
    gc_r = [gc_rows[h:h + 1, :] for h in heads]
    u, w, qk = _gdn_intra(q, k, v, beta_c, gc_c, gc_r, masks)
    qg = [q[h] * jnp.exp(gc_c[h]) for h in heads]
    s = [s_scr[h] for h in heads]
    vnew = [[] for _ in heads]
    ointer = [[] for _ in heads]
    for c in range(tt // chunk):
        r = slice(c * chunk, (c + 1) * chunk)
        for h in heads:
            vn = u[h][r] - _mm1(w[h][r], s[h])
            ointer[h].append(_mm1(qg[h][r], s[h]))
            g_last = gc_c[h][(c + 1) * chunk - 1:(c + 1) * chunk, :]
            kdec = k[h][r] * jnp.exp(g_last - gc_c[h][r])
            s[h] = s[h] * jnp.exp(g_last) + _mm1(kdec, vn, _dot_tn)
            vnew[h].append(vn)
    for h in heads:
        s_scr[h] = s[h]
        o = jnp.concatenate(ointer[h], axis=0) + _mm1(qk[h], jnp.concatenate(vnew[h], axis=0))
        o = _rms(o) * ng * _silu(gz_ref[:, h * hd:(h + 1) * hd])
        o_ref[:, h * hd:(h + 1) * hd] = o.astype(BF16)

    @pl.when(t_idx == pl.num_programs(1) - 1)
    def _():
        s_o_ref[...] = s_scr[...]


def _gdn_prompt(gq, gz, small, small_t, conv_w_l, alog_l, dtb_l, alog_c, dtb_c, ng, batch, seq):
    tt = min(256, seq)
    chunk = min(GDN_CHUNK, seq)
    nt = seq // tt
    gdim = gq.shape[1]
    zdim = gz.shape[1]
    row = lambda b, t: (b * nt + t, 0)
    const = lambda b, t: (0, 0)
    kern = functools.partial(_gdn_prompt_kernel, tt=tt, chunk=chunk)
    return pl.pallas_call(
        kern,
        out_shape=[jax.ShapeDtypeStruct((batch * seq, zdim), BF16),
                   jax.ShapeDtypeStruct((batch, GDN_HEADS, HEAD_DIM, HEAD_DIM), F32)],
        grid=(batch, nt),
        in_specs=[
            pl.BlockSpec((tt, gdim), row),
            pl.BlockSpec((tt, zdim), row),
            pl.BlockSpec((tt, LANES), row),
            pl.BlockSpec((SUBLANES, tt), lambda b, t: (0, b * nt + t)),
            pl.BlockSpec((CONV_W, gdim), const),
            pl.BlockSpec((1, LANES), const),
            pl.BlockSpec((1, LANES), const),
            pl.BlockSpec((SUBLANES, 1), const),
            pl.BlockSpec((SUBLANES, 1), const),
            pl.BlockSpec((1, HEAD_DIM), const),
        ],
        out_specs=[pl.BlockSpec((tt, zdim), row),
                   pl.BlockSpec((None, GDN_HEADS, HEAD_DIM, HEAD_DIM), lambda b, t: (b, 0, 0, 0))],
        scratch_shapes=[pltpu.VMEM((tt + SUBLANES, gdim), F32),
                        pltpu.VMEM((GDN_HEADS, HEAD_DIM, HEAD_DIM), F32)],
        compiler_params=_cparams(("arbitrary", "arbitrary")),
        name="gdn_prompt",
    )(gq, gz, small, small_t, conv_w_l, alog_l, dtb_l, alog_c, dtb_c, ng)


def _gdn_sample_kernel(gq_ref, gz_ref, sm_ref, smt_ref, cst_ref, s0_ref, cw_ref, alog_l, dtb_l, alog_c,
                       dtb_c, ng_ref, o_ref, s_o_ref, stage, uw_scr, vn_scr, oi_scr, *, nb, t_len):
    hd = HEAD_DIM
    nh = GDN_HEADS
    n = nb * t_len
    gdim = gq_ref.shape[1]
    stage[:, 0:SUBLANES, :] = cst_ref[...].reshape(nb, SUBLANES, gdim)
    stage[:, SUBLANES:SUBLANES + t_len, :] = gq_ref[...].reshape(nb, t_len, gdim)
    y = None
    for j in range(CONV_W):
        term = stage[:, pl.ds(SUBLANES - (CONV_W - 1) + j, t_len), :] * cw_ref[j:j + 1, :]
        y = term if y is None else y + term
    y = _silu(y).reshape(n, gdim)

    beta_tile, gc_cols, gc_rows = _gdn_gates(sm_ref[...], smt_ref[...], alog_l[...], dtb_l[...],
                                             alog_c[...], dtb_c[...], t_len)
    masks = _chunk_masks(n, t_len)
    ng = ng_ref[...]
    ri = lax.broadcasted_iota(I32, (n, n), 0)
    ci = lax.broadcasted_iota(I32, (n, n), 1)
    sh = t_len.bit_length() - 1
    pick_last = jnp.where(((ri >> sh) == (ci >> sh)) & ((ci & (t_len - 1)) == t_len - 1), 1.0, 0.0).astype(BF16)
    g_last_cols = _mm_exact_lhs(pick_last, gc_cols)
    heads = range(nh)
    q = [_l2(y[:, h * hd:(h + 1) * hd]) * (hd ** -0.5) for h in heads]
    k = [_l2(y[:, (nh + h) * hd:(nh + h + 1) * hd]) for h in heads]
    v = [y[:, (2 * nh + h) * hd:(2 * nh + h + 1) * hd] for h in heads]
    beta_c = [beta_tile[:, GB_LANE + h:GB_LANE + h + 1] for h in heads]
    gc_c = [gc_cols[:, GA_LANE + h:GA_LANE + h + 1] for h in heads]
    gc_r = [gc_rows[h:h + 1, :] for h in heads]
    u, w, qk = _gdn_intra(q, k, v, beta_c, gc_c, gc_r, masks)
    for h in heads:
        g_last_c = g_last_cols[:, GA_LANE + h:GA_LANE + h + 1]
        uw_scr[h, 0] = u[h]
        uw_scr[h, 1] = w[h]
        uw_scr[h, 2] = q[h] * jnp.exp(gc_c[h])
        uw_scr[h, 3] = k[h] * jnp.exp(g_last_c - gc_c[h])
        uw_scr[h, 4] = jnp.broadcast_to(jnp.exp(g_last_c), (n, hd))

    def seq_step(i, carry):
        r0 = pl.multiple_of(i * t_len, t_len)
        rows = pl.ds(r0, t_len)
        for h in heads:
            s = s0_ref[i, h]
            vn = uw_scr[h, 0, rows, :] - _mm1(uw_scr[h, 1, rows, :], s)
            oi_scr[h, rows, :] = _mm1(uw_scr[h, 2, rows, :], s)
            vn_scr[h, rows, :] = vn
            dec = uw_scr[h, 4, pl.ds(r0, 1), :]
            s_o_ref[i, h] = s * dec + _mm1(uw_scr[h, 3, rows, :], vn, _dot_tn)
        return carry

    lax.fori_loop(0, nb, seq_step, 0)
    for h in heads:
        o = oi_scr[h] + _mm1(qk[h], vn_scr[h])
        o = _rms(o) * ng * _silu(gz_ref[:, h * hd:(h + 1) * hd])
        o_ref[:, h * hd:(h + 1) * hd] = o.astype(BF16)


def _gdn_sample(gq, gz, small, small_t, cstate, state_gdn, layer, conv_w_l, alog_l, dtb_l, alog_c, dtb_c, ng,
                rows_p, db, t_len):
    nb = min(16, db)
    n = nb * t_len
    base = rows_p // n
    gdim = gq.shape[1]
    zdim = gz.shape[1]
    row = lambda i: (base + i, 0)
    const = lambda i: (0, 0)
    kern = functools.partial(_gdn_sample_kernel, nb=nb, t_len=t_len)
    return pl.pallas_call(
        kern,
        out_shape=[jax.ShapeDtypeStruct((db * t_len, zdim), BF16),
                   jax.ShapeDtypeStruct((db, GDN_HEADS, HEAD_DIM, HEAD_DIM), F32)],
        grid=(db // nb,),
        in_specs=[
            pl.BlockSpec((n, gdim), row),
            pl.BlockSpec((n, zdim), row),
            pl.BlockSpec((n, LANES), row),
            pl.BlockSpec((SUBLANES, n), lambda i: (0, base + i)),
            pl.BlockSpec((n, gdim), lambda i: (i, 0)),
            pl.BlockSpec((nb, None, GDN_HEADS, HEAD_DIM, HEAD_DIM), lambda i: (i, layer, 0, 0, 0)),
            pl.BlockSpec((CONV_W, gdim), const),
            pl.BlockSpec((1, LANES), const),
            pl.BlockSpec((1, LANES), const),
            pl.BlockSpec((SUBLANES, 1), const),
            pl.BlockSpec((SUBLANES, 1), const),
            pl.BlockSpec((1, HEAD_DIM), const),
        ],
        out_specs=[pl.BlockSpec((n, zdim), lambda i: (i, 0)),
                   pl.BlockSpec((nb, GDN_HEADS, HEAD_DIM, HEAD_DIM), lambda i: (i, 0, 0, 0))],
        scratch_shapes=[pltpu.VMEM((nb, 2 * SUBLANES, gdim), F32),
                        pltpu.VMEM((GDN_HEADS, 5, n, HEAD_DIM), F32),
                        pltpu.VMEM((GDN_HEADS, n, HEAD_DIM), F32),
                        pltpu.VMEM((GDN_HEADS, n, HEAD_DIM), F32)],
        compiler_params=_cparams(("arbitrary",)),
        name="gdn_sample",
    )(gq, gz, small, small_t, cstate, state_gdn, conv_w_l, alog_l, dtb_l, alog_c, dtb_c, ng)


def _outproj_kernel(x_ref, att_ref, gdn_ref, gate_ref, sh_ref, sc_ref, g_ref, wo_ref, *rest, moe):
    if moe:
        rw_ref, rb_ref, x_o, h_o, lg_o = rest
    else:
        x_o, h_o = rest
    x = x_ref[...]
    adim = att_ref.shape[1]
    y = _dot(att_ref[...], wo_ref[0:adim, :]) + _dot(gdn_ref[...], wo_ref[adim:, :])
    xn = x + gate_ref[...] * y.reshape(x.shape)
    x_o[...] = xn
    h = (_rms(xn) * g_ref[...]) * (1.0 + sc_ref[...]) + sh_ref[...]
    h2 = h.reshape(y.shape)
    h_o[...] = h2.astype(BF16)
    if moe:
        lg_o[...] = _mm3(h2, rw_ref[...]) + rb_ref[...]


def _outproj(x3, att, gdn, gate, sh, sc, g, wo, router, tile_of, tm):
    ng, _, d = x3.shape
    rows = ng * SUBLANES
    gt = tm // SUBLANES
    moe = router is not None
    const = lambda t: (0, 0)
    row = lambda t: (t, 0)
    modspec = pl.BlockSpec((gt, 1, d), lambda t: (tile_of(t), 0, 0))
    in_specs = [
        pl.BlockSpec((gt, SUBLANES, d), lambda t: (t, 0, 0)),
        pl.BlockSpec((tm, att.shape[1]), row),
        pl.BlockSpec((tm, gdn.shape[1]), row),
        modspec, modspec, modspec,
        pl.BlockSpec((1, d), const),
        pl.BlockSpec(wo.shape, const),
    ]
    out_shape = [jax.ShapeDtypeStruct(x3.shape, F32), jax.ShapeDtypeStruct((rows, d), BF16)]
    out_specs = [pl.BlockSpec((gt, SUBLANES, d), lambda t: (t, 0, 0)), pl.BlockSpec((tm, d), row)]
    args = [x3, att, gdn, gate, sh, sc, g, wo]
    if moe:
        in_specs += [pl.BlockSpec((d, LANES), const), pl.BlockSpec((1, LANES), const)]
        out_shape.append(jax.ShapeDtypeStruct((rows, LANES), F32))
        out_specs.append(pl.BlockSpec((tm, LANES), row))
        args += list(router)
    return pl.pallas_call(
        functools.partial(_outproj_kernel, moe=moe),
        out_shape=out_shape,
        grid=(rows // tm,),
        in_specs=in_specs,
        out_specs=out_specs,
        compiler_params=_cparams(("arbitrary",)),
        name="out_proj",
    )(*args)


def _ffn_kernel(x_ref, h_ref, gate_ref, wg_ref, wu_ref, wd_ref, o_ref, *, fc):
    h = h_ref[...]
    dff = wg_ref.shape[1]
    acc = None
    for c in range(dff // fc):
        a = _dot(h, wg_ref[:, c * fc:(c + 1) * fc])
        u = _dot(h, wu_ref[:, c * fc:(c + 1) * fc])
        t = _dot((_silu(a) * u).astype(BF16), wd_ref[c * fc:(c + 1) * fc, :])
        acc = t if acc is None else acc + t
    x = x_ref[...]
    o_ref[...] = x + gate_ref[...] * acc.reshape(x.shape)


def _ffn(x3, hff, gate, wg, wu, wd, tile_of, tm):
    ng, _, d = x3.shape
    rows = ng * SUBLANES
    gt = tm // SUBLANES
    dff = wg.shape[1]
    fc = dff
    for cand in (1408, 1024, 768, 512, 256, 128):
        if dff % cand == 0:
            fc = cand
            break
    const = lambda t: (0, 0)
    return pl.pallas_call(
        functools.partial(_ffn_kernel, fc=fc),
        out_shape=jax.ShapeDtypeStruct(x3.shape, F32),
        grid=(rows // tm,),
        in_specs=[
            pl.BlockSpec((gt, SUBLANES, d), lambda t: (t, 0, 0)),
            pl.BlockSpec((tm, d), lambda t: (t, 0)),
            pl.BlockSpec((gt, 1, d), lambda t: (tile_of(t), 0, 0)),
            pl.BlockSpec(wg.shape, const),
            pl.BlockSpec(wu.shape, const),
            pl.BlockSpec(wd.shape, const),
        ],
        out_specs=pl.BlockSpec((gt, SUBLANES, d), lambda t: (t, 0, 0)),
        compiler_params=_cparams(("arbitrary",)),
        name="ffn_dense",
    )(x3, hff, gate, wg, wu, wd)


def _moe_kernel(x_ref, h_ref, lg_ref, gate_ref, wg_ref, wu_ref, wd_ref, o_ref, acc_scr, g_scr):
    e = pl.program_id(1)
    tm = h_ref.shape[0]

    @pl.when(e == 0)
    def _():
        lane = lax.broadcasted_iota(I32, (tm, LANES), 1)
        lg = jnp.where(lane < N_EXPERTS, lg_ref[...], -jnp.inf)
        m1 = jnp.max(lg, axis=1, keepdims=True)
        i1 = jnp.min(jnp.where(lg == m1, lane, LANES), axis=1, keepdims=True)
        rest = jnp.where(lane == i1, -jnp.inf, lg)
        m2 = jnp.max(rest, axis=1, keepdims=True)
        i2 = jnp.min(jnp.where(rest == m2, lane, LANES), axis=1, keepdims=True)
        e2 = jnp.exp(m2 - m1)
        den = 1.0 + e2
        g_scr[...] = jnp.where(lane == i1, 1.0 / den, 0.0) + jnp.where(lane == i2, e2 / den, 0.0)
        acc_scr[...] = jnp.zeros(acc_scr.shape, F32)

    h = h_ref[...]
    a = _dot(h, wg_ref[...])
    u = _dot(h, wu_ref[...])
    y = _dot((_silu(a) * u).astype(BF16), wd_ref[...])
    lane = lax.broadcasted_iota(I32, (tm, LANES), 1)
    ge = jnp.sum(jnp.where(lane == e, g_scr[...], 0.0), axis=1, keepdims=True)
    acc_scr[...] += ge * y

    @pl.when(e == pl.num_programs(1) - 1)
    def _():
        x = x_ref[...]
        o_ref[...] = x + gate_ref[...] * acc_scr[...].reshape(x.shape)


def _moe(x3, hff, logits, gate, wg, wu, wd, tile_of, tm):
    ng, _, d = x3.shape
    rows = ng * SUBLANES
    gt = tm // SUBLANES
    ne, _, eff = wg.shape
    return pl.pallas_call(
        _moe_kernel,
        out_shape=jax.ShapeDtypeStruct(x3.shape, F32),
        grid=(rows // tm, ne),
        in_specs=[
            pl.BlockSpec((gt, SUBLANES, d), lambda t, e: (t, 0, 0)),
            pl.BlockSpec((tm, d), lambda t, e: (t, 0)),
            pl.BlockSpec((tm, LANES), lambda t, e: (t, 0)),
            pl.BlockSpec((gt, 1, d), lambda t, e: (tile_of(t), 0, 0)),
            pl.BlockSpec((None, d, eff), lambda t, e: (e, 0, 0)),
            pl.BlockSpec((None, d, eff), lambda t, e: (e, 0, 0)),
            pl.BlockSpec((None, eff, d), lambda t, e: (e, 0, 0)),
        ],
        out_specs=pl.BlockSpec((gt, SUBLANES, d), lambda t, e: (t, 0, 0)),
        scratch_shapes=[pltpu.VMEM((tm, d), F32), pltpu.VMEM((tm, LANES), F32)],
        compiler_params=_cparams(("arbitrary", "arbitrary")),
        name="ffn_moe",
    )(x3, hff, logits, gate, wg, wu, wd)


def _rope_tables(pos):
    pos = pos.astype(F32)[:, None]
    half = HEAD_DIM // 2
    ang = pos * (ROPE_THETA ** (-jnp.arange(half, dtype=F32) / half))[None, :]
    c, s = jnp.cos(ang), jnp.sin(ang)
    cq = jnp.concatenate([c, c], axis=1)
    sq = jnp.concatenate([-s, s], axis=1)
    half = IDX_DIM // 2
    ang = pos * (ROPE_THETA ** (-jnp.arange(half, dtype=F32) / half))[None, :]
    c, s = jnp.cos(ang), jnp.sin(ang)
    z = jnp.zeros_like(s)
    ci = jnp.concatenate([c, c, c, c], axis=1)
    sa = jnp.concatenate([-s, z, -s, z], axis=1)
    sb = jnp.concatenate([z, s, z, s], axis=1)
    return cq, sq, ci, sa, sb


def _lane_vec(vals, lane0):
    return jnp.zeros((1, LANES), F32).at[0, lane0:lane0 + vals.shape[0]].set(vals)


def kernel(x_prompt, x_sample, cache_k, cache_v, cache_kidx, state_gdn, state_conv, page_table, c_prompt, c_sample, mod_w, mod_b, norm_mix_g, norm_ffn_g, w_in, q_norm_g, k_norm_g, conv_w, a_log, dt_bias, gdn_norm_g, w_out, ffn_w_gate, ffn_w_up, ffn_w_down, router_w, router_b, moe_w_gate, moe_w_up, moe_w_down):
    batch, seq, d = x_prompt.shape
    db, t_len, _ = x_sample.shape
    n_layers = mod_w.shape[0]
    n_pages = page_table.shape[1]
    past = n_pages * PAGE
    rows_p, rows_s = batch * seq, db * t_len
    rows = rows_p + rows_s
    assert t_len == SUBLANES and seq % LANES == 0
    tm = min(256, rows_s, seq)
    assert seq % tm == 0 and rows_s % tm == 0
    gt = tm // SUBLANES
    tiles_per_batch = seq // tm
    n_ptiles = rows_p // tm

    def tile_of(t):
        return jnp.where(t < n_ptiles, t // tiles_per_batch, batch + (t - n_ptiles))

    def tab_of(t):
        return jnp.where(t < n_ptiles, t % tiles_per_batch, tiles_per_batch)

    def groups(m):
        mp = jnp.repeat(m[:batch], gt, axis=0)
        return jnp.concatenate([mp, m[batch:batch + db]], axis=0)[:, None, :]

    x3 = jnp.concatenate([x_prompt.reshape(rows_p, d), x_sample.reshape(rows_s, d)], axis=0)
    x3 = x3.reshape(rows // SUBLANES, SUBLANES, d)

    n_c = batch + db
    c_all = jnp.concatenate([c_prompt, c_sample, jnp.zeros((-n_c % SUBLANES, d), F32)], axis=0)
    mods = _mods(c_all, mod_w, mod_b)

    pos = jnp.concatenate([jnp.arange(seq), past + (jnp.arange(tm) % t_len)])
    tabs = _rope_tables(pos)

    kvd = KV_HEADS * HEAD_DIM
    depth = cache_k.shape[1]
    cache_k2 = cache_k.reshape(-1, HEAD_DIM)
    cache_v2 = cache_v.reshape(-1, HEAD_DIM)
    cache_kidx_t = jnp.swapaxes(cache_kidx, 2, 3)
    n_sel_s = min(TOPK_MAX, (past + t_len) // 4)
    lp = (n_pages + 1) * PAGE

    outs = {name: [] for name in ("kp", "vp", "kip", "ks", "vs", "kis", "sp", "ss", "cp", "cs")}
    col = 0
    offs = []
    for size in (ATTN_HEADS * HEAD_DIM, kvd, kvd, IDX_HEADS * IDX_DIM, IDX_DIM, IDX_HEADS,
                 GDN_HEADS * 3 * HEAD_DIM, GDN_HEADS * HEAD_DIM, GDN_HEADS, GDN_HEADS):
        offs.append((col, col + size))
        col += size
    o_q, o_k, o_v, o_qi, o_ki, o_wi, o_gq, o_gz, o_ga, o_gb = offs

    for l in range(n_layers):
        m6 = [groups(mods[l, :, j * d:(j + 1) * d]) for j in range(6)]
        w = w_in[l]
        wa = w[:, o_q[0]:o_v[1]].astype(BF16)
        qi_w = w[:, o_qi[0]:o_qi[1]].reshape(d, IDX_HEADS, 1, IDX_DIM)
        qi_w = jnp.broadcast_to(qi_w, (d, IDX_HEADS, 2, IDX_DIM)).reshape(d, IDX_HEADS * LANES)
        ki_w = w[:, o_ki[0]:o_ki[1]]
        misc_w = jnp.concatenate([w[:, o_wi[0]:o_wi[1]], w[:, o_ga[0]:o_ga[1]], w[:, o_gb[0]:o_gb[1]],
                                  jnp.zeros((d, LANES - IDX_HEADS - 2 * GDN_HEADS), F32)], axis=1)
        wi_f = jnp.concatenate([qi_w, ki_w, ki_w, misc_w], axis=1)
        wih = wi_f.astype(BF16)
        wil = (wi_f - wih.astype(F32)).astype(BF16)
        wg = w[:, o_gq[0]:o_gz[1]].astype(BF16)

        (q_bf, k_f, v_f, k_bf, v_bf, qi3, ki_f, ki3, small, gq, gz) = _inproj(
            x3, m6[0], m6[1], norm_mix_g[l][None, :], wa, wih, wil, wg,
            q_norm_g[l][None, :], k_norm_g[l][None, :], tabs, tile_of, tab_of, tm)

        small_t = jnp.transpose(small[:, GA_LANE:GA_LANE + 2 * GDN_HEADS])
        alog_l = _lane_vec(a_log[l], GA_LANE)
        dtb_l = _lane_vec(dt_bias[l], GA_LANE)
        pad4 = jnp.zeros((SUBLANES - GDN_HEADS,), F32)
        alog_c = jnp.concatenate([a_log[l], pad4])[:, None]
        dtb_c = jnp.concatenate([dt_bias[l], pad4])[:, None]
        ng = gdn_norm_g[l][None, :]

        att_p = _dsa_prompt(qi3, small, q_bf, ki3, k_bf, v_bf, batch, seq)
        gdn_p, s_p = _gdn_prompt(gq, gz, small, small_t, conv_w[l], alog_l, dtb_l, alog_c, dtb_c, ng,
                                 batch, seq)

        scores = _sample_scores(page_table, qi3, small, ki3, cache_kidx_t, l, rows_p, t_len)
        bias = _sample_select(scores.reshape(rows_s, lp), n_sel_s)
        att_s = _sample_attend(page_table, q_bf, bias, k_bf, v_bf, cache_k2, cache_v2, depth, l, rows_p, t_len)
        cstate = jnp.pad(state_conv[:, l], ((0, 0), (SUBLANES - (CONV_W - 1), 0), (0, 0)))
        cstate = cstate.reshape(db * SUBLANES, cstate.shape[2])
        gdn_s, s_s = _gdn_sample(gq, gz, small, small_t, cstate, state_gdn, l, conv_w[l], alog_l, dtb_l,
                                 alog_c, dtb_c, ng, rows_p, db, t_len)

        att = jnp.concatenate([att_p, att_s], axis=0)
        gdn = jnp.concatenate([gdn_p, gdn_s], axis=0)
        i = l // 2
        router = None
        if l % 2 == 1:
            rw = jnp.concatenate([router_w[i], jnp.zeros((d, LANES - N_EXPERTS), F32)], axis=1)
            rb = jnp.concatenate([router_b[i], jnp.zeros((LANES - N_EXPERTS,), F32)])[None, :]
            router = (rw, rb)
        res = _outproj(x3, att, gdn, m6[2], m6[3], m6[4], norm_ffn_g[l][None, :], w_out[l].astype(BF16),
                       router, tile_of, tm)
        if l % 2 == 0:
            x3, hff = res
            x3 = _ffn(x3, hff, m6[5], ffn_w_gate[i].astype(BF16), ffn_w_up[i].astype(BF16),
                      ffn_w_down[i].astype(BF16), tile_of, tm)
        else:
            x3, hff, logits = res
            x3 = _moe(x3, hff, logits, m6[5], moe_w_gate[i].astype(BF16), moe_w_up[i].astype(BF16),
                      moe_w_down[i].astype(BF16), tile_of, tm)

        outs["kp"].append(k_f[:rows_p].reshape(batch, seq, KV_HEADS, HEAD_DIM))
        outs["vp"].append(v_f[:rows_p].reshape(batch, seq, KV_HEADS, HEAD_DIM))
        outs["kip"].append(ki_f[:rows_p].reshape(batch, seq, IDX_DIM))
        outs["ks"].append(k_f[rows_p:].reshape(db, t_len, KV_HEADS, HEAD_DIM))
        outs["vs"].append(v_f[rows_p:].reshape(db, t_len, KV_HEADS, HEAD_DIM))
        outs["kis"].append(ki_f[rows_p:].reshape(db, t_len, IDX_DIM))
        outs["sp"].append(s_p)
        outs["ss"].append(s_s)
        gq_p = gq[:rows_p].reshape(batch, seq, -1)
        gq_s = gq[rows_p:].reshape(db, t_len, -1)
        outs["cp"].append(gq_p[:, seq - (CONV_W - 1):])
        outs["cs"].append(gq_s[:, t_len - (CONV_W - 1):])

    x2 = x3.reshape(rows, d)
    st = lambda name: jnp.stack(outs[name], axis=1)
    return (x2[:rows_p].reshape(batch, seq, d), x2[rows_p:].reshape(db, t_len, d),
            st("kp"), st("vp"), st("kip"), st("ks"), st("vs"), st("kis"),
            st("sp"), st("ss"), st("cp"), st("cs"))
```

```python
import functools
import math

import jax
import jax.numpy as jnp
from jax import lax
from jax.experimental import pallas as pl
from jax.experimental.pallas import tpu as pltpu

F32 = jnp.float32
BF16 = jnp.bfloat16
I32 = jnp.int32

HEAD_DIM = 128
ATTN_HEADS = 4
KV_HEADS = 2
GROUP = ATTN_HEADS // KV_HEADS
IDX_HEADS = 8
IDX_DIM = 64
TOPK_MAX = 256
GDN_HEADS = 4
GDN_CHUNK = 64
CONV_W = 4
N_EXPERTS = 8
PAGE = 128
ROPE_THETA = 10000.0
EPS = 1e-6

LANES = 128
SUBLANES = 8
VMEM_LIMIT = 56 * 1024 * 1024
NEG = -1e30
INT_MIN = -2147483648
WI_LANE = 0
GA_LANE = 8
GB_LANE = 12


def _cparams(sem):
    return pltpu.CompilerParams(dimension_semantics=sem, vmem_limit_bytes=VMEM_LIMIT)


def _dot(a, b):
    return jnp.dot(a, b, preferred_element_type=F32)


def _dot_nt(a, b):
    return lax.dot_general(a, b, (((1,), (1,)), ((), ())), preferred_element_type=F32)


def _dot_tn(a, b):
    return lax.dot_general(a, b, (((0,), (0,)), ((), ())), preferred_element_type=F32)


def _split(x):
    hi = x.astype(BF16)
    lo = (x - hi.astype(F32)).astype(BF16)
    return hi, lo


def _mm3(a, b, dot=_dot):
    ah, al = _split(a)
    bh, bl = _split(b)
    return dot(ah, bh) + (dot(ah, bl) + dot(al, bh))


def _split_three(a):
    a1 = a.astype(BF16)
    r1 = a - a1.astype(F32)
    a2 = r1.astype(BF16)
    a3 = (r1 - a2.astype(F32)).astype(BF16)
    return a1, a2, a3


def _mm_exact_rhs(a, b_bf16):
    a1, a2, a3 = _split_three(a)
    return _dot(a1, b_bf16) + (_dot(a2, b_bf16) + _dot(a3, b_bf16))


def _mm_exact_lhs(m_bf16, a):
    a1, a2, a3 = _split_three(a)
    return _dot(m_bf16, a1) + (_dot(m_bf16, a2) + _dot(m_bf16, a3))


def _silu(x):
    return x * jax.nn.sigmoid(x)


def _softplus(x):
    return jnp.maximum(x, 0.0) + jnp.log1p(jnp.exp(-jnp.abs(x)))


def _mods_kernel(c_ref, w_ref, b_ref, o_ref):
    o_ref[...] = _mm3(_silu(c_ref[...]), w_ref[...]) + b_ref[...]


def _mods(c_all, mod_w, mod_b):
    n_layers, d, n6 = mod_w.shape
    rows = c_all.shape[0]
    tn = n6 // 4
    return pl.pallas_call(
        _mods_kernel,
        out_shape=jax.ShapeDtypeStruct((n_layers, rows, n6), F32),
        grid=(n_layers, n6 // tn),
        in_specs=[
            pl.BlockSpec((rows, d), lambda l, j: (0, 0)),
            pl.BlockSpec((None, d, tn), lambda l, j: (l, 0, j)),
            pl.BlockSpec((None, 1, tn), lambda l, j: (l, 0, j)),
        ],
        out_specs=pl.BlockSpec((None, rows, tn), lambda l, j: (l, 0, j)),
        compiler_params=_cparams(("arbitrary", "arbitrary")),
        name="ada_mods",
    )(c_all, mod_w, mod_b.reshape(n_layers, 1, n6))


def _rms(x):
    return x * lax.rsqrt(jnp.mean(x * x, axis=-1, keepdims=True) + EPS)


def _inproj_kernel(x_ref, sh_ref, sc_ref, g_ref, wa_ref, wih_ref, wil_ref, wg_ref, qg_ref, kg_ref,
                   cq_ref, sq_ref, ci_ref, sa_ref, sb_ref,
                   q_o, k_o, v_o, kb_o, vb_o, qi3_o, ki_o, ki3_o, small_o, gq_o, gz_o):
    x = x_ref[...]
    h = (_rms(x) * g_ref[...]) * (1.0 + sc_ref[...]) + sh_ref[...]
    tm = x.shape[0] * x.shape[1]
    h2 = h.reshape(tm, x.shape[2])
    hb, hl = _split(h2)
    za = _dot(hb, wa_ref[...])
    wih = wih_ref[...]
    zi = _dot(hb, wih) + (_dot(hb, wil_ref[...]) + _dot(hl, wih))
    zg = _dot(hb, wg_ref[...])

    cq, sq = cq_ref[...], sq_ref[...]
    qg, kg = qg_ref[...], kg_ref[...]
    for hh in range(ATTN_HEADS):
        qn = _rms(za[:, hh * HEAD_DIM:(hh + 1) * HEAD_DIM]) * qg
        qr = qn * cq + pltpu.roll(qn, HEAD_DIM // 2, 1) * sq
        q_o[:, hh * HEAD_DIM:(hh + 1) * HEAD_DIM] = (qr * (HEAD_DIM ** -0.5)).astype(BF16)
    koff = ATTN_HEADS * HEAD_DIM
    for hh in range(KV_HEADS):
        kn = _rms(za[:, koff + hh * HEAD_DIM:koff + (hh + 1) * HEAD_DIM]) * kg
        kr = kn * cq + pltpu.roll(kn, HEAD_DIM // 2, 1) * sq
        k_o[:, hh * HEAD_DIM:(hh + 1) * HEAD_DIM] = kr
        kb_o[:, hh * HEAD_DIM:(hh + 1) * HEAD_DIM] = kr.astype(BF16)
    voff = koff + KV_HEADS * HEAD_DIM
    v = za[:, voff:voff + KV_HEADS * HEAD_DIM]
    v_o[...] = v
    vb_o[...] = v.astype(BF16)

    ci, sa, sb = ci_ref[...], sa_ref[...], sb_ref[...]
    first = lax.broadcasted_iota(I32, (tm, LANES), 1) < IDX_DIM

    def rope64(t):
        return t * ci + pltpu.roll(t, LANES - IDX_DIM // 2, 1) * sa + pltpu.roll(t, IDX_DIM // 2, 1) * sb

    for hh in range(IDX_HEADS):
        r = rope64(zi[:, hh * LANES:(hh + 1) * LANES]) * (IDX_DIM ** -0.5)
        hi = r.astype(BF16).astype(F32)
        qi3_o[hh, :, 0:LANES] = jnp.where(first, hi, r - hi).astype(BF16)
        qi3_o[hh, :, LANES:2 * LANES] = jnp.where(first, hi, 0.0).astype(BF16)
    r = rope64(zi[:, IDX_HEADS * LANES:(IDX_HEADS + 1) * LANES])
    ki_o[...] = r[:, :IDX_DIM]
    hi = r.astype(BF16).astype(F32)
    ki3_o[:, 0:LANES] = hi.astype(BF16)
    ki3_o[:, LANES:2 * LANES] = jnp.where(first, r - hi, 0.0).astype(BF16)
    misc = zi[:, (IDX_HEADS + 1) * LANES:(IDX_HEADS + 2) * LANES]
    lane = lax.broadcasted_iota(I32, (tm, LANES), 1)
    small_o[...] = jnp.where(lane < IDX_HEADS, misc * (IDX_HEADS ** -0.5), misc)
    gdim = gq_o.shape[1]
    gq_o[...] = zg[:, :gdim]
    gz_o[...] = zg[:, gdim:]


def _inproj(x3, sh, sc, g, wa, wih, wil, wg, qg, kg, tabs, tile_of, tab_of, tm):
    ng, _, d = x3.shape
    rows = ng * SUBLANES
    gt = tm // SUBLANES
    nt = rows // tm
    gdim = GDN_HEADS * 3 * HEAD_DIM
    zdim = GDN_HEADS * HEAD_DIM
    const = lambda t: (0, 0)
    row = lambda t: (t, 0)
    tab = lambda t: (tab_of(t), 0)
    in_specs = [
        pl.BlockSpec((gt, SUBLANES, d), lambda t: (t, 0, 0)),
        pl.BlockSpec((gt, 1, d), lambda t: (tile_of(t), 0, 0)),
        pl.BlockSpec((gt, 1, d), lambda t: (tile_of(t), 0, 0)),
        pl.BlockSpec((1, d), const),
        pl.BlockSpec(wa.shape, const),
        pl.BlockSpec(wih.shape, const),
        pl.BlockSpec(wil.shape, const),
        pl.BlockSpec(wg.shape, const),
        pl.BlockSpec((1, HEAD_DIM), const),
        pl.BlockSpec((1, HEAD_DIM), const),
    ] + [pl.BlockSpec((tm, LANES), tab)] * 5
    kvd = KV_HEADS * HEAD_DIM
    out_shape = [
        jax.ShapeDtypeStruct((rows, ATTN_HEADS * HEAD_DIM), BF16),
        jax.ShapeDtypeStruct((rows, kvd), F32),
        jax.ShapeDtypeStruct((rows, kvd), F32),
        jax.ShapeDtypeStruct((rows, kvd), BF16),
        jax.ShapeDtypeStruct((rows, kvd), BF16),
        jax.ShapeDtypeStruct((IDX_HEADS, rows, 2 * LANES), BF16),
        jax.ShapeDtypeStruct((rows, IDX_DIM), F32),
        jax.ShapeDtypeStruct((rows, 2 * LANES), BF16),
        jax.ShapeDtypeStruct((rows, LANES), F32),
        jax.ShapeDtypeStruct((rows, gdim), F32),
        jax.ShapeDtypeStruct((rows, zdim), F32),
    ]
    out_specs = [
        pl.BlockSpec((tm, ATTN_HEADS * HEAD_DIM), row),
        pl.BlockSpec((tm, kvd), row),
        pl.BlockSpec((tm, kvd), row),
        pl.BlockSpec((tm, kvd), row),
        pl.BlockSpec((tm, kvd), row),
        pl.BlockSpec((IDX_HEADS, tm, 2 * LANES), lambda t: (0, t, 0)),
        pl.BlockSpec((tm, IDX_DIM), row),
        pl.BlockSpec((tm, 2 * LANES), row),
        pl.BlockSpec((tm, LANES), row),
        pl.BlockSpec((tm, gdim), row),
        pl.BlockSpec((tm, zdim), row),
    ]
    return pl.pallas_call(
        _inproj_kernel,
        out_shape=out_shape,
        grid=(nt,),
        in_specs=in_specs,
        out_specs=out_specs,
        compiler_params=_cparams(("arbitrary",)),
        name="in_proj",
    )(x3, sh, sc, g, wa, wih, wil, wg, qg, kg, *tabs)


def _sort_key(score):
    bits = pltpu.bitcast(score, I32)
    return jnp.where(bits < 0, bits ^ jnp.int32(0x7FFFFFFF), bits)


def _count(key_scr, nkc, kc, rows, n, preds):
    def body(c, accs):
        base = pl.multiple_of(c * kc, kc)
        accs = list(accs)
        for j in range(kc // LANES):
            sc = key_scr[:, pl.ds(base + j * LANES, LANES)]
            ps = preds(sc, base + j * LANES)
            for i in range(n):
                accs[i] = accs[i] + jnp.where(ps[i], 1.0, 0.0)
        return tuple(accs)

    accs = lax.fori_loop(0, nkc, body, tuple(jnp.zeros((rows, LANES), F32) for _ in range(n)))
    return [jnp.sum(a, axis=1, keepdims=True) for a in accs]


def _select_threshold(key_scr, nkc, kc, rows, n_sel, idx_bits, resolve_ties=True, bounds=None):
    n_sel_f = float(n_sel)

    def bit_step(b, acc):
        cand = acc | (jnp.int32(1) << (31 - b))
        cand_s = jnp.broadcast_to(cand ^ jnp.int32(INT_MIN), (rows, LANES))
        cnt, = _count(key_scr, nkc, kc, rows, 1, lambda kk, c0: (kk >= cand_s,))
        return jnp.where(cnt >= n_sel_f, cand, acc)

    first_bit, acc0 = 0, jnp.zeros((rows, 1), I32)
    if bounds is not None:
        u_lo, u_hi = bounds[0] ^ jnp.int32(INT_MIN), bounds[1] ^ jnp.int32(INT_MIN)
        shared = jnp.min(lax.clz(u_lo ^ u_hi).astype(F32))
        first_bit = jnp.minimum(shared.astype(I32), 31)
        acc0 = u_hi & jnp.where(first_bit > 0, jnp.int32(-1) << (32 - jnp.maximum(first_bit, 1)), 0)
    acc = lax.fori_loop(first_bit, 32, bit_step, acc0)
    thr = acc ^ jnp.int32(INT_MIN)
    thr_b = jnp.broadcast_to(thr, (rows, LANES))
    cnt_gt, cnt_eq = _count(key_scr, nkc, kc, rows, 2, lambda kk, c0: (kk > thr_b, kk == thr_b))
    need = n_sel_f - cnt_gt
    excess = (acc != 0) & (cnt_eq > need)
    any_excess = jnp.max(jnp.where(excess, 1.0, 0.0)) > 0.0
    lane = lax.broadcasted_iota(I32, (rows, LANES), 1)

    def resolve():
        def idx_step(b, p):
            cand = p | (jnp.int32(1) << (idx_bits - 1 - b))
            cand_b = jnp.broadcast_to(cand, (rows, LANES))
            cnt, = _count(key_scr, nkc, kc, rows, 1,
                          lambda kk, c0: ((kk == thr_b) & ((lane + c0) < cand_b),))
            return jnp.where(cnt < need, cand, p)

        cut = lax.fori_loop(0, idx_bits, idx_step, jnp.zeros((rows, 1), I32))
        cut_b = jnp.broadcast_to(cut, (rows, LANES))
        drop_row = jnp.broadcast_to(excess, (rows, LANES))

        def drop(c, carry):
            base = pl.multiple_of(c * kc, kc)
            for j in range(kc // LANES):
                sl = pl.ds(base + j * LANES, LANES)
                kk = key_scr[:, sl]
                kill = drop_row & (kk == thr_b) & ((lane + (base + j * LANES)) > cut_b)
                key_scr[:, sl] = jnp.where(kill, jnp.int32(INT_MIN), kk)
            return carry

        lax.fori_loop(0, nkc, drop, 0)

    if resolve_ties:
        pl.when(any_excess)(resolve)
    return jnp.maximum(thr, jnp.int32(INT_MIN + 1)), any_excess


LANE_TOP = 12
STREAMS = 2


def _dsa_prompt_kernel(qi3_ref, small_ref, q_ref, ki3_ref, k_ref, v_ref, o_ref, key_scr, w_scr, cand_scr,
                       ckey_scr, m_scr, acc_scr, *, tq, kc, n_sel, idx_bits):
    i = pl.program_id(1)
    nkc = (i * tq + tq + kc - 1) // kc
    q3 = qi3_ref[...].reshape(IDX_HEADS * tq, 2 * LANES)
    wi = small_ref[:, WI_LANE:WI_LANE + IDX_HEADS]
    for h in range(IDX_HEADS):
        w_scr[h] = jnp.broadcast_to(wi[:, h:h + 1], (tq, LANES))
    row = i * tq + lax.broadcasted_iota(I32, (tq, LANES), 0)
    lane = lax.broadcasted_iota(I32, (tq, LANES), 1)
    cand_chunks = LANE_TOP * LANES // kc
    reduce_keys = nkc > cand_chunks
    cand_scr[...] = jnp.full(cand_scr.shape, -jnp.inf, F32)

    def score_chunk(c, carry):
        base = pl.multiple_of(c * kc, kc)
        s = _dot_nt(q3, ki3_ref[pl.ds(base, kc), :])
        keys = []
        for j in range(kc // LANES):
            acc = None
            for h in range(IDX_HEADS):
                t = w_scr[h] * jnp.maximum(s[h * tq:(h + 1) * tq, j * LANES:(j + 1) * LANES], 0.0)
                acc = t if acc is None else acc + t
            valid = (lane + (base + j * LANES)) <= row
            key_scr[:, pl.ds(base + j * LANES, LANES)] = jnp.where(valid, _sort_key(acc), jnp.int32(INT_MIN))
            keys.append(jnp.where(valid, acc, -jnp.inf))
        for lvl in range(LANE_TOP):
            kept = cand_scr[:, lvl * LANES:(lvl + 1) * LANES]
            for j in range(len(keys)):
                kept, keys[j] = jnp.maximum(kept, keys[j]), jnp.minimum(kept, keys[j])
            cand_scr[:, lvl * LANES:(lvl + 1) * LANES] = kept
        return carry

    lax.fori_loop(0, nkc, score_chunk, 0)

    def from_candidates():
        for lvl in range(LANE_TOP):
            sc = cand_scr[:, lvl * LANES:(lvl + 1) * LANES]
            ckey_scr[:, lvl * LANES:(lvl + 1) * LANES] = jnp.where(sc > -jnp.inf, _sort_key(sc), jnp.int32(INT_MIN))
        deep = (n_sel + LANES - 1) // LANES - 1
        lo = _sort_key(jnp.min(cand_scr[:, deep * LANES:(deep + 1) * LANES], axis=1, keepdims=True))
        hi = _sort_key(jnp.max(cand_scr[:, 0:LANES], axis=1, keepdims=True))
        lo = jnp.maximum(lo, jnp.int32(INT_MIN + 1)) - 1
        hi = jnp.minimum(hi, jnp.int32(2147483646)) + 1
        thr_c, tied = _select_threshold(ckey_scr, cand_chunks, kc, tq, n_sel, idx_bits, resolve_ties=False,
                                        bounds=(lo, hi))
        last = ckey_scr[:, (LANE_TOP - 1) * LANES:LANE_TOP * LANES]
        dropped = jnp.max(jnp.where(last >= jnp.broadcast_to(thr_c, (tq, LANES)), 1.0, 0.0)) > 0.0
        return thr_c, jnp.logical_or(tied, dropped)

    thr_c, redo = lax.cond(reduce_keys, from_candidates,
                           lambda: (jnp.zeros((tq, 1), I32), jnp.bool_(True)))
    thr = lax.cond(redo, lambda: _select_threshold(key_scr, nkc, kc, tq, n_sel, idx_bits)[0], lambda: thr_c)
    thr_b = jnp.broadcast_to(thr, (tq, LANES))

    qs = []
    for g in range(KV_HEADS):
        qs.append(jnp.concatenate(
            [q_ref[:, (g * GROUP + a) * HEAD_DIM:(g * GROUP + a + 1) * HEAD_DIM] for a in range(GROUP)], axis=0))

    rq = GROUP * tq
    ones = jnp.ones((kc, HEAD_DIM), BF16)
    m_scr[...] = jnp.full(m_scr.shape, NEG, F32)
    acc_scr[...] = jnp.zeros(acc_scr.shape, F32)

    def attend_step(cc, carry):
        for st in range(STREAMS):
            c = cc * STREAMS + st
            live = c < nkc
            base = pl.multiple_of(jnp.minimum(c, nkc - 1) * kc, kc)
            biases = []
            for j in range(kc // LANES):
                sel = jnp.logical_and(key_scr[:, pl.ds(base + j * LANES, LANES)] >= thr_b, live)
                biases.append(jnp.where(sel, 0.0, NEG))
            bias = jnp.concatenate(biases, axis=1)
            bias = jnp.concatenate([bias] * GROUP, axis=0)
            for g in range(KV_HEADS):
                slot = st * KV_HEADS + g
                kg = k_ref[pl.ds(base, kc), g * HEAD_DIM:(g + 1) * HEAD_DIM]
                vg = jnp.concatenate([v_ref[pl.ds(base, kc), g * HEAD_DIM:(g + 1) * HEAD_DIM], ones], axis=1)
                s = _dot_nt(qs[g], kg) + bias
                m = m_scr[slot]
                m_new = jnp.maximum(m, jnp.broadcast_to(jnp.max(s, axis=1, keepdims=True), (rq, LANES)))
                alpha = jnp.exp(m - m_new)
                p = jnp.concatenate([jnp.exp(s[:, j * LANES:(j + 1) * LANES] - m_new)
                                     for j in range(kc // LANES)], axis=1)
                m_scr[slot] = m_new
                acc_scr[slot] = jnp.concatenate([alpha, alpha], axis=1) * acc_scr[slot] + _dot(p.astype(BF16), vg)
        return carry

    lax.fori_loop(0, (nkc + STREAMS - 1) // STREAMS, attend_step, 0)
    for g in range(KV_HEADS):
        m_all = m_scr[g]
        for st in range(1, STREAMS):
            m_all = jnp.maximum(m_all, m_scr[st * KV_HEADS + g])
        acc = None
        for st in range(STREAMS):
            w = jnp.exp(m_scr[st * KV_HEADS + g] - m_all)
            part = jnp.concatenate([w, w], axis=1) * acc_scr[st * KV_HEADS + g]
            acc = part if acc is None else acc + part
        o = acc[:, :HEAD_DIM] / acc[:, HEAD_DIM:]
        for a in range(GROUP):
            hh = g * GROUP + a
            o_ref[:, hh * HEAD_DIM:(hh + 1) * HEAD_DIM] = o[a * tq:(a + 1) * tq].astype(BF16)


def _dsa_prompt(qi3, small, q_bf, ki3, k_bf, v_bf, batch, seq):
    tq = LANES
    kc = min(512, seq)
    nq = seq // tq
    n_sel = min(TOPK_MAX, seq // 4)
    idx_bits = max(1, (seq - 1).bit_length())
    kvd = KV_HEADS * HEAD_DIM
    qrow = lambda b, i: (b * nq + i, 0)
    kern = functools.partial(_dsa_prompt_kernel, tq=tq, kc=kc, n_sel=n_sel, idx_bits=idx_bits)
    return pl.pallas_call(
        kern,
        out_shape=jax.ShapeDtypeStruct((batch * seq, ATTN_HEADS * HEAD_DIM), BF16),
        grid=(batch, nq),
        in_specs=[
            pl.BlockSpec((IDX_HEADS, tq, 2 * LANES), lambda b, i: (0, b * nq + i, 0)),
            pl.BlockSpec((tq, LANES), qrow),
            pl.BlockSpec((tq, ATTN_HEADS * HEAD_DIM), qrow),
            pl.BlockSpec((seq, 2 * LANES), lambda b, i: (b, 0)),
            pl.BlockSpec((seq, kvd), lambda b, i: (b, 0)),
            pl.BlockSpec((seq, kvd), lambda b, i: (b, 0)),
        ],
        out_specs=pl.BlockSpec((tq, ATTN_HEADS * HEAD_DIM), qrow),
        scratch_shapes=[pltpu.VMEM((tq, seq), I32), pltpu.VMEM((IDX_HEADS, tq, LANES), F32),
                        pltpu.VMEM((tq, LANE_TOP * LANES), F32),
                        pltpu.VMEM((tq, LANE_TOP * LANES), I32),
                        pltpu.VMEM((STREAMS * KV_HEADS, GROUP * tq, LANES), F32),
                        pltpu.VMEM((STREAMS * KV_HEADS, GROUP * tq, 2 * HEAD_DIM), F32)],
        compiler_params=_cparams(("arbitrary", "arbitrary")),
        name="dsa_prompt",
    )(qi3, small, q_bf, ki3, k_bf, v_bf)


SEQ_PER_STEP = 2


def _sample_scores_kernel(pt_ref, qi3_ref, small_ref, ki3n_ref, *rest, n_pages, t_len):
    pages = rest[:SEQ_PER_STEP * n_pages]
    s_o = rest[SEQ_PER_STEP * n_pages]
    nr = SEQ_PER_STEP * t_len
    q3 = qi3_ref[...].reshape(IDX_HEADS * nr, 2 * LANES)
    wi = small_ref[:, WI_LANE:WI_LANE + IDX_HEADS]
    knew = jnp.concatenate([ki3n_ref[...], jnp.zeros((PAGE - nr, 2 * LANES), BF16)], axis=0)
    zpad = jnp.zeros((2 * LANES - 3 * IDX_DIM, PAGE), F32)
    lane = lax.broadcasted_iota(I32, (t_len, LANES), 1)
    trow = lax.broadcasted_iota(I32, (t_len, LANES), 0)

    def head_sum(s, j):
        acc = None
        for h in range(IDX_HEADS):
            r0 = h * nr + j * t_len
            t = wi[j * t_len:(j + 1) * t_len, h:h + 1] * jnp.maximum(s[r0:r0 + t_len], 0.0)
            acc = t if acc is None else acc + t
        return acc

    for j in range(SEQ_PER_STEP):
        for p in range(n_pages):
            kp = pages[j * n_pages + p][...]
            hi = kp.astype(BF16).astype(F32)
            k3 = jnp.concatenate([hi, hi, kp - hi, zpad], axis=0).astype(BF16)
            s_o[j, :, p * PAGE:(p + 1) * PAGE] = head_sum(_dot(q3, k3), j)
        sn = head_sum(_dot_nt(q3, knew), j)
        ok = (lane >= j * t_len) & (lane - j * t_len <= trow)
        s_o[j, :, n_pages * PAGE:(n_pages + 1) * PAGE] = jnp.where(ok, sn, -jnp.inf)


def _sample_scores(page_table, qi3, small, ki3, cache_kidx_t, layer, rows_p, t_len):
    db, n_pages = page_table.shape
    nr = SEQ_PER_STEP * t_len
    base = rows_p // nr
    lp = (n_pages + 1) * PAGE
    in_specs = [
        pl.BlockSpec((IDX_HEADS, nr, 2 * LANES), lambda n, pt: (0, base + n, 0)),
        pl.BlockSpec((nr, LANES), lambda n, pt: (base + n, 0)),
        pl.BlockSpec((nr, 2 * LANES), lambda n, pt: (base + n, 0)),
    ]
    for j in range(SEQ_PER_STEP):
        for p in range(n_pages):
            in_specs.append(pl.BlockSpec(
                (None, None, IDX_DIM, PAGE),
                lambda n, pt, j=j, p=p: (pt[n * SEQ_PER_STEP + j, p], layer, 0, 0)))
    kern = functools.partial(_sample_scores_kernel, n_pages=n_pages, t_len=t_len)
    return pl.pallas_call(
        kern,
        out_shape=jax.ShapeDtypeStruct((db, t_len, lp), F32),
        grid_spec=pltpu.PrefetchScalarGridSpec(
            num_scalar_prefetch=1,
            grid=(db // SEQ_PER_STEP,),
            in_specs=in_specs,
            out_specs=pl.BlockSpec((SEQ_PER_STEP, t_len, lp), lambda n, pt: (n, 0, 0)),
        ),
        compiler_params=_cparams(("arbitrary",)),
        name="dsa_sample_scores",
    )(page_table, qi3, small, ki3, *([cache_kidx_t] * (SEQ_PER_STEP * n_pages)))


def _sample_select_kernel(s_ref, b_ref, key_scr, *, n_sel, idx_bits):
    rows, lp = s_ref.shape
    nkc = lp // LANES
    for c in range(nkc):
        sc = s_ref[:, c * LANES:(c + 1) * LANES]
        key_scr[:, c * LANES:(c + 1) * LANES] = jnp.where(sc > -jnp.inf, _sort_key(sc), jnp.int32(INT_MIN))
    thr, _ = _select_threshold(key_scr, nkc, LANES, rows, n_sel, idx_bits)
    thr_b = jnp.broadcast_to(thr, (rows, LANES))
    for c in range(nkc):
        b_ref[:, c * LANES:(c + 1) * LANES] = jnp.where(key_scr[:, c * LANES:(c + 1) * LANES] >= thr_b, 0.0, NEG)


def _sample_select(scores2d, n_sel):
    rows, lp = scores2d.shape
    tr = min(LANES, rows)
    kern = functools.partial(_sample_select_kernel, n_sel=n_sel, idx_bits=max(1, (lp - 1).bit_length()))
    return pl.pallas_call(
        kern,
        out_shape=jax.ShapeDtypeStruct((rows, lp), F32),
        grid=(rows // tr,),
        in_specs=[pl.BlockSpec((tr, lp), lambda r: (r, 0))],
        out_specs=pl.BlockSpec((tr, lp), lambda r: (r, 0)),
        scratch_shapes=[pltpu.VMEM((tr, lp), I32)],
        compiler_params=_cparams(("arbitrary",)),
        name="dsa_sample_select",
    )(scores2d)


def _sample_attend_kernel(pt_ref, q_ref, bias_ref, kn_ref, vn_ref, *rest, n_pages, t_len):
    npg = SEQ_PER_STEP * n_pages
    kpages, vpages = rest[:npg], rest[npg:2 * npg]
    o_ref = rest[2 * npg]
    kc_scr, vc_scr = rest[2 * npg + 1], rest[2 * npg + 2]
    nr = SEQ_PER_STEP * t_len
    lp = (n_pages + 1) * PAGE
    kvd = KV_HEADS * HEAD_DIM
    bias = jnp.concatenate([bias_ref[...]] * GROUP, axis=0)
    pad = jnp.zeros((PAGE - nr, kvd), BF16)
    kc_scr[n_pages * PAGE:lp, :] = jnp.concatenate([kn_ref[...], pad], axis=0)
    vc_scr[n_pages * PAGE:lp, :] = jnp.concatenate([vn_ref[...], pad], axis=0)
    for j in range(SEQ_PER_STEP):
        for p in range(n_pages):
            for g in range(KV_HEADS):
                head_rows = pl.ds(g, PAGE, stride=KV_HEADS)
                cols = slice(g * HEAD_DIM, (g + 1) * HEAD_DIM)
                kc_scr[p * PAGE:(p + 1) * PAGE, cols] = kpages[j * n_pages + p][head_rows, :].astype(BF16)
                vc_scr[p * PAGE:(p + 1) * PAGE, cols] = vpages[j * n_pages + p][head_rows, :].astype(BF16)
        for g in range(KV_HEADS):
            qs = jnp.concatenate(
                [q_ref[:, (g * GROUP + a) * HEAD_DIM:(g * GROUP + a + 1) * HEAD_DIM] for a in range(GROUP)],
                axis=0)
            s = _dot_nt(qs, kc_scr[:, g * HEAD_DIM:(g + 1) * HEAD_DIM]) + bias
            m = jnp.max(s, axis=1, keepdims=True)
            p_ = jnp.exp(s - m)
            l = jnp.sum(p_, axis=1, keepdims=True)
            o = _dot(p_.astype(BF16), vc_scr[:, g * HEAD_DIM:(g + 1) * HEAD_DIM]) / l
            for a in range(GROUP):
                hh = g * GROUP + a
                r0 = a * nr + j * t_len
                o_ref[j * t_len:(j + 1) * t_len, hh * HEAD_DIM:(hh + 1) * HEAD_DIM] = (
                    o[r0:r0 + t_len].astype(BF16))


def _sample_attend(page_table, q_bf, bias2d, k_bf, v_bf, cache_k2, cache_v2, depth, layer, rows_p, t_len):
    db, n_pages = page_table.shape
    nr = SEQ_PER_STEP * t_len
    base = rows_p // nr
    lp = (n_pages + 1) * PAGE
    kvd = KV_HEADS * HEAD_DIM
    in_specs = [
        pl.BlockSpec((nr, ATTN_HEADS * HEAD_DIM), lambda n, pt: (base + n, 0)),
        pl.BlockSpec((nr, lp), lambda n, pt: (n, 0)),
        pl.BlockSpec((nr, kvd), lambda n, pt: (base + n, 0)),
        pl.BlockSpec((nr, kvd), lambda n, pt: (base + n, 0)),
    ]
    for _ in range(2):
        for j in range(SEQ_PER_STEP):
            for p in range(n_pages):
                in_specs.append(pl.BlockSpec(
                    (PAGE * KV_HEADS, HEAD_DIM),
                    lambda n, pt, j=j, p=p: (pt[n * SEQ_PER_STEP + j, p] * depth + layer, 0)))
    kern = functools.partial(_sample_attend_kernel, n_pages=n_pages, t_len=t_len)
    npg = SEQ_PER_STEP * n_pages
    return pl.pallas_call(
        kern,
        out_shape=jax.ShapeDtypeStruct((db * t_len, ATTN_HEADS * HEAD_DIM), BF16),
        grid_spec=pltpu.PrefetchScalarGridSpec(
            num_scalar_prefetch=1,
            grid=(db // SEQ_PER_STEP,),
            in_specs=in_specs,
            out_specs=pl.BlockSpec((nr, ATTN_HEADS * HEAD_DIM), lambda n, pt: (n, 0)),
            scratch_shapes=[pltpu.VMEM((lp, kvd), BF16), pltpu.VMEM((lp, kvd), BF16)],
        ),
        compiler_params=_cparams(("arbitrary",)),
        name="dsa_sample_attend",
    )(page_table, q_bf, bias2d, k_bf, v_bf, *([cache_k2] * npg), *([cache_v2] * npg))


def _chunk_masks(n, chunk):
    ri = lax.broadcasted_iota(I32, (n, n), 0)
    ci = lax.broadcasted_iota(I32, (n, n), 1)

    def same(size):
        sh = size.bit_length() - 1
        return (ri >> sh) == (ci >> sh)

    same_c = same(chunk)
    incl = same_c & (ci <= ri)
    strict = same_c & (ci < ri)
    base = min(SUBLANES, chunk)
    levels = []
    s = base
    while s < chunk:
        levels.append(same(2 * s) & jnp.logical_not(same(s)))
        s *= 2
    return incl, strict, same(base), levels, (ri == ci)


def _unit_lower_inverse(a_heads, same_base, levels, eye):
    ident = jnp.where(eye, 1.0, 0.0)
    ad = [jnp.where(same_base, a, 0.0) for a in a_heads]
    a2 = [_mm3(x, x) for x in ad]
    a4 = [_mm3(x, x) for x in a2]
    t = [_mm3(ident - x, ident + y) for x, y in zip(ad, a2)]
    t = [_mm3(x, ident + y) for x, y in zip(t, a4)]
    for lv in levels:
        to = [_mm3(x, jnp.where(lv, a, 0.0)) for x, a in zip(t, a_heads)]
        t = [x - _mm3(y, x) for x, y in zip(t, to)]
    return t


def _mm1(a, b, dot=_dot):
    return dot(a.astype(BF16), b.astype(BF16))


def _gdn_intra(q, k, v, beta_c, gc_c, gc_r, masks):
    incl, strict, same_base, levels, eye = masks
    heads = range(len(q))
    decay = [jnp.exp(jnp.where(incl, gc_c[h] - gc_r[h], -jnp.inf)) for h in heads]
    kb = [k[h] * beta_c[h] for h in heads]
    a = [jnp.where(strict, _mm1(kb[h], k[h], _dot_nt) * decay[h], 0.0) for h in heads]
    t = _unit_lower_inverse(a, same_base, levels, eye)
    uw = [_mm3(t[h], jnp.concatenate([v[h] * beta_c[h], kb[h] * jnp.exp(gc_c[h])], axis=1)) for h in heads]
    qk = [_mm1(q[h], k[h], _dot_nt) * decay[h] for h in heads]
    return [x[:, :HEAD_DIM] for x in uw], [x[:, HEAD_DIM:] for x in uw], qk


def _l2(x):
    return x * lax.rsqrt(jnp.sum(x * x, axis=-1, keepdims=True) + EPS)


def _cum_matrices(n, chunk):
    ri = lax.broadcasted_iota(I32, (n, n), 0)
    ci = lax.broadcasted_iota(I32, (n, n), 1)
    sh = chunk.bit_length() - 1
    same = (ri >> sh) == (ci >> sh)
    lower = jnp.where(same & (ci <= ri), 1.0, 0.0).astype(BF16)
    upper = jnp.where(same & (ri <= ci), 1.0, 0.0).astype(BF16)
    return lower, upper


def _gdn_gates(sm, smt, alog_l, dtb_l, alog_c, dtb_c, chunk):
    n = sm.shape[0]
    lower, upper = _cum_matrices(n, chunk)
    g_tile = -jnp.exp(alog_l) * _softplus(sm + dtb_l)
    beta_tile = jax.nn.sigmoid(sm)
    gc_cols = _mm_exact_lhs(lower, g_tile)
    g_rows = -jnp.exp(alog_c) * _softplus(smt + dtb_c)
    gc_rows = _mm_exact_rhs(g_rows, upper)
    return beta_tile, gc_cols, gc_rows


def _gdn_prompt_kernel(gq_ref, gz_ref, sm_ref, smt_ref, cw_ref, alog_l, dtb_l, alog_c, dtb_c, ng_ref,
                       o_ref, s_o_ref, stage, s_scr, *, tt, chunk):
    t_idx = pl.program_id(1)
    hd = HEAD_DIM
    nh = GDN_HEADS

    @pl.when(t_idx == 0)
    def _():
        stage[0:SUBLANES, :] = jnp.zeros((SUBLANES, stage.shape[1]), F32)
        s_scr[...] = jnp.zeros(s_scr.shape, F32)

    x = gq_ref[...]
    stage[SUBLANES:SUBLANES + tt, :] = x
    y = None
    for j in range(CONV_W):
        term = stage[pl.ds(SUBLANES - (CONV_W - 1) + j, tt), :] * cw_ref[j:j + 1, :]
        y = term if y is None else y + term
    stage[0:SUBLANES, :] = x[tt - SUBLANES:tt, :]
    y = _silu(y)

    beta_tile, gc_cols, gc_rows = _gdn_gates(sm_ref[...], smt_ref[...], alog_l[...], dtb_l[...],
                                             alog_c[...], dtb_c[...], chunk)
    masks = _chunk_masks(tt, chunk)
    ng = ng_ref[...]
    heads = range(nh)
    q = [_l2(y[:, h * hd:(h + 1) * hd]) * (hd ** -0.5) for h in heads]
    k = [_l2(y[:, (nh + h) * hd:(nh + h + 1) * hd]) for h in heads]
    v = [y[:, (2 * nh + h) * hd:(2 * nh + h + 1) * hd] for h in heads]
    beta_c = [beta_tile[:, GB_LANE + h:GB_LANE + h + 1] for h in heads]
    gc_c = [gc_cols[:, GA_LANE + h:GA_LANE + h + 1] for h in heads]
    gc_r = [gc_rows[h:h + 1, :] for h in heads]
    u, w, qk = _gdn_intra(q, k, v, beta_c, gc_c, gc_r, masks)
    qg = [q[h] * jnp.exp(gc_c[h]) for h in heads]
    s = [s_scr[h] for h in heads]
    vnew = [[] for _ in heads]
    ointer = [[] for _ in heads]
    for c in range(tt // chunk):
        r = slice(c * chunk, (c + 1) * chunk)
        for h in heads:
            vn = u[h][r] - _mm1(w[h][r], s[h])
            ointer[h].append(_mm1(qg[h][r], s[h]))
            g_last = gc_c[h][(c + 1) * chunk - 1:(c + 1) * chunk, :]
            kdec = k[h][r] * jnp.exp(g_last - gc_c[h][r])
            s[h] = s[h] * jnp.exp(g_last) + _mm1(kdec, vn, _dot_tn)
            vnew[h].append(vn)
    for h in heads:
        s_scr[h] = s[h]
        o = jnp.concatenate(ointer[h], axis=0) + _mm1(qk[h], jnp.concatenate(vnew[h], axis=0))
        o = _rms(o) * ng * _silu(gz_ref[:, h * hd:(h + 1) * hd])
        o_ref[:, h * hd:(h + 1) * hd] = o.astype(BF16)

    @pl.when(t_idx == pl.num_programs(1) - 1)
    def _():
        s_o_ref[...] = s_scr[...]


def _gdn_prompt(gq, gz, small, small_t, conv_w_l, alog_l, dtb_l, alog_c, dtb_c, ng, batch, seq):
    tt = min(256, seq)
    chunk = min(GDN_CHUNK, seq)
    nt = seq // tt
    gdim = gq.shape[1]
    zdim = gz.shape[1]
    row = lambda b, t: (b * nt + t, 0)
    const = lambda b, t: (0, 0)
    kern = functools.partial(_gdn_prompt_kernel, tt=tt, chunk=chunk)
    return pl.pallas_call(
        kern,
        out_shape=[jax.ShapeDtypeStruct((batch * seq, zdim), BF16),
                   jax.ShapeDtypeStruct((batch, GDN_HEADS, HEAD_DIM, HEAD_DIM), F32)],
        grid=(batch, nt),
        in_specs=[
            pl.BlockSpec((tt, gdim), row),
            pl.BlockSpec((tt, zdim), row),
            pl.BlockSpec((tt, LANES), row),
            pl.BlockSpec((SUBLANES, tt), lambda b, t: (0, b * nt + t)),
            pl.BlockSpec((CONV_W, gdim), const),
            pl.BlockSpec((1, LANES), const),
            pl.BlockSpec((1, LANES), const),
            pl.BlockSpec((SUBLANES, 1), const),
            pl.BlockSpec((SUBLANES, 1), const),
            pl.BlockSpec((1, HEAD_DIM), const),
        ],
        out_specs=[pl.BlockSpec((tt, zdim), row),
                   pl.BlockSpec((None, GDN_HEADS, HEAD_DIM, HEAD_DIM), lambda b, t: (b, 0, 0, 0))],
        scratch_shapes=[pltpu.VMEM((tt + SUBLANES, gdim), F32),
                        pltpu.VMEM((GDN_HEADS, HEAD_DIM, HEAD_DIM), F32)],
        compiler_params=_cparams(("arbitrary", "arbitrary")),
        name="gdn_prompt",
    )(gq, gz, small, small_t, conv_w_l, alog_l, dtb_l, alog_c, dtb_c, ng)


def _gdn_sample_kernel(gq_ref, gz_ref, sm_ref, smt_ref, cst_ref, s0_ref, cw_ref, alog_l, dtb_l, alog_c,
                       dtb_c, ng_ref, o_ref, s_o_ref, stage, uw_scr, vn_scr, oi_scr, *, nb, t_len):
    hd = HEAD_DIM
    nh = GDN_HEADS
    n = nb * t_len
    gdim = gq_ref.shape[1]
    stage[:, 0:SUBLANES, :] = cst_ref[...].reshape(nb, SUBLANES, gdim)
    stage[:, SUBLANES:SUBLANES + t_len, :] = gq_ref[...].reshape(nb, t_len, gdim)
    y = None
    for j in range(CONV_W):
        term = stage[:, pl.ds(SUBLANES - (CONV_W - 1) + j, t_len), :] * cw_ref[j:j + 1, :]
        y = term if y is None else y + term
    y = _silu(y).reshape(n, gdim)

    beta_tile, gc_cols, gc_rows = _gdn_gates(sm_ref[...], smt_ref[...], alog_l[...], dtb_l[...],
                                             alog_c[...], dtb_c[...], t_len)
    masks = _chunk_masks(n, t_len)
    ng = ng_ref[...]
    ri = lax.broadcasted_iota(I32, (n, n), 0)
    ci = lax.broadcasted_iota(I32, (n, n), 1)
    sh = t_len.bit_length() - 1
    pick_last = jnp.where(((ri >> sh) == (ci >> sh)) & ((ci & (t_len - 1)) == t_len - 1), 1.0, 0.0).astype(BF16)
    g_last_cols = _mm_exact_lhs(pick_last, gc_cols)
    heads = range(nh)
    q = [_l2(y[:, h * hd:(h + 1) * hd]) * (hd ** -0.5) for h in heads]
    k = [_l2(y[:, (nh + h) * hd:(nh + h + 1) * hd]) for h in heads]
    v = [y[:, (2 * nh + h) * hd:(2 * nh + h + 1) * hd] for h in heads]
    beta_c = [beta_tile[:, GB_LANE + h:GB_LANE + h + 1] for h in heads]
    gc_c = [gc_cols[:, GA_LANE + h:GA_LANE + h + 1] for h in heads]
    gc_r = [gc_rows[h:h + 1, :] for h in heads]
    u, w, qk = _gdn_intra(q, k, v, beta_c, gc_c, gc_r, masks)
    for h in heads:
        g_last_c = g_last_cols[:, GA_LANE + h:GA_LANE + h + 1]
        uw_scr[h, 0] = u[h]
        uw_scr[h, 1] = w[h]
        uw_scr[h, 2] = q[h] * jnp.exp(gc_c[h])
        uw_scr[h, 3] = k[h] * jnp.exp(g_last_c - gc_c[h])
        uw_scr[h, 4] = jnp.broadcast_to(jnp.exp(g_last_c), (n, hd))

    def seq_step(i, carry):
        r0 = pl.multiple_of(i * t_len, t_len)
        rows = pl.ds(r0, t_len)
        for h in heads:
            s = s0_ref[i, h]
            vn = uw_scr[h, 0, rows, :] - _mm1(uw_scr[h, 1, rows, :], s)
            oi_scr[h, rows, :] = _mm1(uw_scr[h, 2, rows, :], s)
            vn_scr[h, rows, :] = vn
            dec = uw_scr[h, 4, pl.ds(r0, 1), :]
            s_o_ref[i, h] = s * dec + _mm1(uw_scr[h, 3, rows, :], vn, _dot_tn)
        return carry

    lax.fori_loop(0, nb, seq_step, 0)
    for h in heads:
        o = oi_scr[h] + _mm1(qk[h], vn_scr[h])
        o = _rms(o) * ng * _silu(gz_ref[:, h * hd:(h + 1) * hd])
        o_ref[:, h * hd:(h + 1) * hd] = o.astype(BF16)


def _gdn_sample(gq, gz, small, small_t, cstate, state_gdn, layer, conv_w_l, alog_l, dtb_l, alog_c, dtb_c, ng,
                rows_p, db, t_len):
    nb = min(16, db)
    n = nb * t_len
    base = rows_p // n
    gdim = gq.shape[1]
    zdim = gz.shape[1]
    row = lambda i: (base + i, 0)
    const = lambda i: (0, 0)
    kern = functools.partial(_gdn_sample_kernel, nb=nb, t_len=t_len)
    return pl.pallas_call(
        kern,
        out_shape=[jax.ShapeDtypeStruct((db * t_len, zdim), BF16),
                   jax.ShapeDtypeStruct((db, GDN_HEADS, HEAD_DIM, HEAD_DIM), F32)],
        grid=(db // nb,),
        in_specs=[
            pl.BlockSpec((n, gdim), row),
            pl.BlockSpec((n, zdim), row),
            pl.BlockSpec((n, LANES), row),
            pl.BlockSpec((SUBLANES, n), lambda i: (0, base + i)),
            pl.BlockSpec((n, gdim), lambda i: (i, 0)),
            pl.BlockSpec((nb, None, GDN_HEADS, HEAD_DIM, HEAD_DIM), lambda i: (i, layer, 0, 0, 0)),
            pl.BlockSpec((CONV_W, gdim), const),
            pl.BlockSpec((1, LANES), const),
            pl.BlockSpec((1, LANES), const),
            pl.BlockSpec((SUBLANES, 1), const),
            pl.BlockSpec((SUBLANES, 1), const),
            pl.BlockSpec((1, HEAD_DIM), const),
        ],
        out_specs=[pl.BlockSpec((n, zdim), lambda i: (i, 0)),
                   pl.BlockSpec((nb, GDN_HEADS, HEAD_DIM, HEAD_DIM), lambda i: (i, 0, 0, 0))],
        scratch_shapes=[pltpu.VMEM((nb, 2 * SUBLANES, gdim), F32),
                        pltpu.VMEM((GDN_HEADS, 5, n, HEAD_DIM), F32),
                        pltpu.VMEM((GDN_HEADS, n, HEAD_DIM), F32),
                        pltpu.VMEM((GDN_HEADS, n, HEAD_DIM), F32)],
        compiler_params=_cparams(("arbitrary",)),
        name="gdn_sample",
    )(gq, gz, small, small_t, cstate, state_gdn, conv_w_l, alog_l, dtb_l, alog_c, dtb_c, ng)


def _outproj_kernel(x_ref, att_ref, gdn_ref, gate_ref, sh_ref, sc_ref, g_ref, wo_ref, *rest, moe):
    if moe:
        rw_ref, rb_ref, x_o, h_o, lg_o = rest
    else:
        x_o, h_o = rest
    x = x_ref[...]
    adim = att_ref.shape[1]
    y = _dot(att_ref[...], wo_ref[0:adim, :]) + _dot(gdn_ref[...], wo_ref[adim:, :])
    xn = x + gate_ref[...] * y.reshape(x.shape)
    x_o[...] = xn
    h = (_rms(xn) * g_ref[...]) * (1.0 + sc_ref[...]) + sh_ref[...]
    h2 = h.reshape(y.shape)
    h_o[...] = h2.astype(BF16)
    if moe:
        lg_o[...] = _mm3(h2, rw_ref[...]) + rb_ref[...]


def _outproj(x3, att, gdn, gate, sh, sc, g, wo, router, tile_of, tm):
    ng, _, d = x3.shape
    rows = ng * SUBLANES
    gt = tm // SUBLANES
    moe = router is not None
    const = lambda t: (0, 0)
    row = lambda t: (t, 0)
    modspec = pl.BlockSpec((gt, 1, d), lambda t: (tile_of(t), 0, 0))
    in_specs = [
        pl.BlockSpec((gt, SUBLANES, d), lambda t: (t, 0, 0)),
        pl.BlockSpec((tm, att.shape[1]), row),
        pl.BlockSpec((tm, gdn.shape[1]), row),
        modspec, modspec, modspec,
        pl.BlockSpec((1, d), const),
        pl.BlockSpec(wo.shape, const),
    ]
    out_shape = [jax.ShapeDtypeStruct(x3.shape, F32), jax.ShapeDtypeStruct((rows, d), BF16)]
    out_specs = [pl.BlockSpec((gt, SUBLANES, d), lambda t: (t, 0, 0)), pl.BlockSpec((tm, d), row)]
    args = [x3, att, gdn, gate, sh, sc, g, wo]
    if moe:
        in_specs += [pl.BlockSpec((d, LANES), const), pl.BlockSpec((1, LANES), const)]
        out_shape.append(jax.ShapeDtypeStruct((rows, LANES), F32))
        out_specs.append(pl.BlockSpec((tm, LANES), row))
        args += list(router)
    return pl.pallas_call(
        functools.partial(_outproj_kernel, moe=moe),
        out_shape=out_shape,
        grid=(rows // tm,),
        in_specs=in_specs,
        out_specs=out_specs,
        compiler_params=_cparams(("arbitrary",)),
        name="out_proj",
    )(*args)


def _ffn_kernel(x_ref, h_ref, gate_ref, wg_ref, wu_ref, wd_ref, o_ref, *, fc):
    h = h_ref[...]
    dff = wg_ref.shape[1]
    acc = None
    for c in range(dff // fc):
        a = _dot(h, wg_ref[:, c * fc:(c + 1) * fc])
        u = _dot(h, wu_ref[:, c * fc:(c + 1) * fc])
        t = _dot((_silu(a) * u).astype(BF16), wd_ref[c * fc:(c + 1) * fc, :])
        acc = t if acc is None else acc + t
    x = x_ref[...]
    o_ref[...] = x + gate_ref[...] * acc.reshape(x.shape)


def _ffn(x3, hff, gate, wg, wu, wd, tile_of, tm):
    ng, _, d = x3.shape
    rows = ng * SUBLANES
    gt = tm // SUBLANES
    dff = wg.shape[1]
    fc = dff
    for cand in (1408, 1024, 768, 512, 256, 128):
        if dff % cand == 0:
            fc = cand
            break
    const = lambda t: (0, 0)
    return pl.pallas_call(
        functools.partial(_ffn_kernel, fc=fc),
        out_shape=jax.ShapeDtypeStruct(x3.shape, F32),
        grid=(rows // tm,),
        in_specs=[
            pl.BlockSpec((gt, SUBLANES, d), lambda t: (t, 0, 0)),
            pl.BlockSpec((tm, d), lambda t: (t, 0)),
            pl.BlockSpec((gt, 1, d), lambda t: (tile_of(t), 0, 0)),
            pl.BlockSpec(wg.shape, const),
            pl.BlockSpec(wu.shape, const),
            pl.BlockSpec(wd.shape, const),
        ],
        out_specs=pl.BlockSpec((gt, SUBLANES, d), lambda t: (t, 0, 0)),
        compiler_params=_cparams(("arbitrary",)),
        name="ffn_dense",
    )(x3, hff, gate, wg, wu, wd)


def _moe_kernel(x_ref, h_ref, lg_ref, gate_ref, wg_ref, wu_ref, wd_ref, o_ref, acc_scr, g_scr):
    e = pl.program_id(1)
    tm = h_ref.shape[0]

    @pl.when(e == 0)
    def _():
        lane = lax.broadcasted_iota(I32, (tm, LANES), 1)
        lg = jnp.where(lane < N_EXPERTS, lg_ref[...], -jnp.inf)
        m1 = jnp.max(lg, axis=1, keepdims=True)
        i1 = jnp.min(jnp.where(lg == m1, lane, LANES), axis=1, keepdims=True)
        rest = jnp.where(lane == i1, -jnp.inf, lg)
        m2 = jnp.max(rest, axis=1, keepdims=True)
        i2 = jnp.min(jnp.where(rest == m2, lane, LANES), axis=1, keepdims=True)
        e2 = jnp.exp(m2 - m1)
        den = 1.0 + e2
        g_scr[...] = jnp.where(lane == i1, 1.0 / den, 0.0) + jnp.where(lane == i2, e2 / den, 0.0)
        acc_scr[...] = jnp.zeros(acc_scr.shape, F32)

    h = h_ref[...]
    a = _dot(h, wg_ref[...])
    u = _dot(h, wu_ref[...])
    y = _dot((_silu(a) * u).astype(BF16), wd_ref[...])
    lane = lax.broadcasted_iota(I32, (tm, LANES), 1)
    ge = jnp.sum(jnp.where(lane == e, g_scr[...], 0.0), axis=1, keepdims=True)
    acc_scr[...] += ge * y

    @pl.when(e == pl.num_programs(1) - 1)
    def _():
        x = x_ref[...]
        o_ref[...] = x + gate_ref[...] * acc_scr[...].reshape(x.shape)


def _moe(x3, hff, logits, gate, wg, wu, wd, tile_of, tm):
    ng, _, d = x3.shape
    rows = ng * SUBLANES
    gt = tm // SUBLANES
    ne, _, eff = wg.shape
    return pl.pallas_call(
        _moe_kernel,
        out_shape=jax.ShapeDtypeStruct(x3.shape, F32),
        grid=(rows // tm, ne),
        in_specs=[
            pl.BlockSpec((gt, SUBLANES, d), lambda t, e: (t, 0, 0)),
            pl.BlockSpec((tm, d), lambda t, e: (t, 0)),
            pl.BlockSpec((tm, LANES), lambda t, e: (t, 0)),
            pl.BlockSpec((gt, 1, d), lambda t, e: (tile_of(t), 0, 0)),
            pl.BlockSpec((None, d, eff), lambda t, e: (e, 0, 0)),
            pl.BlockSpec((None, d, eff), lambda t, e: (e, 0, 0)),
            pl.BlockSpec((None, eff, d), lambda t, e: (e, 0, 0)),
        ],
        out_specs=pl.BlockSpec((gt, SUBLANES, d), lambda t, e: (t, 0, 0)),
        scratch_shapes=[pltpu.VMEM((tm, d), F32), pltpu.VMEM((tm, LANES), F32)],
        compiler_params=_cparams(("arbitrary", "arbitrary")),
        name="ffn_moe",
    )(x3, hff, logits, gate, wg, wu, wd)


def _rope_tables(pos):
    pos = pos.astype(F32)[:, None]
    half = HEAD_DIM // 2
    ang = pos * (ROPE_THETA ** (-jnp.arange(half, dtype=F32) / half))[None, :]
    c, s = jnp.cos(ang), jnp.sin(ang)
    cq = jnp.concatenate([c, c], axis=1)
    sq = jnp.concatenate([-s, s], axis=1)
    half = IDX_DIM // 2
    ang = pos * (ROPE_THETA ** (-jnp.arange(half, dtype=F32) / half))[None, :]
    c, s = jnp.cos(ang), jnp.sin(ang)
    z = jnp.zeros_like(s)
    ci = jnp.concatenate([c, c, c, c], axis=1)
    sa = jnp.concatenate([-s, z, -s, z], axis=1)
    sb = jnp.concatenate([z, s, z, s], axis=1)
    return cq, sq, ci, sa, sb


def _lane_vec(vals, lane0):
    return jnp.zeros((1, LANES), F32).at[0, lane0:lane0 + vals.shape[0]].set(vals)


def kernel(x_prompt, x_sample, cache_k, cache_v, cache_kidx, state_gdn, state_conv, page_table, c_prompt, c_sample, mod_w, mod_b, norm_mix_g, norm_ffn_g, w_in, q_norm_g, k_norm_g, conv_w, a_log, dt_bias, gdn_norm_g, w_out, ffn_w_gate, ffn_w_up, ffn_w_down, router_w, router_b, moe_w_gate, moe_w_up, moe_w_down):
    batch, seq, d = x_prompt.shape
    db, t_len, _ = x_sample.shape
    n_layers = mod_w.shape[0]
    n_pages = page_table.shape[1]
    past = n_pages * PAGE
    rows_p, rows_s = batch * seq, db * t_len
    rows = rows_p + rows_s
    assert t_len == SUBLANES and seq % LANES == 0
    tm = min(256, rows_s, seq)
    assert seq % tm == 0 and rows_s % tm == 0
    gt = tm // SUBLANES
    tiles_per_batch = seq // tm
    n_ptiles = rows_p // tm

    def tile_of(t):
        return jnp.where(t < n_ptiles, t // tiles_per_batch, batch + (t - n_ptiles))

    def tab_of(t):
        return jnp.where(t < n_ptiles, t % tiles_per_batch, tiles_per_batch)

    def groups(m):
        mp = jnp.repeat(m[:batch], gt, axis=0)
        return jnp.concatenate([mp, m[batch:batch + db]], axis=0)[:, None, :]

    x3 = jnp.concatenate([x_prompt.reshape(rows_p, d), x_sample.reshape(rows_s, d)], axis=0)
    x3 = x3.reshape(rows // SUBLANES, SUBLANES, d)

    n_c = batch + db
    c_all = jnp.concatenate([c_prompt, c_sample, jnp.zeros((-n_c % SUBLANES, d), F32)], axis=0)
    mods = _mods(c_all, mod_w, mod_b)

    pos = jnp.concatenate([jnp.arange(seq), past + (jnp.arange(tm) % t_len)])
    tabs = _rope_tables(pos)

    kvd = KV_HEADS * HEAD_DIM
    depth = cache_k.shape[1]
    cache_k2 = cache_k.reshape(-1, HEAD_DIM)
    cache_v2 = cache_v.reshape(-1, HEAD_DIM)
    cache_kidx_t = jnp.swapaxes(cache_kidx, 2, 3)
    n_sel_s = min(TOPK_MAX, (past + t_len) // 4)
    lp = (n_pages + 1) * PAGE

    outs = {name: [] for name in ("kp", "vp", "kip", "ks", "vs", "kis", "sp", "ss", "cp", "cs")}
    col = 0
    offs = []
    for size in (ATTN_HEADS * HEAD_DIM, kvd, kvd, IDX_HEADS * IDX_DIM, IDX_DIM, IDX_HEADS,
                 GDN_HEADS * 3 * HEAD_DIM, GDN_HEADS * HEAD_DIM, GDN_HEADS, GDN_HEADS):
        offs.append((col, col + size))
        col += size
    o_q, o_k, o_v, o_qi, o_ki, o_wi, o_gq, o_gz, o_ga, o_gb = offs

    for l in range(n_layers):
        m6 = [groups(mods[l, :, j * d:(j + 1) * d]) for j in range(6)]
        w = w_in[l]
        wa = w[:, o_q[0]:o_v[1]].astype(BF16)
        qi_w = w[:, o_qi[0]:o_qi[1]].reshape(d, IDX_HEADS, 1, IDX_DIM)
        qi_w = jnp.broadcast_to(qi_w, (d, IDX_HEADS, 2, IDX_DIM)).reshape(d, IDX_HEADS * LANES)
        ki_w = w[:, o_ki[0]:o_ki[1]]
        misc_w = jnp.concatenate([w[:, o_wi[0]:o_wi[1]], w[:, o_ga[0]:o_ga[1]], w[:, o_gb[0]:o_gb[1]],
                                  jnp.zeros((d, LANES - IDX_HEADS - 2 * GDN_HEADS), F32)], axis=1)
        wi_f = jnp.concatenate([qi_w, ki_w, ki_w, misc_w], axis=1)
        wih = wi_f.astype(BF16)
        wil = (wi_f - wih.astype(F32)).astype(BF16)
        wg = w[:, o_gq[0]:o_gz[1]].astype(BF16)

        (q_bf, k_f, v_f, k_bf, v_bf, qi3, ki_f, ki3, small, gq, gz) = _inproj(
            x3, m6[0], m6[1], norm_mix_g[l][None, :], wa, wih, wil, wg,
            q_norm_g[l][None, :], k_norm_g[l][None, :], tabs, tile_of, tab_of, tm)

        small_t = jnp.transpose(small[:, GA_LANE:GA_LANE + 2 * GDN_HEADS])
        alog_l = _lane_vec(a_log[l], GA_LANE)
        dtb_l = _lane_vec(dt_bias[l], GA_LANE)
        pad4 = jnp.zeros((SUBLANES - GDN_HEADS,), F32)
        alog_c = jnp.concatenate([a_log[l], pad4])[:, None]
        dtb_c = jnp.concatenate([dt_bias[l], pad4])[:, None]
        ng = gdn_norm_g[l][None, :]

        att_p = _dsa_prompt(qi3, small, q_bf, ki3, k_bf, v_bf, batch, seq)
        gdn_p, s_p = _gdn_prompt(gq, gz, small, small_t, conv_w[l], alog_l, dtb_l, alog_c, dtb_c, ng,
                                 batch, seq)

        scores = _sample_scores(page_table, qi3, small, ki3, cache_kidx_t, l, rows_p, t_len)
        bias = _sample_select(scores.reshape(rows_s, lp), n_sel_s)
        att_s = _sample_attend(page_table, q_bf, bias, k_bf, v_bf, cache_k2, cache_v2, depth, l, rows_p, t_len)
        cstate = jnp.pad(state_conv[:, l], ((0, 0), (SUBLANES - (CONV_W - 1), 0), (0, 0)))
        cstate = cstate.reshape(db * SUBLANES, cstate.shape[2])
        gdn_s, s_s = _gdn_sample(gq, gz, small, small_t, cstate, state_gdn, l, conv_w[l], alog_l, dtb_l,
                                 alog_c, dtb_c, ng, rows_p, db, t_len)

        att = jnp.concatenate([att_p, att_s], axis=0)
        gdn = jnp.concatenate([gdn_p, gdn_s], axis=0)
        i = l // 2
        router = None
        if l % 2 == 1:
            rw = jnp.concatenate([router_w[i], jnp.zeros((d, LANES - N_EXPERTS), F32)], axis=1)
            rb = jnp.concatenate([router_b[i], jnp.zeros((LANES - N_EXPERTS,), F32)])[None, :]
            router = (rw, rb)
        res = _outproj(x3, att, gdn, m6[2], m6[3], m6[4], norm_ffn_g[l][None, :], w_out[l].astype(BF16),
                       router, tile_of, tm)
        if l % 2 == 0:
            x3, hff = res
            x3 = _ffn(x3, hff, m6[5], ffn_w_gate[i].astype(BF16), ffn_w_up[i].astype(BF16),
                      ffn_w_down[i].astype(BF16), tile_of, tm)
        else:
            x3, hff, logits = res
            x3 = _moe(x3, hff, logits, m6[5], moe_w_gate[i].astype(BF16), moe_w_up[i].astype(BF16),
                      moe_w_down[i].astype(BF16), tile_of, tm)

        outs["kp"].append(k_f[:rows_p].reshape(batch, seq, KV_HEADS, HEAD_DIM))
        outs["vp"].append(v_f[:rows_p].reshape(batch, seq, KV_HEADS, HEAD_DIM))
        outs["kip"].append(ki_f[:rows_p].reshape(batch, seq, IDX_DIM))
        outs["ks"].append(k_f[rows_p:].reshape(db, t_len, KV_HEADS, HEAD_DIM))
        outs["vs"].append(v_f[rows_p:].reshape(db, t_len, KV_HEADS, HEAD_DIM))
        outs["kis"].append(ki_f[rows_p:].reshape(db, t_len, IDX_DIM))
        outs["sp"].append(s_p)
        outs["ss"].append(s_s)
        tail = CONV_W - 1
        outs["cp"].append(jnp.stack([gq[(b + 1) * seq - tail:(b + 1) * seq] for b in range(batch)], axis=0))
        gq_s = gq[rows_p:].reshape(db, t_len, -1)
        outs["cs"].append(gq_s[:, t_len - tail:])

    x2 = x3.reshape(rows, d)
    st = lambda name: jnp.stack(outs[name], axis=1)
    return (x2[:rows_p].reshape(batch, seq, d), x2[rows_p:].reshape(db, t_len, d),
            st("kp"), st("vp"), st("kip"), st("ks"), st("vs"), st("kis"),
            st("sp"), st("ss"), st("cp"), st("cs"))
```

```python
import functools
import math

import jax
import jax.numpy as jnp
from jax import lax
from jax.experimental import pallas as pl
from jax.experimental.pallas import tpu as pltpu

F32 = jnp.float32
BF16 = jnp.bfloat16
I32 = jnp.int32

HEAD_DIM = 128
ATTN_HEADS = 4
KV_HEADS = 2
GROUP = ATTN_HEADS // KV_HEADS
IDX_HEADS = 8
IDX_DIM = 64
TOPK_MAX = 256
GDN_HEADS = 4
GDN_CHUNK = 64
CONV_W = 4
N_EXPERTS = 8
PAGE = 128
ROPE_THETA = 10000.0
EPS = 1e-6

LANES = 128
SUBLANES = 8
VMEM_LIMIT = 56 * 1024 * 1024
NEG = -1e30
INT_MIN = -2147483648
WI_LANE = 0
GA_LANE = 8
GB_LANE = 12


def _cparams(sem):
    return pltpu.CompilerParams(dimension_semantics=sem, vmem_limit_bytes=VMEM_LIMIT)


def _dot(a, b):
    return jnp.dot(a, b, preferred_element_type=F32)


def _dot_nt(a, b):
    return lax.dot_general(a, b, (((1,), (1,)), ((), ())), preferred_element_type=F32)


def _dot_tn(a, b):
    return lax.dot_general(a, b, (((0,), (0,)), ((), ())), preferred_element_type=F32)


def _split(x):
    hi = x.astype(BF16)
    lo = (x - hi.astype(F32)).astype(BF16)
    return hi, lo


def _mm3(a, b, dot=_dot):
    ah, al = _split(a)
    bh, bl = _split(b)
    return dot(ah, bh) + (dot(ah, bl) + dot(al, bh))


def _split_three(a):
    a1 = a.astype(BF16)
    r1 = a - a1.astype(F32)
    a2 = r1.astype(BF16)
    a3 = (r1 - a2.astype(F32)).astype(BF16)
    return a1, a2, a3


def _mm_exact_rhs(a, b_bf16):
    a1, a2, a3 = _split_three(a)
    return _dot(a1, b_bf16) + (_dot(a2, b_bf16) + _dot(a3, b_bf16))


def _mm_exact_lhs(m_bf16, a):
    a1, a2, a3 = _split_three(a)
    return _dot(m_bf16, a1) + (_dot(m_bf16, a2) + _dot(m_bf16, a3))


def _silu(x):
    return x * jax.nn.sigmoid(x)


def _softplus(x):
    return jnp.maximum(x, 0.0) + jnp.log1p(jnp.exp(-jnp.abs(x)))


def _mods_kernel(c_ref, w_ref, b_ref, o_ref):
    o_ref[...] = _mm3(_silu(c_ref[...]), w_ref[...]) + b_ref[...]


def _mods(c_all, mod_w, mod_b):
    n_layers, d, n6 = mod_w.shape
    rows = c_all.shape[0]
    tn = n6 // 4
    return pl.pallas_call(
        _mods_kernel,
        out_shape=jax.ShapeDtypeStruct((n_layers, rows, n6), F32),
        grid=(n_layers, n6 // tn),
        in_specs=[
            pl.BlockSpec((rows, d), lambda l, j: (0, 0)),
            pl.BlockSpec((None, d, tn), lambda l, j: (l, 0, j)),
            pl.BlockSpec((None, 1, tn), lambda l, j: (l, 0, j)),
        ],
        out_specs=pl.BlockSpec((None, rows, tn), lambda l, j: (l, 0, j)),
        compiler_params=_cparams(("arbitrary", "arbitrary")),
        name="ada_mods",
    )(c_all, mod_w, mod_b.reshape(n_layers, 1, n6))


def _rms(x):
    return x * lax.rsqrt(jnp.mean(x * x, axis=-1, keepdims=True) + EPS)


def _inproj_kernel(x_ref, sh_ref, sc_ref, g_ref, wa_ref, wih_ref, wil_ref, wg_ref, qg_ref, kg_ref,
                   cq_ref, sq_ref, ci_ref, sa_ref, sb_ref,
                   q_o, k_o, v_o, kb_o, vb_o, qi3_o, ki_o, ki3_o, small_o, gq_o, gz_o):
    x = x_ref[...]
    h = (_rms(x) * g_ref[...]) * (1.0 + sc_ref[...]) + sh_ref[...]
    tm = x.shape[0] * x.shape[1]
    h2 = h.reshape(tm, x.shape[2])
    hb, hl = _split(h2)
    za = _dot(hb, wa_ref[...])
    wih = wih_ref[...]
    zi = _dot(hb, wih) + (_dot(hb, wil_ref[...]) + _dot(hl, wih))
    zg = _dot(hb, wg_ref[...])

    cq, sq = cq_ref[...], sq_ref[...]
    qg, kg = qg_ref[...], kg_ref[...]
    for hh in range(ATTN_HEADS):
        qn = _rms(za[:, hh * HEAD_DIM:(hh + 1) * HEAD_DIM]) * qg
        qr = qn * cq + pltpu.roll(qn, HEAD_DIM // 2, 1) * sq
        q_o[:, hh * HEAD_DIM:(hh + 1) * HEAD_DIM] = (qr * (HEAD_DIM ** -0.5)).astype(BF16)
    koff = ATTN_HEADS * HEAD_DIM
    for hh in range(KV_HEADS):
        kn = _rms(za[:, koff + hh * HEAD_DIM:koff + (hh + 1) * HEAD_DIM]) * kg
        kr = kn * cq + pltpu.roll(kn, HEAD_DIM // 2, 1) * sq
        k_o[:, hh * HEAD_DIM:(hh + 1) * HEAD_DIM] = kr
        kb_o[:, hh * HEAD_DIM:(hh + 1) * HEAD_DIM] = kr.astype(BF16)
    voff = koff + KV_HEADS * HEAD_DIM
    v = za[:, voff:voff + KV_HEADS * HEAD_DIM]
    v_o[...] = v
    vb_o[...] = v.astype(BF16)

    ci, sa, sb = ci_ref[...], sa_ref[...], sb_ref[...]
    first = lax.broadcasted_iota(I32, (tm, LANES), 1) < IDX_DIM

    def rope64(t):
        return t * ci + pltpu.roll(t, LANES - IDX_DIM // 2, 1) * sa + pltpu.roll(t, IDX_DIM // 2, 1) * sb

    for hh in range(IDX_HEADS):
        r = rope64(zi[:, hh * LANES:(hh + 1) * LANES]) * (IDX_DIM ** -0.5)
        hi = r.astype(BF16).astype(F32)
        qi3_o[hh, :, 0:LANES] = jnp.where(first, hi, r - hi).astype(BF16)
        qi3_o[hh, :, LANES:2 * LANES] = jnp.where(first, hi, 0.0).astype(BF16)
    r = rope64(zi[:, IDX_HEADS * LANES:(IDX_HEADS + 1) * LANES])
    ki_o[...] = r[:, :IDX_DIM]
    hi = r.astype(BF16).astype(F32)
    ki3_o[:, 0:LANES] = hi.astype(BF16)
    ki3_o[:, LANES:2 * LANES] = jnp.where(first, r - hi, 0.0).astype(BF16)
    misc = zi[:, (IDX_HEADS + 1) * LANES:(IDX_HEADS + 2) * LANES]
    lane = lax.broadcasted_iota(I32, (tm, LANES), 1)
    small_o[...] = jnp.where(lane < IDX_HEADS, misc * (IDX_HEADS ** -0.5), misc)
    gdim = gq_o.shape[1]
    gq_o[...] = zg[:, :gdim]
    gz_o[...] = zg[:, gdim:]


def _inproj(x3, sh, sc, g, wa, wih, wil, wg, qg, kg, tabs, tile_of, tab_of, tm):
    ng, _, d = x3.shape
    rows = ng * SUBLANES
    gt = tm // SUBLANES
    nt = rows // tm
    gdim = GDN_HEADS * 3 * HEAD_DIM
    zdim = GDN_HEADS * HEAD_DIM
    const = lambda t: (0, 0)
    row = lambda t: (t, 0)
    tab = lambda t: (tab_of(t), 0)
    in_specs = [
        pl.BlockSpec((gt, SUBLANES, d), lambda t: (t, 0, 0)),
        pl.BlockSpec((gt, 1, d), lambda t: (tile_of(t), 0, 0)),
        pl.BlockSpec((gt, 1, d), lambda t: (tile_of(t), 0, 0)),
        pl.BlockSpec((1, d), const),
        pl.BlockSpec(wa.shape, const),
        pl.BlockSpec(wih.shape, const),
        pl.BlockSpec(wil.shape, const),
        pl.BlockSpec(wg.shape, const),
        pl.BlockSpec((1, HEAD_DIM), const),
        pl.BlockSpec((1, HEAD_DIM), const),
    ] + [pl.BlockSpec((tm, LANES), tab)] * 5
    kvd = KV_HEADS * HEAD_DIM
    out_shape = [
        jax.ShapeDtypeStruct((rows, ATTN_HEADS * HEAD_DIM), BF16),
        jax.ShapeDtypeStruct((rows, kvd), F32),
        jax.ShapeDtypeStruct((rows, kvd), F32),
        jax.ShapeDtypeStruct((rows, kvd), BF16),
        jax.ShapeDtypeStruct((rows, kvd), BF16),
        jax.ShapeDtypeStruct((IDX_HEADS, rows, 2 * LANES), BF16),
        jax.ShapeDtypeStruct((rows, IDX_DIM), F32),
        jax.ShapeDtypeStruct((rows, 2 * LANES), BF16),
        jax.ShapeDtypeStruct((rows, LANES), F32),
        jax.ShapeDtypeStruct((rows, gdim), F32),
        jax.ShapeDtypeStruct((rows, zdim), F32),
    ]
    out_specs = [
        pl.BlockSpec((tm, ATTN_HEADS * HEAD_DIM), row),
        pl.BlockSpec((tm, kvd), row),
        pl.BlockSpec((tm, kvd), row),
        pl.BlockSpec((tm, kvd), row),
        pl.BlockSpec((tm, kvd), row),
        pl.BlockSpec((IDX_HEADS, tm, 2 * LANES), lambda t: (0, t, 0)),
        pl.BlockSpec((tm, IDX_DIM), row),
        pl.BlockSpec((tm, 2 * LANES), row),
        pl.BlockSpec((tm, LANES), row),
        pl.BlockSpec((tm, gdim), row),
        pl.BlockSpec((tm, zdim), row),
    ]
    return pl.pallas_call(
        _inproj_kernel,
        out_shape=out_shape,
        grid=(nt,),
        in_specs=in_specs,
        out_specs=out_specs,
        compiler_params=_cparams(("arbitrary",)),
        name="in_proj",
    )(x3, sh, sc, g, wa, wih, wil, wg, qg, kg, *tabs)


def _sort_key(score):
    bits = pltpu.bitcast(score, I32)
    return jnp.where(bits < 0, bits ^ jnp.int32(0x7FFFFFFF), bits)


def _count(key_scr, nkc, kc, rows, n, preds):
    def body(c, accs):
        base = pl.multiple_of(c * kc, kc)
        accs = list(accs)
        for j in range(kc // LANES):
            sc = key_scr[:, pl.ds(base + j * LANES, LANES)]
            ps = preds(sc, base + j * LANES)
            for i in range(n):
                accs[i] = accs[i] + jnp.where(ps[i], 1.0, 0.0)
        return tuple(accs)

    accs = lax.fori_loop(0, nkc, body, tuple(jnp.zeros((rows, LANES), F32) for _ in range(n)))
    return [jnp.sum(a, axis=1, keepdims=True) for a in accs]


def _select_threshold(key_scr, nkc, kc, rows, n_sel, idx_bits, resolve_ties=True, bounds=None):
    n_sel_f = float(n_sel)

    def bit_step(b, acc):
        cand = acc | (jnp.int32(1) << (31 - b))
        cand_s = jnp.broadcast_to(cand ^ jnp.int32(INT_MIN), (rows, LANES))
        cnt, = _count(key_scr, nkc, kc, rows, 1, lambda kk, c0: (kk >= cand_s,))
        return jnp.where(cnt >= n_sel_f, cand, acc)

    first_bit, acc0 = 0, jnp.zeros((rows, 1), I32)
    if bounds is not None:
        u_lo, u_hi = bounds[0] ^ jnp.int32(INT_MIN), bounds[1] ^ jnp.int32(INT_MIN)
        shared = jnp.min(lax.clz(u_lo ^ u_hi).astype(F32))
        first_bit = jnp.minimum(shared.astype(I32), 31)
        acc0 = u_hi & jnp.where(first_bit > 0, jnp.int32(-1) << (32 - jnp.maximum(first_bit, 1)), 0)
    acc = lax.fori_loop(first_bit, 32, bit_step, acc0)
    thr = acc ^ jnp.int32(INT_MIN)
    thr_b = jnp.broadcast_to(thr, (rows, LANES))
    cnt_gt, cnt_eq = _count(key_scr, nkc, kc, rows, 2, lambda kk, c0: (kk > thr_b, kk == thr_b))
    need = n_sel_f - cnt_gt
    excess = (acc != 0) & (cnt_eq > need)
    any_excess = jnp.max(jnp.where(excess, 1.0, 0.0)) > 0.0
    lane = lax.broadcasted_iota(I32, (rows, LANES), 1)

    def resolve():
        def idx_step(b, p):
            cand = p | (jnp.int32(1) << (idx_bits - 1 - b))
            cand_b = jnp.broadcast_to(cand, (rows, LANES))
            cnt, = _count(key_scr, nkc, kc, rows, 1,
                          lambda kk, c0: ((kk == thr_b) & ((lane + c0) < cand_b),))
            return jnp.where(cnt < need, cand, p)

        cut = lax.fori_loop(0, idx_bits, idx_step, jnp.zeros((rows, 1), I32))
        cut_b = jnp.broadcast_to(cut, (rows, LANES))
        drop_row = jnp.broadcast_to(excess, (rows, LANES))

        def drop(c, carry):
            base = pl.multiple_of(c * kc, kc)
            for j in range(kc // LANES):
                sl = pl.ds(base + j * LANES, LANES)
                kk = key_scr[:, sl]
                kill = drop_row & (kk == thr_b) & ((lane + (base + j * LANES)) > cut_b)
                key_scr[:, sl] = jnp.where(kill, jnp.int32(INT_MIN), kk)
            return carry

        lax.fori_loop(0, nkc, drop, 0)

    if resolve_ties:
        pl.when(any_excess)(resolve)
    return jnp.maximum(thr, jnp.int32(INT_MIN + 1)), any_excess


LANE_TOP = 12
STREAMS = 2


def _dsa_prompt_kernel(qi3_ref, small_ref, q_ref, ki3_ref, k_ref, v_ref, o_ref, key_scr, w_scr, cand_scr,
                       ckey_scr, m_scr, acc_scr, *, tq, kc, n_sel, idx_bits):
    i = pl.program_id(1)
    nkc = (i * tq + tq + kc - 1) // kc
    q3 = qi3_ref[...].reshape(IDX_HEADS * tq, 2 * LANES)
    wi = small_ref[:, WI_LANE:WI_LANE + IDX_HEADS]
    for h in range(IDX_HEADS):
        w_scr[h] = jnp.broadcast_to(wi[:, h:h + 1], (tq, LANES))
    row = i * tq + lax.broadcasted_iota(I32, (tq, LANES), 0)
    lane = lax.broadcasted_iota(I32, (tq, LANES), 1)
    cand_chunks = LANE_TOP * LANES // kc
    reduce_keys = nkc > cand_chunks
    cand_scr[...] = jnp.full(cand_scr.shape, -jnp.inf, F32)

    def score_chunk(c, carry):
        base = pl.multiple_of(c * kc, kc)
        s = _dot_nt(q3, ki3_ref[pl.ds(base, kc), :])
        keys = []
        for j in range(kc // LANES):
            acc = None
            for h in range(IDX_HEADS):
                t = w_scr[h] * jnp.maximum(s[h * tq:(h + 1) * tq, j * LANES:(j + 1) * LANES], 0.0)
                acc = t if acc is None else acc + t
            valid = (lane + (base + j * LANES)) <= row
            key_scr[:, pl.ds(base + j * LANES, LANES)] = jnp.where(valid, _sort_key(acc), jnp.int32(INT_MIN))
            keys.append(jnp.where(valid, acc, -jnp.inf))
        for lvl in range(LANE_TOP):
            kept = cand_scr[:, lvl * LANES:(lvl + 1) * LANES]
            for j in range(len(keys)):
                kept, keys[j] = jnp.maximum(kept, keys[j]), jnp.minimum(kept, keys[j])
            cand_scr[:, lvl * LANES:(lvl + 1) * LANES] = kept
        return carry

    lax.fori_loop(0, nkc, score_chunk, 0)

    def from_candidates():
        for lvl in range(LANE_TOP):
            sc = cand_scr[:, lvl * LANES:(lvl + 1) * LANES]
            ckey_scr[:, lvl * LANES:(lvl + 1) * LANES] = jnp.where(sc > -jnp.inf, _sort_key(sc), jnp.int32(INT_MIN))
        deep = (n_sel + LANES - 1) // LANES - 1
        lo = _sort_key(jnp.min(cand_scr[:, deep * LANES:(deep + 1) * LANES], axis=1, keepdims=True))
        hi = _sort_key(jnp.max(cand_scr[:, 0:LANES], axis=1, keepdims=True))
        lo = jnp.maximum(lo, jnp.int32(INT_MIN + 1)) - 1
        hi = jnp.minimum(hi, jnp.int32(2147483646)) + 1
        thr_c, tied = _select_threshold(ckey_scr, cand_chunks, kc, tq, n_sel, idx_bits, resolve_ties=False,
                                        bounds=(lo, hi))
        last = ckey_scr[:, (LANE_TOP - 1) * LANES:LANE_TOP * LANES]
        dropped = jnp.max(jnp.where(last >= jnp.broadcast_to(thr_c, (tq, LANES)), 1.0, 0.0)) > 0.0
        return thr_c, jnp.logical_or(tied, dropped)

    thr_c, redo = lax.cond(reduce_keys, from_candidates,
                           lambda: (jnp.zeros((tq, 1), I32), jnp.bool_(True)))
    thr = lax.cond(redo, lambda: _select_threshold(key_scr, nkc, kc, tq, n_sel, idx_bits)[0], lambda: thr_c)
    thr_b = jnp.broadcast_to(thr, (tq, LANES))

    qs = []
    for g in range(KV_HEADS):
        qs.append(jnp.concatenate(
            [q_ref[:, (g * GROUP + a) * HEAD_DIM:(g * GROUP + a + 1) * HEAD_DIM] for a in range(GROUP)], axis=0))

    rq = GROUP * tq
    ones = jnp.ones((kc, HEAD_DIM), BF16)
    m_scr[...] = jnp.full(m_scr.shape, NEG, F32)
    acc_scr[...] = jnp.zeros(acc_scr.shape, F32)

    def attend_step(cc, carry):
        for st in range(STREAMS):
            c = cc * STREAMS + st
            live = c < nkc
            base = pl.multiple_of(jnp.minimum(c, nkc - 1) * kc, kc)
            biases = []
            for j in range(kc // LANES):
                sel = jnp.logical_and(key_scr[:, pl.ds(base + j * LANES, LANES)] >= thr_b, live)
                biases.append(jnp.where(sel, 0.0, NEG))
            bias = jnp.concatenate(biases, axis=1)
            bias = jnp.concatenate([bias] * GROUP, axis=0)
            for g in range(KV_HEADS):
                slot = st * KV_HEADS + g
                kg = k_ref[pl.ds(base, kc), g * HEAD_DIM:(g + 1) * HEAD_DIM]
                vg = jnp.concatenate([v_ref[pl.ds(base, kc), g * HEAD_DIM:(g + 1) * HEAD_DIM], ones], axis=1)
                s = _dot_nt(qs[g], kg) + bias
                m = m_scr[slot]
                m_new = jnp.maximum(m, jnp.broadcast_to(jnp.max(s, axis=1, keepdims=True), (rq, LANES)))
                alpha = jnp.exp(m - m_new)
                p = jnp.concatenate([jnp.exp(s[:, j * LANES:(j + 1) * LANES] - m_new)
                                     for j in range(kc // LANES)], axis=1)
                m_scr[slot] = m_new
                acc_scr[slot] = jnp.concatenate([alpha, alpha], axis=1) * acc_scr[slot] + _dot(p.astype(BF16), vg)
        return carry

    lax.fori_loop(0, (nkc + STREAMS - 1) // STREAMS, attend_step, 0)
    for g in range(KV_HEADS):
        m_all = m_scr[g]
        for st in range(1, STREAMS):
            m_all = jnp.maximum(m_all, m_scr[st * KV_HEADS + g])
        acc = None
        for st in range(STREAMS):
            w = jnp.exp(m_scr[st * KV_HEADS + g] - m_all)
            part = jnp.concatenate([w, w], axis=1) * acc_scr[st * KV_HEADS + g]
            acc = part if acc is None else acc + part
        o = acc[:, :HEAD_DIM] / acc[:, HEAD_DIM:]
        for a in range(GROUP):
            hh = g * GROUP + a
            o_ref[:, hh * HEAD_DIM:(hh + 1) * HEAD_DIM] = o[a * tq:(a + 1) * tq].astype(BF16)


def _dsa_prompt(qi3, small, q_bf, ki3, k_bf, v_bf, batch, seq):
    tq = LANES
    kc = min(512, seq)
    nq = seq // tq
    n_sel = min(TOPK_MAX, seq // 4)
    idx_bits = max(1, (seq - 1).bit_length())
    kvd = KV_HEADS * HEAD_DIM
    qrow = lambda b, i: (b * nq + i, 0)
    kern = functools.partial(_dsa_prompt_kernel, tq=tq, kc=kc, n_sel=n_sel, idx_bits=idx_bits)
    return pl.pallas_call(
        kern,
        out_shape=jax.ShapeDtypeStruct((batch * seq, ATTN_HEADS * HEAD_DIM), BF16),
        grid=(batch, nq),
        in_specs=[
            pl.BlockSpec((IDX_HEADS, tq, 2 * LANES), lambda b, i: (0, b * nq + i, 0)),
            pl.BlockSpec((tq, LANES), qrow),
            pl.BlockSpec((tq, ATTN_HEADS * HEAD_DIM), qrow),
            pl.BlockSpec((seq, 2 * LANES), lambda b, i: (b, 0)),
            pl.BlockSpec((seq, kvd), lambda b, i: (b, 0)),
            pl.BlockSpec((seq, kvd), lambda b, i: (b, 0)),
        ],
        out_specs=pl.BlockSpec((tq, ATTN_HEADS * HEAD_DIM), qrow),
        scratch_shapes=[pltpu.VMEM((tq, seq), I32), pltpu.VMEM((IDX_HEADS, tq, LANES), F32),
                        pltpu.VMEM((tq, LANE_TOP * LANES), F32),
                        pltpu.VMEM((tq, LANE_TOP * LANES), I32),
                        pltpu.VMEM((STREAMS * KV_HEADS, GROUP * tq, LANES), F32),
                        pltpu.VMEM((STREAMS * KV_HEADS, GROUP * tq, 2 * HEAD_DIM), F32)],
        compiler_params=_cparams(("arbitrary", "arbitrary")),
        name="dsa_prompt",
    )(qi3, small, q_bf, ki3, k_bf, v_bf)


SEQ_PER_STEP = 2


def _sample_scores_kernel(pt_ref, qi3_ref, small_ref, ki3n_ref, *rest, n_pages, t_len):
    pages = rest[:SEQ_PER_STEP * n_pages]
    s_o = rest[SEQ_PER_STEP * n_pages]
    nr = SEQ_PER_STEP * t_len
    q3 = qi3_ref[...].reshape(IDX_HEADS * nr, 2 * LANES)
    wi = small_ref[:, WI_LANE:WI_LANE + IDX_HEADS]
    knew = jnp.concatenate([ki3n_ref[...], jnp.zeros((PAGE - nr, 2 * LANES), BF16)], axis=0)
    zpad = jnp.zeros((2 * LANES - 3 * IDX_DIM, PAGE), F32)
    lane = lax.broadcasted_iota(I32, (t_len, LANES), 1)
    trow = lax.broadcasted_iota(I32, (t_len, LANES), 0)

    def head_sum(s, j):
        acc = None
        for h in range(IDX_HEADS):
            r0 = h * nr + j * t_len
            t = wi[j * t_len:(j + 1) * t_len, h:h + 1] * jnp.maximum(s[r0:r0 + t_len], 0.0)
            acc = t if acc is None else acc + t
        return acc

    for j in range(SEQ_PER_STEP):
        for p in range(n_pages):
            kp = pages[j * n_pages + p][...]
            hi = kp.astype(BF16).astype(F32)
            k3 = jnp.concatenate([hi, hi, kp - hi, zpad], axis=0).astype(BF16)
            s_o[j, :, p * PAGE:(p + 1) * PAGE] = head_sum(_dot(q3, k3), j)
        sn = head_sum(_dot_nt(q3, knew), j)
        ok = (lane >= j * t_len) & (lane - j * t_len <= trow)
        s_o[j, :, n_pages * PAGE:(n_pages + 1) * PAGE] = jnp.where(ok, sn, -jnp.inf)


def _sample_scores(page_table, qi3, small, ki3, cache_kidx_t, layer, rows_p, t_len):
    db, n_pages = page_table.shape
    nr = SEQ_PER_STEP * t_len
    base = rows_p // nr
    lp = (n_pages + 1) * PAGE
    in_specs = [
        pl.BlockSpec((IDX_HEADS, nr, 2 * LANES), lambda n, pt: (0, base + n, 0)),
        pl.BlockSpec((nr, LANES), lambda n, pt: (base + n, 0)),
        pl.BlockSpec((nr, 2 * LANES), lambda n, pt: (base + n, 0)),
    ]
    for j in range(SEQ_PER_STEP):
        for p in range(n_pages):
            in_specs.append(pl.BlockSpec(
                (None, None, IDX_DIM, PAGE),
                lambda n, pt, j=j, p=p: (pt[n * SEQ_PER_STEP + j, p], layer, 0, 0)))
    kern = functools.partial(_sample_scores_kernel, n_pages=n_pages, t_len=t_len)
    return pl.pallas_call(
        kern,
        out_shape=jax.ShapeDtypeStruct((db, t_len, lp), F32),
        grid_spec=pltpu.PrefetchScalarGridSpec(
            num_scalar_prefetch=1,
            grid=(db // SEQ_PER_STEP,),
            in_specs=in_specs,
            out_specs=pl.BlockSpec((SEQ_PER_STEP, t_len, lp), lambda n, pt: (n, 0, 0)),
        ),
        compiler_params=_cparams(("arbitrary",)),
        name="dsa_sample_scores",
    )(page_table, qi3, small, ki3, *([cache_kidx_t] * (SEQ_PER_STEP * n_pages)))


def _sample_select_kernel(s_ref, b_ref, key_scr, *, n_sel, idx_bits):
    rows, lp = s_ref.shape
    nkc = lp // LANES
    for c in range(nkc):
        sc = s_ref[:, c * LANES:(c + 1) * LANES]
        key_scr[:, c * LANES:(c + 1) * LANES] = jnp.where(sc > -jnp.inf, _sort_key(sc), jnp.int32(INT_MIN))
    thr, _ = _select_threshold(key_scr, nkc, LANES, rows, n_sel, idx_bits)
    thr_b = jnp.broadcast_to(thr, (rows, LANES))
    for c in range(nkc):
        b_ref[:, c * LANES:(c + 1) * LANES] = jnp.where(key_scr[:, c * LANES:(c + 1) * LANES] >= thr_b, 0.0, NEG)


def _sample_select(scores2d, n_sel):
    rows, lp = scores2d.shape
    tr = min(LANES, rows)
    kern = functools.partial(_sample_select_kernel, n_sel=n_sel, idx_bits=max(1, (lp - 1).bit_length()))
    return pl.pallas_call(
        kern,
        out_shape=jax.ShapeDtypeStruct((rows, lp), F32),
        grid=(rows // tr,),
        in_specs=[pl.BlockSpec((tr, lp), lambda r: (r, 0))],
        out_specs=pl.BlockSpec((tr, lp), lambda r: (r, 0)),
        scratch_shapes=[pltpu.VMEM((tr, lp), I32)],
        compiler_params=_cparams(("arbitrary",)),
        name="dsa_sample_select",
    )(scores2d)


def _sample_attend_kernel(pt_ref, q_ref, bias_ref, kn_ref, vn_ref, *rest, n_pages, t_len):
    npg = SEQ_PER_STEP * n_pages
    kpages, vpages = rest[:npg], rest[npg:2 * npg]
    o_ref = rest[2 * npg]
    kc_scr, vc_scr = rest[2 * npg + 1], rest[2 * npg + 2]
    nr = SEQ_PER_STEP * t_len
    lp = (n_pages + 1) * PAGE
    kvd = KV_HEADS * HEAD_DIM
    bias = jnp.concatenate([bias_ref[...]] * GROUP, axis=0)
    pad = jnp.zeros((PAGE - nr, kvd), BF16)
    kc_scr[n_pages * PAGE:lp, :] = jnp.concatenate([kn_ref[...], pad], axis=0)
    vc_scr[n_pages * PAGE:lp, :] = jnp.concatenate([vn_ref[...], pad], axis=0)
    for j in range(SEQ_PER_STEP):
        for p in range(n_pages):
            for g in range(KV_HEADS):
                head_rows = pl.ds(g, PAGE, stride=KV_HEADS)
                cols = slice(g * HEAD_DIM, (g + 1) * HEAD_DIM)
                kc_scr[p * PAGE:(p + 1) * PAGE, cols] = kpages[j * n_pages + p][head_rows, :].astype(BF16)
                vc_scr[p * PAGE:(p + 1) * PAGE, cols] = vpages[j * n_pages + p][head_rows, :].astype(BF16)
        for g in range(KV_HEADS):
            qs = jnp.concatenate(
                [q_ref[:, (g * GROUP + a) * HEAD_DIM:(g * GROUP + a + 1) * HEAD_DIM] for a in range(GROUP)],
                axis=0)
            s = _dot_nt(qs, kc_scr[:, g * HEAD_DIM:(g + 1) * HEAD_DIM]) + bias
            m = jnp.max(s, axis=1, keepdims=True)
            p_ = jnp.exp(s - m)
            l = jnp.sum(p_, axis=1, keepdims=True)
            o = _dot(p_.astype(BF16), vc_scr[:, g * HEAD_DIM:(g + 1) * HEAD_DIM]) / l
            for a in range(GROUP):
                hh = g * GROUP + a
                r0 = a * nr + j * t_len
                o_ref[j * t_len:(j + 1) * t_len, hh * HEAD_DIM:(hh + 1) * HEAD_DIM] = (
                    o[r0:r0 + t_len].astype(BF16))


def _sample_attend(page_table, q_bf, bias2d, k_bf, v_bf, cache_k2, cache_v2, depth, layer, rows_p, t_len):
    db, n_pages = page_table.shape
    nr = SEQ_PER_STEP * t_len
    base = rows_p // nr
    lp = (n_pages + 1) * PAGE
    kvd = KV_HEADS * HEAD_DIM
    in_specs = [
        pl.BlockSpec((nr, ATTN_HEADS * HEAD_DIM), lambda n, pt: (base + n, 0)),
        pl.BlockSpec((nr, lp), lambda n, pt: (n, 0)),
        pl.BlockSpec((nr, kvd), lambda n, pt: (base + n, 0)),
        pl.BlockSpec((nr, kvd), lambda n, pt: (base + n, 0)),
    ]
    for _ in range(2):
        for j in range(SEQ_PER_STEP):
            for p in range(n_pages):
                in_specs.append(pl.BlockSpec(
                    (PAGE * KV_HEADS, HEAD_DIM),
                    lambda n, pt, j=j, p=p: (pt[n * SEQ_PER_STEP + j, p] * depth + layer, 0)))
    kern = functools.partial(_sample_attend_kernel, n_pages=n_pages, t_len=t_len)
    npg = SEQ_PER_STEP * n_pages
    return pl.pallas_call(
        kern,
        out_shape=jax.ShapeDtypeStruct((db * t_len, ATTN_HEADS * HEAD_DIM), BF16),
        grid_spec=pltpu.PrefetchScalarGridSpec(
            num_scalar_prefetch=1,
            grid=(db // SEQ_PER_STEP,),
            in_specs=in_specs,
            out_specs=pl.BlockSpec((nr, ATTN_HEADS * HEAD_DIM), lambda n, pt: (n, 0)),
            scratch_shapes=[pltpu.VMEM((lp, kvd), BF16), pltpu.VMEM((lp, kvd), BF16)],
        ),
        compiler_params=_cparams(("arbitrary",)),
        name="dsa_sample_attend",
    )(page_table, q_bf, bias2d, k_bf, v_bf, *([cache_k2] * npg), *([cache_v2] * npg))


def _chunk_masks(n, chunk):
    ri = lax.broadcasted_iota(I32, (n, n), 0)
    ci = lax.broadcasted_iota(I32, (n, n), 1)

    def same(size):
        sh = size.bit_length() - 1
        return (ri >> sh) == (ci >> sh)

    same_c = same(chunk)
    incl = same_c & (ci <= ri)
    strict = same_c & (ci < ri)
    base = min(SUBLANES, chunk)
    levels = []
    s = base
    while s < chunk:
        levels.append(same(2 * s) & jnp.logical_not(same(s)))
        s *= 2
    return incl, strict, same(base), levels, (ri == ci)


def _unit_lower_inverse(a_heads, same_base, levels, eye):
    ident = jnp.where(eye, 1.0, 0.0)
    ad = [jnp.where(same_base, a, 0.0) for a in a_heads]
    a2 = [_mm3(x, x) for x in ad]
    a4 = [_mm3(x, x) for x in a2]
    t = [_mm3(ident - x, ident + y) for x, y in zip(ad, a2)]
    t = [_mm3(x, ident + y) for x, y in zip(t, a4)]
    for lv in levels:
        to = [_mm3(x, jnp.where(lv, a, 0.0)) for x, a in zip(t, a_heads)]
        t = [x - _mm3(y, x) for x, y in zip(t, to)]
    return t


def _mm1(a, b, dot=_dot):
    return dot(a.astype(BF16), b.astype(BF16))


def _gdn_intra(q, k, v, beta_c, gc_c, gc_r, masks):
    incl, strict, same_base, levels, eye = masks
    heads = range(len(q))
    decay = [jnp.exp(jnp.where(incl, gc_c[h] - gc_r[h], -jnp.inf)) for h in heads]
    kb = [k[h] * beta_c[h] for h in heads]
    a = [jnp.where(strict, _mm1(kb[h], k[h], _dot_nt) * decay[h], 0.0) for h in heads]
    t = _unit_lower_inverse(a, same_base, levels, eye)
    uw = [_mm3(t[h], jnp.concatenate([v[h] * beta_c[h], kb[h] * jnp.exp(gc_c[h])], axis=1)) for h in heads]
    qk = [_mm1(q[h], k[h], _dot_nt) * decay[h] for h in heads]
    return [x[:, :HEAD_DIM] for x in uw], [x[:, HEAD_DIM:] for x in uw], qk


def _l2(x):
    return x * lax.rsqrt(jnp.sum(x * x, axis=-1, keepdims=True) + EPS)


def _cum_matrices(n, chunk):
    ri = lax.broadcasted_iota(I32, (n, n), 0)
    ci = lax.broadcasted_iota(I32, (n, n), 1)
    sh = chunk.bit_length() - 1
    same = (ri >> sh) == (ci >> sh)
    lower = jnp.where(same & (ci <= ri), 1.0, 0.0).astype(BF16)
    upper = jnp.where(same & (ri <= ci), 1.0, 0.0).astype(BF16)
    return lower, upper


def _gdn_gates(sm, smt, alog_l, dtb_l, alog_c, dtb_c, chunk):
    n = sm.shape[0]
    lower, upper = _cum_matrices(n, chunk)
    g_tile = -jnp.exp(alog_l) * _softplus(sm + dtb_l)
    beta_tile = jax.nn.sigmoid(sm)
    gc_cols = _mm_exact_lhs(lower, g_tile)
    g_rows = -jnp.exp(alog_c) * _softplus(smt + dtb_c)
    gc_rows = _mm_exact_rhs(g_rows, upper)
    return beta_tile, gc_cols, gc_rows


def _gdn_prompt_kernel(gq_ref, gz_ref, sm_ref, smt_ref, cw_ref, alog_l, dtb_l, alog_c, dtb_c, ng_ref,
                       o_ref, s_o_ref, stage, s_scr, *, tt, chunk):
    t_idx = pl.program_id(1)
    hd = HEAD_DIM
    nh = GDN_HEADS

    @pl.when(t_idx == 0)
    def _():
        stage[0:SUBLANES, :] = jnp.zeros((SUBLANES, stage.shape[1]), F32)
        s_scr[...] = jnp.zeros(s_scr.shape, F32)

    x = gq_ref[...]
    stage[SUBLANES:SUBLANES + tt, :] = x
    y = None
    for j in range(CONV_W):
        term = stage[pl.ds(SUBLANES - (CONV_W - 1) + j, tt), :] * cw_ref[j:j + 1, :]
        y = term if y is None else y + term
    stage[0:SUBLANES, :] = x[tt - SUBLANES:tt, :]
    y = _silu(y)

    beta_tile, gc_cols, gc_rows = _gdn_gates(sm_ref[...], smt_ref[...], alog_l[...], dtb_l[...],
                                             alog_c[...], dtb_c[...], chunk)
    masks = _chunk_masks(tt, chunk)
    ng = ng_ref[...]
    heads = range(nh)
    q = [_l2(y[:, h * hd:(h + 1) * hd]) * (hd ** -0.5) for h in heads]
    k = [_l2(y[:, (nh + h) * hd:(nh + h + 1) * hd]) for h in heads]
    v = [y[:, (2 * nh + h) * hd:(2 * nh + h + 1) * hd] for h in heads]
    beta_c = [beta_tile[:, GB_LANE + h:GB_LANE + h + 1] for h in heads]
    gc_c = [gc_cols[:, GA_LANE + h:GA_LANE + h + 1] for h in heads]
    gc_r = [gc_rows[h:h + 1, :] for h in heads]
    u, w, qk = _gdn_intra(q, k, v, beta_c, gc_c, gc_r, masks)
    qg = [q[h] * jnp.exp(gc_c[h]) for h in heads]
    s = [s_scr[h] for h in heads]
    vnew = [[] for _ in heads]
    ointer = [[] for _ in heads]
    for c in range(tt // chunk):
        r = slice(c * chunk, (c + 1) * chunk)
        for h in heads:
            vn = u[h][r] - _mm1(w[h][r], s[h])
            ointer[h].append(_mm1(qg[h][r], s[h]))
            g_last = gc_c[h][(c + 1) * chunk - 1:(c + 1) * chunk, :]
            kdec = k[h][r] * jnp.exp(g_last - gc_c[h][r])
            s[h] = s[h] * jnp.exp(g_last) + _mm1(kdec, vn, _dot_tn)
            vnew[h].append(vn)
    for h in heads:
        s_scr[h] = s[h]
        o = jnp.concatenate(ointer[h], axis=0) + _mm1(qk[h], jnp.concatenate(vnew[h], axis=0))
        o = _rms(o) * ng * _silu(gz_ref[:, h * hd:(h + 1) * hd])
        o_ref[:, h * hd:(h + 1) * hd] = o.astype(BF16)

    @pl.when(t_idx == pl.num_programs(1) - 1)
    def _():
        s_o_ref[...] = s_scr[...]


def _gdn_prompt(gq, gz, small, small_t, conv_w_l, alog_l, dtb_l, alog_c, dtb_c, ng, batch, seq):
    tt = min(256, seq)
    chunk = min(GDN_CHUNK, seq)
    nt = seq // tt
    gdim = gq.shape[1]
    zdim = gz.shape[1]
    row = lambda b, t: (b * nt + t, 0)
    const = lambda b, t: (0, 0)
    kern = functools.partial(_gdn_prompt_kernel, tt=tt, chunk=chunk)
    return pl.pallas_call(
        kern,
        out_shape=[jax.ShapeDtypeStruct((batch * seq, zdim), BF16),
                   jax.ShapeDtypeStruct((batch, GDN_HEADS, HEAD_DIM, HEAD_DIM), F32)],
        grid=(batch, nt),
        in_specs=[
            pl.BlockSpec((tt, gdim), row),
            pl.BlockSpec((tt, zdim), row),
            pl.BlockSpec((tt, LANES), row),
            pl.BlockSpec((SUBLANES, tt), lambda b, t: (0, b * nt + t)),
            pl.BlockSpec((CONV_W, gdim), const),
            pl.BlockSpec((1, LANES), const),
            pl.BlockSpec((1, LANES), const),
            pl.BlockSpec((SUBLANES, 1), const),
            pl.BlockSpec((SUBLANES, 1), const),
            pl.BlockSpec((1, HEAD_DIM), const),
        ],
        out_specs=[pl.BlockSpec((tt, zdim), row),
                   pl.BlockSpec((None, GDN_HEADS, HEAD_DIM, HEAD_DIM), lambda b, t: (b, 0, 0, 0))],
        scratch_shapes=[pltpu.VMEM((tt + SUBLANES, gdim), F32),
                        pltpu.VMEM((GDN_HEADS, HEAD_DIM, HEAD_DIM), F32)],
        compiler_params=_cparams(("arbitrary", "arbitrary")),
        name="gdn_prompt",
    )(gq, gz, small, small_t, conv_w_l, alog_l, dtb_l, alog_c, dtb_c, ng)


def _gdn_sample_kernel(gq_ref, gz_ref, sm_ref, smt_ref, cst_ref, s0_ref, cw_ref, alog_l, dtb_l, alog_c,
                       dtb_c, ng_ref, o_ref, s_o_ref, stage, uw_scr, vn_scr, oi_scr, *, nb, t_len):
    hd = HEAD_DIM
    nh = GDN_HEADS
    n = nb * t_len
    gdim = gq_ref.shape[1]
    stage[:, 0:SUBLANES, :] = cst_ref[...].reshape(nb, SUBLANES, gdim)
    stage[:, SUBLANES:SUBLANES + t_len, :] = gq_ref[...].reshape(nb, t_len, gdim)
    y = None
    for j in range(CONV_W):
        term = stage[:, pl.ds(SUBLANES - (CONV_W - 1) + j, t_len), :] * cw_ref[j:j + 1, :]
        y = term if y is None else y + term
    y = _silu(y).reshape(n, gdim)

    beta_tile, gc_cols, gc_rows = _gdn_gates(sm_ref[...], smt_ref[...], alog_l[...], dtb_l[...],
                                             alog_c[...], dtb_c[...], t_len)
    masks = _chunk_masks(n, t_len)
    ng = ng_ref[...]
    ri = lax.broadcasted_iota(I32, (n, n), 0)
    ci = lax.broadcasted_iota(I32, (n, n), 1)
    sh = t_len.bit_length() - 1
    pick_last = jnp.where(((ri >> sh) == (ci >> sh)) & ((ci & (t_len - 1)) == t_len - 1), 1.0, 0.0).astype(BF16)
    g_last_cols = _mm_exact_lhs(pick_last, gc_cols)
    heads = range(nh)
    q = [_l2(y[:, h * hd:(h + 1) * hd]) * (hd ** -0.5) for h in heads]
    k = [_l2(y[:, (nh + h) * hd:(nh + h + 1) * hd]) for h in heads]
    v = [y[:, (2 * nh + h) * hd:(2 * nh + h + 1) * hd] for h in heads]
    beta_c = [beta_tile[:, GB_LANE + h:GB_LANE + h + 1] for h in heads]
    gc_c = [gc_cols[:, GA_LANE + h:GA_LANE + h + 1] for h in heads]
    gc_r = [gc_rows[h:h + 1, :] for h in heads]
    u, w, qk = _gdn_intra(q, k, v, beta_c, gc_c, gc_r, masks)
    for h in heads:
        g_last_c = g_last_cols[:, GA_LANE + h:GA_LANE + h + 1]
        uw_scr[h, 0] = u[h]
        uw_scr[h, 1] = w[h]
        uw_scr[h, 2] = q[h] * jnp.exp(gc_c[h])
        uw_scr[h, 3] = k[h] * jnp.exp(g_last_c - gc_c[h])
        uw_scr[h, 4] = jnp.broadcast_to(jnp.exp(g_last_c), (n, hd))

    def seq_step(i, carry):
        r0 = pl.multiple_of(i * t_len, t_len)
        rows = pl.ds(r0, t_len)
        for h in heads:
            s = s0_ref[i, h]
            vn = uw_scr[h, 0, rows, :] - _mm1(uw_scr[h, 1, rows, :], s)
            oi_scr[h, rows, :] = _mm1(uw_scr[h, 2, rows, :], s)
            vn_scr[h, rows, :] = vn
            dec = uw_scr[h, 4, pl.ds(r0, 1), :]
            s_o_ref[i, h] = s * dec + _mm1(uw_scr[h, 3, rows, :], vn, _dot_tn)
        return carry

    lax.fori_loop(0, nb, seq_step, 0)
    for h in heads:
        o = oi_scr[h] + _mm1(qk[h], vn_scr[h])
        o = _rms(o) * ng * _silu(gz_ref[:, h * hd:(h + 1) * hd])
        o_ref[:, h * hd:(h + 1) * hd] = o.astype(BF16)


def _gdn_sample(gq, gz, small, small_t, cstate, state_gdn, layer, conv_w_l, alog_l, dtb_l, alog_c, dtb_c, ng,
                rows_p, db, t_len):
    nb = min(16, db)
    n = nb * t_len
    base = rows_p // n
    gdim = gq.shape[1]
    zdim = gz.shape[1]
    row = lambda i: (base + i, 0)
    const = lambda i: (0, 0)
    kern = functools.partial(_gdn_sample_kernel, nb=nb, t_len=t_len)
    return pl.pallas_call(
        kern,
        out_shape=[jax.ShapeDtypeStruct((db * t_len, zdim), BF16),
                   jax.ShapeDtypeStruct((db, GDN_HEADS, HEAD_DIM, HEAD_DIM), F32)],
        grid=(db // nb,),
        in_specs=[
            pl.BlockSpec((n, gdim), row),
            pl.BlockSpec((n, zdim), row),
            pl.BlockSpec((n, LANES), row),
            pl.BlockSpec((SUBLANES, n), lambda i: (0, base + i)),
            pl.BlockSpec((n, gdim), lambda i: (i, 0)),
            pl.BlockSpec((nb, None, GDN_HEADS, HEAD_DIM, HEAD_DIM), lambda i: (i, layer, 0, 0, 0)),
            pl.BlockSpec((CONV_W, gdim), const),
            pl.BlockSpec((1, LANES), const),
            pl.BlockSpec((1, LANES), const),
            pl.BlockSpec((SUBLANES, 1), const),
            pl.BlockSpec((SUBLANES, 1), const),
            pl.BlockSpec((1, HEAD_DIM), const),
        ],
        out_specs=[pl.BlockSpec((n, zdim), lambda i: (i, 0)),
                   pl.BlockSpec((nb, GDN_HEADS, HEAD_DIM, HEAD_DIM), lambda i: (i, 0, 0, 0))],
        scratch_shapes=[pltpu.VMEM((nb, 2 * SUBLANES, gdim), F32),
                        pltpu.VMEM((GDN_HEADS, 5, n, HEAD_DIM), F32),
                        pltpu.VMEM((GDN_HEADS, n, HEAD_DIM), F32),
                        pltpu.VMEM((GDN_HEADS, n, HEAD_DIM), F32)],
        compiler_params=_cparams(("arbitrary",)),
        name="gdn_sample",
    )(gq, gz, small, small_t, cstate, state_gdn, conv_w_l, alog_l, dtb_l, alog_c, dtb_c, ng)


def _outproj_kernel(x_ref, att_ref, gdn_ref, gate_ref, sh_ref, sc_ref, g_ref, wo_ref, *rest, moe):
    if moe:
        rw_ref, rb_ref, x_o, h_o, lg_o = rest
    else:
        x_o, h_o = rest
    x = x_ref[...]
    adim = att_ref.shape[1]
    y = _dot(att_ref[...], wo_ref[0:adim, :]) + _dot(gdn_ref[...], wo_ref[adim:, :])
    xn = x + gate_ref[...] * y.reshape(x.shape)
    x_o[...] = xn
    h = (_rms(xn) * g_ref[...]) * (1.0 + sc_ref[...]) + sh_ref[...]
    h2 = h.reshape(y.shape)
    h_o[...] = h2.astype(BF16)
    if moe:
        lg_o[...] = _mm3(h2, rw_ref[...]) + rb_ref[...]


def _outproj(x3, att, gdn, gate, sh, sc, g, wo, router, tile_of, tm):
    ng, _, d = x3.shape
    rows = ng * SUBLANES
    gt = tm // SUBLANES
    moe = router is not None
    const = lambda t: (0, 0)
    row = lambda t: (t, 0)
    modspec = pl.BlockSpec((gt, 1, d), lambda t: (tile_of(t), 0, 0))
    in_specs = [
        pl.BlockSpec((gt, SUBLANES, d), lambda t: (t, 0, 0)),
        pl.BlockSpec((tm, att.shape[1]), row),
        pl.BlockSpec((tm, gdn.shape[1]), row),
        modspec, modspec, modspec,
        pl.BlockSpec((1, d), const),
        pl.BlockSpec(wo.shape, const),
    ]
    out_shape = [jax.ShapeDtypeStruct(x3.shape, F32), jax.ShapeDtypeStruct((rows, d), BF16)]
    out_specs = [pl.BlockSpec((gt, SUBLANES, d), lambda t: (t, 0, 0)), pl.BlockSpec((tm, d), row)]
    args = [x3, att, gdn, gate, sh, sc, g, wo]
    if moe:
        in_specs += [pl.BlockSpec((d, LANES), const), pl.BlockSpec((1, LANES), const)]
        out_shape.append(jax.ShapeDtypeStruct((rows, LANES), F32))
        out_specs.append(pl.BlockSpec((tm, LANES), row))
        args += list(router)
    return pl.pallas_call(
        functools.partial(_outproj_kernel, moe=moe),
        out_shape=out_shape,
        grid=(rows // tm,),
        in_specs=in_specs,
        out_specs=out_specs,
        compiler_params=_cparams(("arbitrary",)),
        name="out_proj",
    )(*args)


def _ffn_kernel(x_ref, h_ref, gate_ref, wg_ref, wu_ref, wd_ref, o_ref, *, fc):
    h = h_ref[...]
    dff = wg_ref.shape[1]
    acc = None
    for c in range(dff // fc):
        a = _dot(h, wg_ref[:, c * fc:(c + 1) * fc])
        u = _dot(h, wu_ref[:, c * fc:(c + 1) * fc])
        t = _dot((_silu(a) * u).astype(BF16), wd_ref[c * fc:(c + 1) * fc, :])
        acc = t if acc is None else acc + t
    x = x_ref[...]
    o_ref[...] = x + gate_ref[...] * acc.reshape(x.shape)


def _ffn(x3, hff, gate, wg, wu, wd, tile_of, tm):
    ng, _, d = x3.shape
    rows = ng * SUBLANES
    gt = tm // SUBLANES
    dff = wg.shape[1]
    fc = dff
    for cand in (1408, 1024, 768, 512, 256, 128):
        if dff % cand == 0:
            fc = cand
            break
    const = lambda t: (0, 0)
    return pl.pallas_call(
        functools.partial(_ffn_kernel, fc=fc),
        out_shape=jax.ShapeDtypeStruct(x3.shape, F32),
        grid=(rows // tm,),
        in_specs=[
            pl.BlockSpec((gt, SUBLANES, d), lambda t: (t, 0, 0)),
            pl.BlockSpec((tm, d), lambda t: (t, 0)),
            pl.BlockSpec((gt, 1, d), lambda t: (tile_of(t), 0, 0)),
            pl.BlockSpec(wg.shape, const),
            pl.BlockSpec(wu.shape, const),
            pl.BlockSpec(wd.shape, const),
        ],
        out_specs=pl.BlockSpec((gt, SUBLANES, d), lambda t: (t, 0, 0)),
        compiler_params=_cparams(("arbitrary",)),
        name="ffn_dense",
    )(x3, hff, gate, wg, wu, wd)


def _moe_kernel(x_ref, h_ref, lg_ref, gate_ref, wg_ref, wu_ref, wd_ref, o_ref, acc_scr, g_scr):
    e = pl.program_id(1)
    tm = h_ref.shape[0]

    @pl.when(e == 0)
    def _():
        lane = lax.broadcasted_iota(I32, (tm, LANES), 1)
        lg = jnp.where(lane < N_EXPERTS, lg_ref[...], -jnp.inf)
        m1 = jnp.max(lg, axis=1, keepdims=True)
        i1 = jnp.min(jnp.where(lg == m1, lane, LANES), axis=1, keepdims=True)
        rest = jnp.where(lane == i1, -jnp.inf, lg)
        m2 = jnp.max(rest, axis=1, keepdims=True)
        i2 = jnp.min(jnp.where(rest == m2, lane, LANES), axis=1, keepdims=True)
        e2 = jnp.exp(m2 - m1)
        den = 1.0 + e2
        g_scr[...] = jnp.where(lane == i1, 1.0 / den, 0.0) + jnp.where(lane == i2, e2 / den, 0.0)
        acc_scr[...] = jnp.zeros(acc_scr.shape, F32)

    h = h_ref[...]
    a = _dot(h, wg_ref[...])
    u = _dot(h, wu_ref[...])
    y = _dot((_silu(a) * u).astype(BF16), wd_ref[...])
    lane = lax.broadcasted_iota(I32, (tm, LANES), 1)
    ge = jnp.sum(jnp.where(lane == e, g_scr[...], 0.0), axis=1, keepdims=True)
    acc_scr[...] += ge * y

    @pl.when(e == pl.num_programs(1) - 1)
    def _():
        x = x_ref[...]
        o_ref[...] = x + gate_ref[...] * acc_scr[...].reshape(x.shape)


def _moe(x3, hff, logits, gate, wg, wu, wd, tile_of, tm):
    ng, _, d = x3.shape
    rows = ng * SUBLANES
    gt = tm // SUBLANES
    ne, _, eff = wg.shape
    return pl.pallas_call(
        _moe_kernel,
        out_shape=jax.ShapeDtypeStruct(x3.shape, F32),
        grid=(rows // tm, ne),
        in_specs=[
            pl.BlockSpec((gt, SUBLANES, d), lambda t, e: (t, 0, 0)),
            pl.BlockSpec((tm, d), lambda t, e: (t, 0)),
            pl.BlockSpec((tm, LANES), lambda t, e: (t, 0)),
            pl.BlockSpec((gt, 1, d), lambda t, e: (tile_of(t), 0, 0)),
            pl.BlockSpec((None, d, eff), lambda t, e: (e, 0, 0)),
            pl.BlockSpec((None, d, eff), lambda t, e: (e, 0, 0)),
            pl.BlockSpec((None, eff, d), lambda t, e: (e, 0, 0)),
        ],
        out_specs=pl.BlockSpec((gt, SUBLANES, d), lambda t, e: (t, 0, 0)),
        scratch_shapes=[pltpu.VMEM((tm, d), F32), pltpu.VMEM((tm, LANES), F32)],
        compiler_params=_cparams(("arbitrary", "arbitrary")),
        name="ffn_moe",
    )(x3, hff, logits, gate, wg, wu, wd)


def _rope_tables(pos):
    pos = pos.astype(F32)[:, None]
    half = HEAD_DIM // 2
    ang = pos * (ROPE_THETA ** (-jnp.arange(half, dtype=F32) / half))[None, :]
    c, s = jnp.cos(ang), jnp.sin(ang)
    cq = jnp.concatenate([c, c], axis=1)
    sq = jnp.concatenate([-s, s], axis=1)
    half = IDX_DIM // 2
    ang = pos * (ROPE_THETA ** (-jnp.arange(half, dtype=F32) / half))[None, :]
    c, s = jnp.cos(ang), jnp.sin(ang)
    z = jnp.zeros_like(s)
    ci = jnp.concatenate([c, c, c, c], axis=1)
    sa = jnp.concatenate([-s, z, -s, z], axis=1)
    sb = jnp.concatenate([z, s, z, s], axis=1)
    return cq, sq, ci, sa, sb


def _lane_vec(vals, lane0):
    return jnp.zeros((1, LANES), F32).at[0, lane0:lane0 + vals.shape[0]].set(vals)


def kernel(x_prompt, x_sample, cache_k, cache_v, cache_kidx, state_gdn, state_conv, page_table, c_prompt, c_sample, mod_w, mod_b, norm_mix_g, norm_ffn_g, w_in, q_norm_g, k_norm_g, conv_w, a_log, dt_bias, gdn_norm_g, w_out, ffn_w_gate, ffn_w_up, ffn_w_down, router_w, router_b, moe_w_gate, moe_w_up, moe_w_down):
    batch, seq, d = x_prompt.shape
    db, t_len, _ = x_sample.shape
    n_layers = mod_w.shape[0]
    n_pages = page_table.shape[1]
    past = n_pages * PAGE
    rows_p, rows_s = batch * seq, db * t_len
    rows = rows_p + rows_s
    assert t_len == SUBLANES and seq % LANES == 0
    tm = min(256, rows_s, seq)
    assert seq % tm == 0 and rows_s % tm == 0
    tiles_per_batch = seq // tm
    n_ptiles = rows_p // tm
    tm_moe = 2 * tm if (seq % (2 * tm) == 0 and rows_s % (2 * tm) == 0) else tm

    def tile_of_for(tile_rows):
        per_batch, n_prompt = seq // tile_rows, rows_p // tile_rows
        return lambda t: jnp.where(t < n_prompt, t // per_batch, batch + (t - n_prompt))

    tile_of = tile_of_for(tm)

    def tab_of(t):
        return jnp.where(t < n_ptiles, t % tiles_per_batch, tiles_per_batch)

    def groups(m, tile_rows=tm):
        mp = jnp.repeat(m[:batch], tile_rows // SUBLANES, axis=0)
        return jnp.concatenate([mp, m[batch:batch + db]], axis=0)[:, None, :]

    x3 = jnp.concatenate([x_prompt.reshape(rows_p, d), x_sample.reshape(rows_s, d)], axis=0)
    x3 = x3.reshape(rows // SUBLANES, SUBLANES, d)

    n_c = batch + db
    c_all = jnp.concatenate([c_prompt, c_sample, jnp.zeros((-n_c % SUBLANES, d), F32)], axis=0)
    mods = _mods(c_all, mod_w, mod_b)

    pos = jnp.concatenate([jnp.arange(seq), past + (jnp.arange(tm) % t_len)])
    tabs = _rope_tables(pos)

    kvd = KV_HEADS * HEAD_DIM
    depth = cache_k.shape[1]
    cache_k2 = cache_k.reshape(-1, HEAD_DIM)
    cache_v2 = cache_v.reshape(-1, HEAD_DIM)
    cache_kidx_t = jnp.swapaxes(cache_kidx, 2, 3)
    n_sel_s = min(TOPK_MAX, (past + t_len) // 4)
    lp = (n_pages + 1) * PAGE

    outs = {name: [] for name in ("kp", "vp", "kip", "ks", "vs", "kis", "sp", "ss", "cp", "cs")}
    col = 0
    offs = []
    for size in (ATTN_HEADS * HEAD_DIM, kvd, kvd, IDX_HEADS * IDX_DIM, IDX_DIM, IDX_HEADS,
                 GDN_HEADS * 3 * HEAD_DIM, GDN_HEADS * HEAD_DIM, GDN_HEADS, GDN_HEADS):
        offs.append((col, col + size))
        col += size
    o_q, o_k, o_v, o_qi, o_ki, o_wi, o_gq, o_gz, o_ga, o_gb = offs

    for l in range(n_layers):
        m6 = [groups(mods[l, :, j * d:(j + 1) * d]) for j in range(6)]
        w = w_in[l]
        wa = w[:, o_q[0]:o_v[1]].astype(BF16)
        qi_w = w[:, o_qi[0]:o_qi[1]].reshape(d, IDX_HEADS, 1, IDX_DIM)
        qi_w = jnp.broadcast_to(qi_w, (d, IDX_HEADS, 2, IDX_DIM)).reshape(d, IDX_HEADS * LANES)
        ki_w = w[:, o_ki[0]:o_ki[1]]
        misc_w = jnp.concatenate([w[:, o_wi[0]:o_wi[1]], w[:, o_ga[0]:o_ga[1]], w[:, o_gb[0]:o_gb[1]],
                                  jnp.zeros((d, LANES - IDX_HEADS - 2 * GDN_HEADS), F32)], axis=1)
        wi_f = jnp.concatenate([qi_w, ki_w, ki_w, misc_w], axis=1)
        wih = wi_f.astype(BF16)
        wil = (wi_f - wih.astype(F32)).astype(BF16)
        wg = w[:, o_gq[0]:o_gz[1]].astype(BF16)

        (q_bf, k_f, v_f, k_bf, v_bf, qi3, ki_f, ki3, small, gq, gz) = _inproj(
            x3, m6[0], m6[1], norm_mix_g[l][None, :], wa, wih, wil, wg,
            q_norm_g[l][None, :], k_norm_g[l][None, :], tabs, tile_of, tab_of, tm)

        small_t = jnp.transpose(small[:, GA_LANE:GA_LANE + 2 * GDN_HEADS])
        alog_l = _lane_vec(a_log[l], GA_LANE)
        dtb_l = _lane_vec(dt_bias[l], GA_LANE)
        pad4 = jnp.zeros((SUBLANES - GDN_HEADS,), F32)
        alog_c = jnp.concatenate([a_log[l], pad4])[:, None]
        dtb_c = jnp.concatenate([dt_bias[l], pad4])[:, None]
        ng = gdn_norm_g[l][None, :]

        att_p = _dsa_prompt(qi3, small, q_bf, ki3, k_bf, v_bf, batch, seq)
        gdn_p, s_p = _gdn_prompt(gq, gz, small, small_t, conv_w[l], alog_l, dtb_l, alog_c, dtb_c, ng,
                                 batch, seq)

        scores = _sample_scores(page_table, qi3, small, ki3, cache_kidx_t, l, rows_p, t_len)
        bias = _sample_select(scores.reshape(rows_s, lp), n_sel_s)
        att_s = _sample_attend(page_table, q_bf, bias, k_bf, v_bf, cache_k2, cache_v2, depth, l, rows_p, t_len)
        cstate = jnp.pad(state_conv[:, l], ((0, 0), (SUBLANES - (CONV_W - 1), 0), (0, 0)))
        cstate = cstate.reshape(db * SUBLANES, cstate.shape[2])
        gdn_s, s_s = _gdn_sample(gq, gz, small, small_t, cstate, state_gdn, l, conv_w[l], alog_l, dtb_l,
                                 alog_c, dtb_c, ng, rows_p, db, t_len)

        att = jnp.concatenate([att_p, att_s], axis=0)
        gdn = jnp.concatenate([gdn_p, gdn_s], axis=0)
        i = l // 2
        router = None
        if l % 2 == 1:
            rw = jnp.concatenate([router_w[i], jnp.zeros((d, LANES - N_EXPERTS), F32)], axis=1)
            rb = jnp.concatenate([router_b[i], jnp.zeros((LANES - N_EXPERTS,), F32)])[None, :]
            router = (rw, rb)
        res = _outproj(x3, att, gdn, m6[2], m6[3], m6[4], norm_ffn_g[l][None, :], w_out[l].astype(BF16),
                       router, tile_of, tm)
        if l % 2 == 0:
            x3, hff = res
            x3 = _ffn(x3, hff, m6[5], ffn_w_gate[i].astype(BF16), ffn_w_up[i].astype(BF16),
                      ffn_w_down[i].astype(BF16), tile_of, tm)
        else:
            x3, hff, logits = res
            gate_moe = groups(mods[l, :, 5 * d:6 * d], tm_moe)
            x3 = _moe(x3, hff, logits, gate_moe, moe_w_gate[i].astype(BF16), moe_w_up[i].astype(BF16),
                      moe_w_down[i].astype(BF16), tile_of_for(tm_moe), tm_moe)

        outs["kp"].append(k_f[:rows_p].reshape(batch, seq, KV_HEADS, HEAD_DIM))
        outs["vp"].append(v_f[:rows_p].reshape(batch, seq, KV_HEADS, HEAD_DIM))
        outs["kip"].append(ki_f[:rows_p].reshape(batch, seq, IDX_DIM))
        outs["ks"].append(k_f[rows_p:].reshape(db, t_len, KV_HEADS, HEAD_DIM))
        outs["vs"].append(v_f[rows_p:].reshape(db, t_len, KV_HEADS, HEAD_DIM))
        outs["kis"].append(ki_f[rows_p:].reshape(db, t_len, IDX_DIM))
        outs["sp"].append(s_p)
        outs["ss"].append(s_s)
        tail = CONV_W - 1
        outs["cp"].append(jnp.stack([gq[(b + 1) * seq - tail:(b + 1) * seq] for b in range(batch)], axis=0))
        gq_s = gq[rows_p:].reshape(db, t_len, -1)
        outs["cs"].append(gq_s[:, t_len - tail:])

    x2 = x3.reshape(rows, d)
    st = lambda name: jnp.stack(outs[name], axis=1)
    return (x2[:rows_p].reshape(batch, seq, d), x2[rows_p:].reshape(db, t_len, d),
            st("kp"), st("vp"), st("kip"), st("ks"), st("vs"), st("kis"),
            st("sp"), st("ss"), st("cp"), st("cs"))
```

```python
import functools
import math

import jax
import jax.numpy as jnp
from jax import lax
from jax.experimental import pallas as pl
from jax.experimental.pallas import tpu as pltpu

F32 = jnp.float32
BF16 = jnp.bfloat16
I32 = jnp.int32

HEAD_DIM = 128
ATTN_HEADS = 4
KV_HEADS = 2
GROUP = ATTN_HEADS // KV_HEADS
IDX_HEADS = 8
IDX_DIM = 64
TOPK_MAX = 256
GDN_HEADS = 4
GDN_CHUNK = 64
CONV_W = 4
N_EXPERTS = 8
PAGE = 128
ROPE_THETA = 10000.0
EPS = 1e-6

LANES = 128
SUBLANES = 8
VMEM_LIMIT = 56 * 1024 * 1024
NEG = -1e30
INT_MIN = -2147483648
WI_LANE = 0
GA_LANE = 8
GB_LANE = 12


def _cparams(sem):
    return pltpu.CompilerParams(dimension_semantics=sem, vmem_limit_bytes=VMEM_LIMIT)


def _dot(a, b):
    return jnp.dot(a, b, preferred_element_type=F32)


def _dot_nt(a, b):
    return lax.dot_general(a, b, (((1,), (1,)), ((), ())), preferred_element_type=F32)


def _dot_tn(a, b):
    return lax.dot_general(a, b, (((0,), (0,)), ((), ())), preferred_element_type=F32)


def _split(x):
    hi = x.astype(BF16)
    lo = (x - hi.astype(F32)).astype(BF16)
    return hi, lo


def _mm3(a, b, dot=_dot):
    ah, al = _split(a)
    bh, bl = _split(b)
    return dot(ah, bh) + (dot(ah, bl) + dot(al, bh))


def _split_three(a):
    a1 = a.astype(BF16)
    r1 = a - a1.astype(F32)
    a2 = r1.astype(BF16)
    a3 = (r1 - a2.astype(F32)).astype(BF16)
    return a1, a2, a3


def _mm_exact_rhs(a, b_bf16):
    a1, a2, a3 = _split_three(a)
    return _dot(a1, b_bf16) + (_dot(a2, b_bf16) + _dot(a3, b_bf16))


def _mm_exact_lhs(m_bf16, a):
    a1, a2, a3 = _split_three(a)
    return _dot(m_bf16, a1) + (_dot(m_bf16, a2) + _dot(m_bf16, a3))


def _silu(x):
    return x * jax.nn.sigmoid(x)


def _softplus(x):
    return jnp.maximum(x, 0.0) + jnp.log1p(jnp.exp(-jnp.abs(x)))


def _mods_kernel(c_ref, w_ref, b_ref, o_ref):
    o_ref[...] = _mm3(_silu(c_ref[...]), w_ref[...]) + b_ref[...]


def _mods(c_all, mod_w, mod_b):
    n_layers, d, n6 = mod_w.shape
    rows = c_all.shape[0]
    tn = n6 // 4
    return pl.pallas_call(
        _mods_kernel,
        out_shape=jax.ShapeDtypeStruct((n_layers, rows, n6), F32),
        grid=(n_layers, n6 // tn),
        in_specs=[
            pl.BlockSpec((rows, d), lambda l, j: (0, 0)),
            pl.BlockSpec((None, d, tn), lambda l, j: (l, 0, j)),
            pl.BlockSpec((None, 1, tn), lambda l, j: (l, 0, j)),
        ],
        out_specs=pl.BlockSpec((None, rows, tn), lambda l, j: (l, 0, j)),
        compiler_params=_cparams(("arbitrary", "arbitrary")),
        name="ada_mods",
    )(c_all, mod_w, mod_b.reshape(n_layers, 1, n6))


def _rms(x):
    return x * lax.rsqrt(jnp.mean(x * x, axis=-1, keepdims=True) + EPS)


def _inproj_kernel(x_ref, sh_ref, sc_ref, g_ref, wa_ref, wih_ref, wil_ref, wg_ref, qg_ref, kg_ref,
                   cq_ref, sq_ref, ci_ref, sa_ref, sb_ref,
                   q_o, k_o, v_o, kb_o, vb_o, qi3_o, ki_o, ki3_o, small_o, gq_o, gz_o):
    x = x_ref[...]
    h = (_rms(x) * g_ref[...]) * (1.0 + sc_ref[...]) + sh_ref[...]
    tm = x.shape[0] * x.shape[1]
    h2 = h.reshape(tm, x.shape[2])
    hb, hl = _split(h2)
    za = _dot(hb, wa_ref[...])
    wih = wih_ref[...]
    zi = _dot(hb, wih) + (_dot(hb, wil_ref[...]) + _dot(hl, wih))
    zg = _dot(hb, wg_ref[...])

    cq, sq = cq_ref[...], sq_ref[...]
    qg, kg = qg_ref[...], kg_ref[...]
    for hh in range(ATTN_HEADS):
        qn = _rms(za[:, hh * HEAD_DIM:(hh + 1) * HEAD_DIM]) * qg
        qr = qn * cq + pltpu.roll(qn, HEAD_DIM // 2, 1) * sq
        q_o[:, hh * HEAD_DIM:(hh + 1) * HEAD_DIM] = (qr * (HEAD_DIM ** -0.5)).astype(BF16)
    koff = ATTN_HEADS * HEAD_DIM
    for hh in range(KV_HEADS):
        kn = _rms(za[:, koff + hh * HEAD_DIM:koff + (hh + 1) * HEAD_DIM]) * kg
        kr = kn * cq + pltpu.roll(kn, HEAD_DIM // 2, 1) * sq
        k_o[:, hh * HEAD_DIM:(hh + 1) * HEAD_DIM] = kr
        kb_o[:, hh * HEAD_DIM:(hh + 1) * HEAD_DIM] = kr.astype(BF16)
    voff = koff + KV_HEADS * HEAD_DIM
    v = za[:, voff:voff + KV_HEADS * HEAD_DIM]
    v_o[...] = v
    vb_o[...] = v.astype(BF16)

    ci, sa, sb = ci_ref[...], sa_ref[...], sb_ref[...]
    first = lax.broadcasted_iota(I32, (tm, LANES), 1) < IDX_DIM

    def rope64(t):
        return t * ci + pltpu.roll(t, LANES - IDX_DIM // 2, 1) * sa + pltpu.roll(t, IDX_DIM // 2, 1) * sb

    for hh in range(IDX_HEADS):
        r = rope64(zi[:, hh * LANES:(hh + 1) * LANES]) * (IDX_DIM ** -0.5)
        hi = r.astype(BF16).astype(F32)
        qi3_o[hh, :, 0:LANES] = jnp.where(first, hi, r - hi).astype(BF16)
        qi3_o[hh, :, LANES:2 * LANES] = jnp.where(first, hi, 0.0).astype(BF16)
    r = rope64(zi[:, IDX_HEADS * LANES:(IDX_HEADS + 1) * LANES])
    ki_o[...] = r[:, :IDX_DIM]
    hi = r.astype(BF16).astype(F32)
    ki3_o[:, 0:LANES] = hi.astype(BF16)
    ki3_o[:, LANES:2 * LANES] = jnp.where(first, r - hi, 0.0).astype(BF16)
    misc = zi[:, (IDX_HEADS + 1) * LANES:(IDX_HEADS + 2) * LANES]
    lane = lax.broadcasted_iota(I32, (tm, LANES), 1)
    small_o[...] = jnp.where(lane < IDX_HEADS, misc * (IDX_HEADS ** -0.5), misc)
    gdim = gq_o.shape[1]
    gq_o[...] = zg[:, :gdim]
    gz_o[...] = zg[:, gdim:]


def _inproj(x3, sh, sc, g, wa, wih, wil, wg, qg, kg, tabs, tile_of, tab_of, tm):
    ng, _, d = x3.shape
    rows = ng * SUBLANES
    gt = tm // SUBLANES
    nt = rows // tm
    gdim = GDN_HEADS * 3 * HEAD_DIM
    zdim = GDN_HEADS * HEAD_DIM
    const = lambda t: (0, 0)
    row = lambda t: (t, 0)
    tab = lambda t: (tab_of(t), 0)
    in_specs = [
        pl.BlockSpec((gt, SUBLANES, d), lambda t: (t, 0, 0)),
        pl.BlockSpec((gt, 1, d), lambda t: (tile_of(t), 0, 0)),
        pl.BlockSpec((gt, 1, d), lambda t: (tile_of(t), 0, 0)),
        pl.BlockSpec((1, d), const),
        pl.BlockSpec(wa.shape, const),
        pl.BlockSpec(wih.shape, const),
        pl.BlockSpec(wil.shape, const),
        pl.BlockSpec(wg.shape, const),
        pl.BlockSpec((1, HEAD_DIM), const),
        pl.BlockSpec((1, HEAD_DIM), const),
    ] + [pl.BlockSpec((tm, LANES), tab)] * 5
    kvd = KV_HEADS * HEAD_DIM
    out_shape = [
        jax.ShapeDtypeStruct((rows, ATTN_HEADS * HEAD_DIM), BF16),
        jax.ShapeDtypeStruct((rows, kvd), F32),
        jax.ShapeDtypeStruct((rows, kvd), F32),
        jax.ShapeDtypeStruct((rows, kvd), BF16),
        jax.ShapeDtypeStruct((rows, kvd), BF16),
        jax.ShapeDtypeStruct((IDX_HEADS, rows, 2 * LANES), BF16),
        jax.ShapeDtypeStruct((rows, IDX_DIM), F32),
        jax.ShapeDtypeStruct((rows, 2 * LANES), BF16),
        jax.ShapeDtypeStruct((rows, LANES), F32),
        jax.ShapeDtypeStruct((rows, gdim), F32),
        jax.ShapeDtypeStruct((rows, zdim), F32),
    ]
    out_specs = [
        pl.BlockSpec((tm, ATTN_HEADS * HEAD_DIM), row),
        pl.BlockSpec((tm, kvd), row),
        pl.BlockSpec((tm, kvd), row),
        pl.BlockSpec((tm, kvd), row),
        pl.BlockSpec((tm, kvd), row),
        pl.BlockSpec((IDX_HEADS, tm, 2 * LANES), lambda t: (0, t, 0)),
        pl.BlockSpec((tm, IDX_DIM), row),
        pl.BlockSpec((tm, 2 * LANES), row),
        pl.BlockSpec((tm, LANES), row),
        pl.BlockSpec((tm, gdim), row),
        pl.BlockSpec((tm, zdim), row),
    ]
    return pl.pallas_call(
        _inproj_kernel,
        out_shape=out_shape,
        grid=(nt,),
        in_specs=in_specs,
        out_specs=out_specs,
        compiler_params=_cparams(("arbitrary",)),
        name="in_proj",
    )(x3, sh, sc, g, wa, wih, wil, wg, qg, kg, *tabs)


def _sort_key(score):
    bits = pltpu.bitcast(score, I32)
    return jnp.where(bits < 0, bits ^ jnp.int32(0x7FFFFFFF), bits)


def _count(key_scr, nkc, kc, rows, n, preds):
    def body(c, accs):
        base = pl.multiple_of(c * kc, kc)
        accs = list(accs)
        for j in range(kc // LANES):
            sc = key_scr[:, pl.ds(base + j * LANES, LANES)]
            ps = preds(sc, base + j * LANES)
            for i in range(n):
                accs[i] = accs[i] + jnp.where(ps[i], 1.0, 0.0)
        return tuple(accs)

    accs = lax.fori_loop(0, nkc, body, tuple(jnp.zeros((rows, LANES), F32) for _ in range(n)))
    return [jnp.sum(a, axis=1, keepdims=True) for a in accs]


def _select_threshold(key_scr, nkc, kc, rows, n_sel, idx_bits, resolve_ties=True, bounds=None):
    n_sel_f = float(n_sel)

    def bit_step(b, acc):
        cand = acc | (jnp.int32(1) << (31 - b))
        cand_s = jnp.broadcast_to(cand ^ jnp.int32(INT_MIN), (rows, LANES))
        cnt, = _count(key_scr, nkc, kc, rows, 1, lambda kk, c0: (kk >= cand_s,))
        return jnp.where(cnt >= n_sel_f, cand, acc)

    first_bit, acc0 = 0, jnp.zeros((rows, 1), I32)
    if bounds is not None:
        u_lo, u_hi = bounds[0] ^ jnp.int32(INT_MIN), bounds[1] ^ jnp.int32(INT_MIN)
        shared = jnp.min(lax.clz(u_lo ^ u_hi).astype(F32))
        first_bit = jnp.minimum(shared.astype(I32), 31)
        acc0 = u_hi & jnp.where(first_bit > 0, jnp.int32(-1) << (32 - jnp.maximum(first_bit, 1)), 0)
    acc = lax.fori_loop(first_bit, 32, bit_step, acc0)
    thr = acc ^ jnp.int32(INT_MIN)
    thr_b = jnp.broadcast_to(thr, (rows, LANES))
    cnt_gt, cnt_eq = _count(key_scr, nkc, kc, rows, 2, lambda kk, c0: (kk > thr_b, kk == thr_b))
    need = n_sel_f - cnt_gt
    excess = (acc != 0) & (cnt_eq > need)
    any_excess = jnp.max(jnp.where(excess, 1.0, 0.0)) > 0.0
    lane = lax.broadcasted_iota(I32, (rows, LANES), 1)

    def resolve():
        def idx_step(b, p):
            cand = p | (jnp.int32(1) << (idx_bits - 1 - b))
            cand_b = jnp.broadcast_to(cand, (rows, LANES))
            cnt, = _count(key_scr, nkc, kc, rows, 1,
                          lambda kk, c0: ((kk == thr_b) & ((lane + c0) < cand_b),))
            return jnp.where(cnt < need, cand, p)

        cut = lax.fori_loop(0, idx_bits, idx_step, jnp.zeros((rows, 1), I32))
        cut_b = jnp.broadcast_to(cut, (rows, LANES))
        drop_row = jnp.broadcast_to(excess, (rows, LANES))

        def drop(c, carry):
            base = pl.multiple_of(c * kc, kc)
            for j in range(kc // LANES):
                sl = pl.ds(base + j * LANES, LANES)
                kk = key_scr[:, sl]
                kill = drop_row & (kk == thr_b) & ((lane + (base + j * LANES)) > cut_b)
                key_scr[:, sl] = jnp.where(kill, jnp.int32(INT_MIN), kk)
            return carry

        lax.fori_loop(0, nkc, drop, 0)

    if resolve_ties:
        pl.when(any_excess)(resolve)
    return jnp.maximum(thr, jnp.int32(INT_MIN + 1)), any_excess


def _select_threshold_lanes(keyt_scr, n_sel, lo, hi):
    nkeys, nq = keyt_scr.shape
    n_f = float(n_sel)

    def count(n, preds):
        ways = 8
        accs = [[jnp.zeros((SUBLANES, nq), F32) for _ in range(ways)] for _ in range(n)]
        for v in range(nkeys // SUBLANES):
            ps = preds(keyt_scr[v * SUBLANES:(v + 1) * SUBLANES, :])
            for i in range(n):
                accs[i][v % ways] = accs[i][v % ways] + jnp.where(ps[i], 1.0, 0.0)
        out = []
        for i in range(n):
            parts = accs[i]
            while len(parts) > 1:
                parts = [parts[j] + parts[j + 1] for j in range(0, len(parts), 2)]
            out.append(jnp.sum(parts[0], axis=0, keepdims=True))
        return out

    def bit_step(b, acc):
        cand = acc | (jnp.int32(1) << (31 - b))
        cand_s = jnp.broadcast_to(cand ^ jnp.int32(INT_MIN), (SUBLANES, nq))
        cnt, = count(1, lambda kk: (kk >= cand_s,))
        return jnp.where(cnt >= n_f, cand, acc)

    u_lo, u_hi = lo ^ jnp.int32(INT_MIN), hi ^ jnp.int32(INT_MIN)
    first_bit = jnp.minimum(jnp.min(lax.clz(u_lo ^ u_hi).astype(F32)).astype(I32), 31)
    acc0 = u_hi & jnp.where(first_bit > 0, jnp.int32(-1) << (32 - jnp.maximum(first_bit, 1)), 0)
    acc = lax.fori_loop(first_bit, 32, bit_step, acc0)
    thr = acc ^ jnp.int32(INT_MIN)
    thr_s = jnp.broadcast_to(thr, (SUBLANES, nq))
    cnt_gt, cnt_eq = count(2, lambda kk: (kk > thr_s, kk == thr_s))
    tied = cnt_eq > n_f - cnt_gt
    last = keyt_scr[nkeys - LANES:nkeys, :]
    dropped = jnp.max(jnp.where(last >= jnp.broadcast_to(thr, (LANES, nq)), 1.0, 0.0), axis=0, keepdims=True) > 0.0
    redo = jnp.max(jnp.where(jnp.logical_or(tied, dropped), 1.0, 0.0)) > 0.0
    return thr, redo


LANE_TOP = 12
STREAMS = 2


def _dsa_prompt_kernel(qi3_ref, small_ref, q_ref, ki3_ref, k_ref, v_ref, o_ref, key_scr, w_scr, cand_scr,
                       ckey_scr, m_scr, acc_scr, *, tq, kc, n_sel, idx_bits):
    i = pl.program_id(1)
    nkc = (i * tq + tq + kc - 1) // kc
    q3 = qi3_ref[...].reshape(IDX_HEADS * tq, 2 * LANES)
    wi = small_ref[:, WI_LANE:WI_LANE + IDX_HEADS]
    for h in range(IDX_HEADS):
        w_scr[h] = jnp.broadcast_to(wi[:, h:h + 1], (tq, LANES))
    row = i * tq + lax.broadcasted_iota(I32, (tq, LANES), 0)
    lane = lax.broadcasted_iota(I32, (tq, LANES), 1)
    cand_chunks = LANE_TOP * LANES // kc
    reduce_keys = nkc > cand_chunks
    cand_scr[...] = jnp.full(cand_scr.shape, -jnp.inf, F32)

    def score_chunk(c, carry):
        base = pl.multiple_of(c * kc, kc)
        s = _dot_nt(q3, ki3_ref[pl.ds(base, kc), :])
        keys = []
        for j in range(kc // LANES):
            acc = None
            for h in range(IDX_HEADS):
                t = w_scr[h] * jnp.maximum(s[h * tq:(h + 1) * tq, j * LANES:(j + 1) * LANES], 0.0)
                acc = t if acc is None else acc + t
            valid = (lane + (base + j * LANES)) <= row
            key_scr[:, pl.ds(base + j * LANES, LANES)] = jnp.where(valid, _sort_key(acc), jnp.int32(INT_MIN))
            keys.append(jnp.where(valid, acc, -jnp.inf))
        for lvl in range(LANE_TOP):
            kept = cand_scr[:, lvl * LANES:(lvl + 1) * LANES]
            for j in range(len(keys)):
                kept, keys[j] = jnp.maximum(kept, keys[j]), jnp.minimum(kept, keys[j])
            cand_scr[:, lvl * LANES:(lvl + 1) * LANES] = kept
        return carry

    lax.fori_loop(0, nkc, score_chunk, 0)

    def from_candidates():
        for lvl in range(LANE_TOP):
            sc = cand_scr[:, lvl * LANES:(lvl + 1) * LANES].T
            ckey_scr[lvl * LANES:(lvl + 1) * LANES, :] = jnp.where(sc > -jnp.inf, _sort_key(sc), jnp.int32(INT_MIN))
        deep = (n_sel + LANES - 1) // LANES - 1
        lo = jnp.min(ckey_scr[deep * LANES:(deep + 1) * LANES, :].astype(F32), axis=0, keepdims=True)
        hi = jnp.max(ckey_scr[0:LANES, :].astype(F32), axis=0, keepdims=True)
        lo = jnp.maximum(lo - 256.0, -2147483000.0).astype(I32)
        hi = jnp.minimum(hi + 256.0, 2147483000.0).astype(I32)
        thr_row, redo_c = _select_threshold_lanes(ckey_scr, n_sel, lo, hi)
        thr_sq = pltpu.bitcast(jnp.broadcast_to(thr_row, (tq, LANES)), F32).T
        return pltpu.bitcast(thr_sq, I32), redo_c

    thr_c, redo = lax.cond(reduce_keys, from_candidates,
                           lambda: (jnp.zeros((tq, LANES), I32), jnp.bool_(True)))
    thr_b = lax.cond(
        redo,
        lambda: jnp.broadcast_to(_select_threshold(key_scr, nkc, kc, tq, n_sel, idx_bits)[0], (tq, LANES)),
        lambda: thr_c)

    qs = []
    for g in range(KV_HEADS):
        qs.append(jnp.concatenate(
            [q_ref[:, (g * GROUP + a) * HEAD_DIM:(g * GROUP + a + 1) * HEAD_DIM] for a in range(GROUP)], axis=0))

    rq = GROUP * tq
    ones = jnp.ones((kc, HEAD_DIM), BF16)
    m_scr[...] = jnp.full(m_scr.shape, NEG, F32)
    acc_scr[...] = jnp.zeros(acc_scr.shape, F32)

    def attend_step(cc, carry):
        for st in range(STREAMS):
            c = cc * STREAMS + st
            live = c < nkc
            base = pl.multiple_of(jnp.minimum(c, nkc - 1) * kc, kc)
            biases = []
            for j in range(kc // LANES):
                sel = jnp.logical_and(key_scr[:, pl.ds(base + j * LANES, LANES)] >= thr_b, live)
                biases.append(jnp.where(sel, 0.0, NEG))
            bias = jnp.concatenate(biases, axis=1)
            bias = jnp.concatenate([bias] * GROUP, axis=0)
            for g in range(KV_HEADS):
                slot = st * KV_HEADS + g
                kg = k_ref[pl.ds(base, kc), g * HEAD_DIM:(g + 1) * HEAD_DIM]
                vg = jnp.concatenate([v_ref[pl.ds(base, kc), g * HEAD_DIM:(g + 1) * HEAD_DIM], ones], axis=1)
                s = _dot_nt(qs[g], kg) + bias
                m = m_scr[slot]
                m_new = jnp.maximum(m, jnp.broadcast_to(jnp.max(s, axis=1, keepdims=True), (rq, LANES)))
                alpha = jnp.exp(m - m_new)
                p = jnp.concatenate([jnp.exp(s[:, j * LANES:(j + 1) * LANES] - m_new)
                                     for j in range(kc // LANES)], axis=1)
                m_scr[slot] = m_new
                acc_scr[slot] = jnp.concatenate([alpha, alpha], axis=1) * acc_scr[slot] + _dot(p.astype(BF16), vg)
        return carry

    lax.fori_loop(0, (nkc + STREAMS - 1) // STREAMS, attend_step, 0)
    for g in range(KV_HEADS):
        m_all = m_scr[g]
        for st in range(1, STREAMS):
            m_all = jnp.maximum(m_all, m_scr[st * KV_HEADS + g])
        acc = None
        for st in range(STREAMS):
            w = jnp.exp(m_scr[st * KV_HEADS + g] - m_all)
            part = jnp.concatenate([w, w], axis=1) * acc_scr[st * KV_HEADS + g]
            acc = part if acc is None else acc + part
        o = acc[:, :HEAD_DIM] / acc[:, HEAD_DIM:]
        for a in range(GROUP):
            hh = g * GROUP + a
            o_ref[:, hh * HEAD_DIM:(hh + 1) * HEAD_DIM] = o[a * tq:(a + 1) * tq].astype(BF16)


def _dsa_prompt(qi3, small, q_bf, ki3, k_bf, v_bf, batch, seq):
    tq = LANES
    kc = min(512, seq)
    nq = seq // tq
    n_sel = min(TOPK_MAX, seq // 4)
    idx_bits = max(1, (seq - 1).bit_length())
    kvd = KV_HEADS * HEAD_DIM
    qrow = lambda b, i: (b * nq + i, 0)
    kern = functools.partial(_dsa_prompt_kernel, tq=tq, kc=kc, n_sel=n_sel, idx_bits=idx_bits)
    return pl.pallas_call(
        kern,
        out_shape=jax.ShapeDtypeStruct((batch * seq, ATTN_HEADS * HEAD_DIM), BF16),
        grid=(batch, nq),
        in_specs=[
            pl.BlockSpec((IDX_HEADS, tq, 2 * LANES), lambda b, i: (0, b * nq + i, 0)),
            pl.BlockSpec((tq, LANES), qrow),
            pl.BlockSpec((tq, ATTN_HEADS * HEAD_DIM), qrow),
            pl.BlockSpec((seq, 2 * LANES), lambda b, i: (b, 0)),
            pl.BlockSpec((seq, kvd), lambda b, i: (b, 0)),
            pl.BlockSpec((seq, kvd), lambda b, i: (b, 0)),
        ],
        out_specs=pl.BlockSpec((tq, ATTN_HEADS * HEAD_DIM), qrow),
        scratch_shapes=[pltpu.VMEM((tq, seq), I32), pltpu.VMEM((IDX_HEADS, tq, LANES), F32),
                        pltpu.VMEM((tq, LANE_TOP * LANES), F32),
                        pltpu.VMEM((LANE_TOP * LANES, tq), I32),
                        pltpu.VMEM((STREAMS * KV_HEADS, GROUP * tq, LANES), F32),
                        pltpu.VMEM((STREAMS * KV_HEADS, GROUP * tq, 2 * HEAD_DIM), F32)],
        compiler_params=_cparams(("arbitrary", "arbitrary")),
        name="dsa_prompt",
    )(qi3, small, q_bf, ki3, k_bf, v_bf)


SEQ_PER_STEP = 2


def _sample_scores_kernel(pt_ref, qi3_ref, small_ref, ki3n_ref, *rest, n_pages, t_len):
    pages = rest[:SEQ_PER_STEP * n_pages]
    s_o = rest[SEQ_PER_STEP * n_pages]
    nr = SEQ_PER_STEP * t_len
    q3 = qi3_ref[...].reshape(IDX_HEADS * nr, 2 * LANES)
    wi = small_ref[:, WI_LANE:WI_LANE + IDX_HEADS]
    knew = jnp.concatenate([ki3n_ref[...], jnp.zeros((PAGE - nr, 2 * LANES), BF16)], axis=0)
    zpad = jnp.zeros((2 * LANES - 3 * IDX_DIM, PAGE), F32)
    lane = lax.broadcasted_iota(I32, (t_len, LANES), 1)
    trow = lax.broadcasted_iota(I32, (t_len, LANES), 0)

    def head_sum(s, j):
        acc = None
        for h in range(IDX_HEADS):
            r0 = h * nr + j * t_len
            t = wi[j * t_len:(j + 1) * t_len, h:h + 1] * jnp.maximum(s[r0:r0 + t_len], 0.0)
            acc = t if acc is None else acc + t
        return acc

    for j in range(SEQ_PER_STEP):
        for p in range(n_pages):
            kp = pages[j * n_pages + p][...]
            hi = kp.astype(BF16).astype(F32)
            k3 = jnp.concatenate([hi, hi, kp - hi, zpad], axis=0).astype(BF16)
            s_o[j, :, p * PAGE:(p + 1) * PAGE] = head_sum(_dot(q3, k3), j)
        sn = head_sum(_dot_nt(q3, knew), j)
        ok = (lane >= j * t_len) & (lane - j * t_len <= trow)
        s_o[j, :, n_pages * PAGE:(n_pages + 1) * PAGE] = jnp.where(ok, sn, -jnp.inf)


def _sample_scores(page_table, qi3, small, ki3, cache_kidx_t, layer, rows_p, t_len):
    db, n_pages = page_table.shape
    nr = SEQ_PER_STEP * t_len
    base = rows_p // nr
    lp = (n_pages + 1) * PAGE
    in_specs = [
        pl.BlockSpec((IDX_HEADS, nr, 2 * LANES), lambda n, pt: (0, base + n, 0)),
        pl.BlockSpec((nr, LANES), lambda n, pt: (base + n, 0)),
        pl.BlockSpec((nr, 2 * LANES), lambda n, pt: (base + n, 0)),
    ]
    for j in range(SEQ_PER_STEP):
        for p in range(n_pages):
            in_specs.append(pl.BlockSpec(
                (None, None, IDX_DIM, PAGE),
                lambda n, pt, j=j, p=p: (pt[n * SEQ_PER_STEP + j, p], layer, 0, 0)))
    kern = functools.partial(_sample_scores_kernel, n_pages=n_pages, t_len=t_len)
    return pl.pallas_call(
        kern,
        out_shape=jax.ShapeDtypeStruct((db, t_len, lp), F32),
        grid_spec=pltpu.PrefetchScalarGridSpec(
            num_scalar_prefetch=1,
            grid=(db // SEQ_PER_STEP,),
            in_specs=in_specs,
            out_specs=pl.BlockSpec((SEQ_PER_STEP, t_len, lp), lambda n, pt: (n, 0, 0)),
        ),
        compiler_params=_cparams(("arbitrary",)),
        name="dsa_sample_scores",
    )(page_table, qi3, small, ki3, *([cache_kidx_t] * (SEQ_PER_STEP * n_pages)))


def _sample_select_kernel(s_ref, b_ref, key_scr, *, n_sel, idx_bits):
    rows, lp = s_ref.shape
    nkc = lp // LANES
    for c in range(nkc):
        sc = s_ref[:, c * LANES:(c + 1) * LANES]
        key_scr[:, c * LANES:(c + 1) * LANES] = jnp.where(sc > -jnp.inf, _sort_key(sc), jnp.int32(INT_MIN))
    thr, _ = _select_threshold(key_scr, nkc, LANES, rows, n_sel, idx_bits)
    thr_b = jnp.broadcast_to(thr, (rows, LANES))
    for c in range(nkc):
        b_ref[:, c * LANES:(c + 1) * LANES] = jnp.where(key_scr[:, c * LANES:(c + 1) * LANES] >= thr_b, 0.0, NEG)


def _sample_select(scores2d, n_sel):
    rows, lp = scores2d.shape
    tr = min(LANES, rows)
    kern = functools.partial(_sample_select_kernel, n_sel=n_sel, idx_bits=max(1, (lp - 1).bit_length()))
    return pl.pallas_call(
        kern,
        out_shape=jax.ShapeDtypeStruct((rows, lp), F32),
        grid=(rows // tr,),
        in_specs=[pl.BlockSpec((tr, lp), lambda r: (r, 0))],
        out_specs=pl.BlockSpec((tr, lp), lambda r: (r, 0)),
        scratch_shapes=[pltpu.VMEM((tr, lp), I32)],
        compiler_params=_cparams(("arbitrary",)),
        name="dsa_sample_select",
    )(scores2d)


def _sample_attend_kernel(pt_ref, q_ref, bias_ref, kn_ref, vn_ref, *rest, n_pages, t_len):
    npg = SEQ_PER_STEP * n_pages
    kpages, vpages = rest[:npg], rest[npg:2 * npg]
    o_ref = rest[2 * npg]
    kc_scr, vc_scr = rest[2 * npg + 1], rest[2 * npg + 2]
    nr = SEQ_PER_STEP * t_len
    lp = (n_pages + 1) * PAGE
    kvd = KV_HEADS * HEAD_DIM
    bias = jnp.concatenate([bias_ref[...]] * GROUP, axis=0)
    pad = jnp.zeros((PAGE - nr, kvd), BF16)
    kc_scr[n_pages * PAGE:lp, :] = jnp.concatenate([kn_ref[...], pad], axis=0)
    vc_scr[n_pages * PAGE:lp, :] = jnp.concatenate([vn_ref[...], pad], axis=0)
    for j in range(SEQ_PER_STEP):
        for p in range(n_pages):
            for g in range(KV_HEADS):
                head_rows = pl.ds(g, PAGE, stride=KV_HEADS)
                cols = slice(g * HEAD_DIM, (g + 1) * HEAD_DIM)
                kc_scr[p * PAGE:(p + 1) * PAGE, cols] = kpages[j * n_pages + p][head_rows, :].astype(BF16)
                vc_scr[p * PAGE:(p + 1) * PAGE, cols] = vpages[j * n_pages + p][head_rows, :].astype(BF16)
        for g in range(KV_HEADS):
            qs = jnp.concatenate(
                [q_ref[:, (g * GROUP + a) * HEAD_DIM:(g * GROUP + a + 1) * HEAD_DIM] for a in range(GROUP)],
                axis=0)
            s = _dot_nt(qs, kc_scr[:, g * HEAD_DIM:(g + 1) * HEAD_DIM]) + bias
            m = jnp.max(s, axis=1, keepdims=True)
            p_ = jnp.exp(s - m)
            l = jnp.sum(p_, axis=1, keepdims=True)
            o = _dot(p_.astype(BF16), vc_scr[:, g * HEAD_DIM:(g + 1) * HEAD_DIM]) / l
            for a in range(GROUP):
                hh = g * GROUP + a
                r0 = a * nr + j * t_len
                o_ref[j * t_len:(j + 1) * t_len, hh * HEAD_DIM:(hh + 1) * HEAD_DIM] = (
                    o[r0:r0 + t_len].astype(BF16))


def _sample_attend(page_table, q_bf, bias2d, k_bf, v_bf, cache_k2, cache_v2, depth, layer, rows_p, t_len):
    db, n_pages = page_table.shape
    nr = SEQ_PER_STEP * t_len
    base = rows_p // nr
    lp = (n_pages + 1) * PAGE
    kvd = KV_HEADS * HEAD_DIM
    in_specs = [
        pl.BlockSpec((nr, ATTN_HEADS * HEAD_DIM), lambda n, pt: (base + n, 0)),
        pl.BlockSpec((nr, lp), lambda n, pt: (n, 0)),
        pl.BlockSpec((nr, kvd), lambda n, pt: (base + n, 0)),
        pl.BlockSpec((nr, kvd), lambda n, pt: (base + n, 0)),
    ]
    for _ in range(2):
        for j in range(SEQ_PER_STEP):
            for p in range(n_pages):
                in_specs.append(pl.BlockSpec(
                    (PAGE * KV_HEADS, HEAD_DIM),
                    lambda n, pt, j=j, p=p: (pt[n * SEQ_PER_STEP + j, p] * depth + layer, 0)))
    kern = functools.partial(_sample_attend_kernel, n_pages=n_pages, t_len=t_len)
    npg = SEQ_PER_STEP * n_pages
    return pl.pallas_call(
        kern,
        out_shape=jax.ShapeDtypeStruct((db * t_len, ATTN_HEADS * HEAD_DIM), BF16),
        grid_spec=pltpu.PrefetchScalarGridSpec(
            num_scalar_prefetch=1,
            grid=(db // SEQ_PER_STEP,),
            in_specs=in_specs,
            out_specs=pl.BlockSpec((nr, ATTN_HEADS * HEAD_DIM), lambda n, pt: (n, 0)),
            scratch_shapes=[pltpu.VMEM((lp, kvd), BF16), pltpu.VMEM((lp, kvd), BF16)],
        ),
        compiler_params=_cparams(("arbitrary",)),
        name="dsa_sample_attend",
    )(page_table, q_bf, bias2d, k_bf, v_bf, *([cache_k2] * npg), *([cache_v2] * npg))


def _chunk_masks(n, chunk):
    ri = lax.broadcasted_iota(I32, (n, n), 0)
    ci = lax.broadcasted_iota(I32, (n, n), 1)

    def same(size):
        sh = size.bit_length() - 1
        return (ri >> sh) == (ci >> sh)

    same_c = same(chunk)
    incl = same_c & (ci <= ri)
    strict = same_c & (ci < ri)
    base = min(SUBLANES, chunk)
    levels = []
    s = base
    while s < chunk:
        levels.append(same(2 * s) & jnp.logical_not(same(s)))
        s *= 2
    return incl, strict, same(base), levels, (ri == ci)


def _unit_lower_inverse(a_heads, same_base, levels, eye):
    ident = jnp.where(eye, 1.0, 0.0)
    ad = [jnp.where(same_base, a, 0.0) for a in a_heads]
    a2 = [_mm3(x, x) for x in ad]
    a4 = [_mm3(x, x) for x in a2]
    t = [_mm3(ident - x, ident + y) for x, y in zip(ad, a2)]
    t = [_mm3(x, ident + y) for x, y in zip(t, a4)]
    for lv in levels:
        to = [_mm3(x, jnp.where(lv, a, 0.0)) for x, a in zip(t, a_heads)]
        t = [x - _mm3(y, x) for x, y in zip(t, to)]
    return t


def _mm1(a, b, dot=_dot):
    return dot(a.astype(BF16), b.astype(BF16))


def _gdn_intra(q, k, v, beta_c, gc_c, gc_r, masks):
    incl, strict, same_base, levels, eye = masks
    heads = range(len(q))
    decay = [jnp.exp(jnp.where(incl, gc_c[h] - gc_r[h], -jnp.inf)) for h in heads]
    kb = [k[h] * beta_c[h] for h in heads]
    a = [jnp.where(strict, _mm1(kb[h], k[h], _dot_nt) * decay[h], 0.0) for h in heads]
    t = _unit_lower_inverse(a, same_base, levels, eye)
    uw = [_mm3(t[h], jnp.concatenate([v[h] * beta_c[h], kb[h] * jnp.exp(gc_c[h])], axis=1)) for h in heads]
    qk = [_mm1(q[h], k[h], _dot_nt) * decay[h] for h in heads]
    return [x[:, :HEAD_DIM] for x in uw], [x[:, HEAD_DIM:] for x in uw], qk


def _l2(x):
    return x * lax.rsqrt(jnp.sum(x * x, axis=-1, keepdims=True) + EPS)


def _cum_matrices(n, chunk):
    ri = lax.broadcasted_iota(I32, (n, n), 0)
    ci = lax.broadcasted_iota(I32, (n, n), 1)
    sh = chunk.bit_length() - 1
    same = (ri >> sh) == (ci >> sh)
    lower = jnp.where(same & (ci <= ri), 1.0, 0.0).astype(BF16)
    upper = jnp.where(same & (ri <= ci), 1.0, 0.0).astype(BF16)
    return lower, upper


def _gdn_gates(sm, smt, alog_l, dtb_l, alog_c, dtb_c, chunk):
    n = sm.shape[0]
    lower, upper = _cum_matrices(n, chunk)
    g_tile = -jnp.exp(alog_l) * _softplus(sm + dtb_l)
    beta_tile = jax.nn.sigmoid(sm)
    gc_cols = _mm_exact_lhs(lower, g_tile)
    g_rows = -jnp.exp(alog_c) * _softplus(smt + dtb_c)
    gc_rows = _mm_exact_rhs(g_rows, upper)
    return beta_tile, gc_cols, gc_rows


def _gdn_prompt_kernel(gq_ref, gz_ref, sm_ref, smt_ref, cw_ref, alog_l, dtb_l, alog_c, dtb_c, ng_ref,
                       o_ref, s_o_ref, stage, s_scr, *, tt, chunk):
    t_idx = pl.program_id(1)
    hd = HEAD_DIM
    nh = GDN_HEADS

    @pl.when(t_idx == 0)
    def _():
        stage[0:SUBLANES, :] = jnp.zeros((SUBLANES, stage.shape[1]), F32)
        s_scr[...] = jnp.zeros(s_scr.shape, F32)

    x = gq_ref[...]
    stage[SUBLANES:SUBLANES + tt, :] = x
    y = None
    for j in range(CONV_W):
        term = stage[pl.ds(SUBLANES - (CONV_W - 1) + j, tt), :] * cw_ref[j:j + 1, :]
        y = term if y is None else y + term
    stage[0:SUBLANES, :] = x[tt - SUBLANES:tt, :]
    y = _silu(y)

    beta_tile, gc_cols, gc_rows = _gdn_gates(sm_ref[...], smt_ref[...], alog_l[...], dtb_l[...],
                                             alog_c[...], dtb_c[...], chunk)
    masks = _chunk_masks(tt, chunk)
    ng = ng_ref[...]
    heads = range(nh)
    q = [_l2(y[:, h * hd:(h + 1) * hd]) * (hd ** -0.5) for h in heads]
    k = [_l2(y[:, (nh + h) * hd:(nh + h + 1) * hd]) for h in heads]
    v = [y[:, (2 * nh + h) * hd:(2 * nh + h + 1) * hd] for h in heads]
    beta_c = [beta_tile[:, GB_LANE + h:GB_LANE + h + 1] for h in heads]
    gc_c = [gc_cols[:, GA_LANE + h:GA_LANE + h + 1] for h in heads]
    gc_r = [gc_rows[h:h + 1, :] for h in heads]
    u, w, qk = _gdn_intra(q, k, v, beta_c, gc_c, gc_r, masks)
    qg = [q[h] * jnp.exp(gc_c[h]) for h in heads]
    s = [s_scr[h] for h in heads]
    vnew = [[] for _ in heads]
    ointer = [[] for _ in heads]
    for c in range(tt // chunk):
        r = slice(c * chunk, (c + 1) * chunk)
        for h in heads:
            vn = u[h][r] - _mm1(w[h][r], s[h])
            ointer[h].append(_mm1(qg[h][r], s[h]))
            g_last = gc_c[h][(c + 1) * chunk - 1:(c + 1) * chunk, :]
            kdec = k[h][r] * jnp.exp(g_last - gc_c[h][r])
            s[h] = s[h] * jnp.exp(g_last) + _mm1(kdec, vn, _dot_tn)
            vnew[h].append(vn)
    for h in heads:
        s_scr[h] = s[h]
        o = jnp.concatenate(ointer[h], axis=0) + _mm1(qk[h], jnp.concatenate(vnew[h], axis=0))
        o = _rms(o) * ng * _silu(gz_ref[:, h * hd:(h + 1) * hd])
        o_ref[:, h * hd:(h + 1) * hd] = o.astype(BF16)

    @pl.when(t_idx == pl.num_programs(1) - 1)
    def _():
        s_o_ref[...] = s_scr[...]


def _gdn_prompt(gq, gz, small, small_t, conv_w_l, alog_l, dtb_l, alog_c, dtb_c, ng, batch, seq):
    tt = min(256, seq)
    chunk = min(GDN_CHUNK, seq)
    nt = seq // tt
    gdim = gq.shape[1]
    zdim = gz.shape[1]
    row = lambda b, t: (b * nt + t, 0)
    const = lambda b, t: (0, 0)
    kern = functools.partial(_gdn_prompt_kernel, tt=tt, chunk=chunk)
    return pl.pallas_call(
        kern,
        out_shape=[jax.ShapeDtypeStruct((batch * seq, zdim), BF16),
                   jax.ShapeDtypeStruct((batch, GDN_HEADS, HEAD_DIM, HEAD_DIM), F32)],
        grid=(batch, nt),
        in_specs=[
            pl.BlockSpec((tt, gdim), row),
            pl.BlockSpec((tt, zdim), row),
            pl.BlockSpec((tt, LANES), row),
            pl.BlockSpec((SUBLANES, tt), lambda b, t: (0, b * nt + t)),
            pl.BlockSpec((CONV_W, gdim), const),
            pl.BlockSpec((1, LANES), const),
            pl.BlockSpec((1, LANES), const),
            pl.BlockSpec((SUBLANES, 1), const),
            pl.BlockSpec((SUBLANES, 1), const),
            pl.BlockSpec((1, HEAD_DIM), const),
        ],
        out_specs=[pl.BlockSpec((tt, zdim), row),
                   pl.BlockSpec((None, GDN_HEADS, HEAD_DIM, HEAD_DIM), lambda b, t: (b, 0, 0, 0))],
        scratch_shapes=[pltpu.VMEM((tt + SUBLANES, gdim), F32),
                        pltpu.VMEM((GDN_HEADS, HEAD_DIM, HEAD_DIM), F32)],
        compiler_params=_cparams(("arbitrary", "arbitrary")),
        name="gdn_prompt",
    )(gq, gz, small, small_t, conv_w_l, alog_l, dtb_l, alog_c, dtb_c, ng)


def _gdn_sample_kernel(gq_ref, gz_ref, sm_ref, smt_ref, cst_ref, s0_ref, cw_ref, alog_l, dtb_l, alog_c,
                       dtb_c, ng_ref, o_ref, s_o_ref, stage, uw_scr, vn_scr, oi_scr, *, nb, t_len):
    hd = HEAD_DIM
    nh = GDN_HEADS
    n = nb * t_len
    gdim = gq_ref.shape[1]
    stage[:, 0:SUBLANES, :] = cst_ref[...].reshape(nb, SUBLANES, gdim)
    stage[:, SUBLANES:SUBLANES + t_len, :] = gq_ref[...].reshape(nb, t_len, gdim)
    y = None
    for j in range(CONV_W):
        term = stage[:, pl.ds(SUBLANES - (CONV_W - 1) + j, t_len), :] * cw_ref[j:j + 1, :]
        y = term if y is None else y + term
    y = _silu(y).reshape(n, gdim)

    beta_tile, gc_cols, gc_rows = _gdn_gates(sm_ref[...], smt_ref[...], alog_l[...], dtb_l[...],
                                             alog_c[...], dtb_c[...], t_len)
    masks = _chunk_masks(n, t_len)
    ng = ng_ref[...]
    ri = lax.broadcasted_iota(I32, (n, n), 0)
    ci = lax.broadcasted_iota(I32, (n, n), 1)
    sh = t_len.bit_length() - 1
    pick_last = jnp.where(((ri >> sh) == (ci >> sh)) & ((ci & (t_len - 1)) == t_len - 1), 1.0, 0.0).astype(BF16)
    g_last_cols = _mm_exact_lhs(pick_last, gc_cols)
    heads = range(nh)
    q = [_l2(y[:, h * hd:(h + 1) * hd]) * (hd ** -0.5) for h in heads]
    k = [_l2(y[:, (nh + h) * hd:(nh + h + 1) * hd]) for h in heads]
    v = [y[:, (2 * nh + h) * hd:(2 * nh + h + 1) * hd] for h in heads]
    beta_c = [beta_tile[:, GB_LANE + h:GB_LANE + h + 1] for h in heads]
    gc_c = [gc_cols[:, GA_LANE + h:GA_LANE + h + 1] for h in heads]
    gc_r = [gc_rows[h:h + 1, :] for h in heads]
    u, w, qk = _gdn_intra(q, k, v, beta_c, gc_c, gc_r, masks)
    for h in heads:
        g_last_c = g_last_cols[:, GA_LANE + h:GA_LANE + h + 1]
        uw_scr[h, 0] = u[h]
        uw_scr[h, 1] = w[h]
        uw_scr[h, 2] = q[h] * jnp.exp(gc_c[h])
        uw_scr[h, 3] = k[h] * jnp.exp(g_last_c - gc_c[h])
        uw_scr[h, 4] = jnp.broadcast_to(jnp.exp(g_last_c), (n, hd))

    def seq_step(i, carry):
        r0 = pl.multiple_of(i * t_len, t_len)
        rows = pl.ds(r0, t_len)
        for h in heads:
            s = s0_ref[i, h]
            vn = uw_scr[h, 0, rows, :] - _mm1(uw_scr[h, 1, rows, :], s)
            oi_scr[h, rows, :] = _mm1(uw_scr[h, 2, rows, :], s)
            vn_scr[h, rows, :] = vn
            dec = uw_scr[h, 4, pl.ds(r0, 1), :]
            s_o_ref[i, h] = s * dec + _mm1(uw_scr[h, 3, rows, :], vn, _dot_tn)
        return carry

    lax.fori_loop(0, nb, seq_step, 0)
    for h in heads:
        o = oi_scr[h] + _mm1(qk[h], vn_scr[h])
        o = _rms(o) * ng * _silu(gz_ref[:, h * hd:(h + 1) * hd])
        o_ref[:, h * hd:(h + 1) * hd] = o.astype(BF16)


def _gdn_sample(gq, gz, small, small_t, cstate, state_gdn, layer, conv_w_l, alog_l, dtb_l, alog_c, dtb_c, ng,
                rows_p, db, t_len):
    nb = min(16, db)
    n = nb * t_len
    base = rows_p // n
    gdim = gq.shape[1]
    zdim = gz.shape[1]
    row = lambda i: (base + i, 0)
    const = lambda i: (0, 0)
    kern = functools.partial(_gdn_sample_kernel, nb=nb, t_len=t_len)
    return pl.pallas_call(
        kern,
        out_shape=[jax.ShapeDtypeStruct((db * t_len, zdim), BF16),
                   jax.ShapeDtypeStruct((db, GDN_HEADS, HEAD_DIM, HEAD_DIM), F32)],
        grid=(db // nb,),
        in_specs=[
            pl.BlockSpec((n, gdim), row),
            pl.BlockSpec((n, zdim), row),
            pl.BlockSpec((n, LANES), row),
            pl.BlockSpec((SUBLANES, n), lambda i: (0, base + i)),
            pl.BlockSpec((n, gdim), lambda i: (i, 0)),
            pl.BlockSpec((nb, None, GDN_HEADS, HEAD_DIM, HEAD_DIM), lambda i: (i, layer, 0, 0, 0)),
            pl.BlockSpec((CONV_W, gdim), const),
            pl.BlockSpec((1, LANES), const),
            pl.BlockSpec((1, LANES), const),
            pl.BlockSpec((SUBLANES, 1), const),
            pl.BlockSpec((SUBLANES, 1), const),
            pl.BlockSpec((1, HEAD_DIM), const),
        ],
        out_specs=[pl.BlockSpec((n, zdim), lambda i: (i, 0)),
                   pl.BlockSpec((nb, GDN_HEADS, HEAD_DIM, HEAD_DIM), lambda i: (i, 0, 0, 0))],
        scratch_shapes=[pltpu.VMEM((nb, 2 * SUBLANES, gdim), F32),
                        pltpu.VMEM((GDN_HEADS, 5, n, HEAD_DIM), F32),
                        pltpu.VMEM((GDN_HEADS, n, HEAD_DIM), F32),
                        pltpu.VMEM((GDN_HEADS, n, HEAD_DIM), F32)],
        compiler_params=_cparams(("arbitrary",)),
        name="gdn_sample",
    )(gq, gz, small, small_t, cstate, state_gdn, conv_w_l, alog_l, dtb_l, alog_c, dtb_c, ng)


def _outproj_kernel(x_ref, att_ref, gdn_ref, gate_ref, sh_ref, sc_ref, g_ref, wo_ref, *rest, moe):
    if moe:
        rw_ref, rb_ref, x_o, h_o, lg_o = rest
    else:
        x_o, h_o = rest
    x = x_ref[...]
    adim = att_ref.shape[1]
    y = _dot(att_ref[...], wo_ref[0:adim, :]) + _dot(gdn_ref[...], wo_ref[adim:, :])
    xn = x + gate_ref[...] * y.reshape(x.shape)
    x_o[...] = xn
    h = (_rms(xn) * g_ref[...]) * (1.0 + sc_ref[...]) + sh_ref[...]
    h2 = h.reshape(y.shape)
    h_o[...] = h2.astype(BF16)
    if moe:
        lg_o[...] = _mm3(h2, rw_ref[...]) + rb_ref[...]


def _outproj(x3, att, gdn, gate, sh, sc, g, wo, router, tile_of, tm):
    ng, _, d = x3.shape
    rows = ng * SUBLANES
    gt = tm // SUBLANES
    moe = router is not None
    const = lambda t: (0, 0)
    row = lambda t: (t, 0)
    modspec = pl.BlockSpec((gt, 1, d), lambda t: (tile_of(t), 0, 0))
    in_specs = [
        pl.BlockSpec((gt, SUBLANES, d), lambda t: (t, 0, 0)),
        pl.BlockSpec((tm, att.shape[1]), row),
        pl.BlockSpec((tm, gdn.shape[1]), row),
        modspec, modspec, modspec,
        pl.BlockSpec((1, d), const),
        pl.BlockSpec(wo.shape, const),
    ]
    out_shape = [jax.ShapeDtypeStruct(x3.shape, F32), jax.ShapeDtypeStruct((rows, d), BF16)]
    out_specs = [pl.BlockSpec((gt, SUBLANES, d), lambda t: (t, 0, 0)), pl.BlockSpec((tm, d), row)]
    args = [x3, att, gdn, gate, sh, sc, g, wo]
    if moe:
        in_specs += [pl.BlockSpec((d, LANES), const), pl.BlockSpec((1, LANES), const)]
        out_shape.append(jax.ShapeDtypeStruct((rows, LANES), F32))
        out_specs.append(pl.BlockSpec((tm, LANES), row))
        args += list(router)
    return pl.pallas_call(
        functools.partial(_outproj_kernel, moe=moe),
        out_shape=out_shape,
        grid=(rows // tm,),
        in_specs=in_specs,
        out_specs=out_specs,
        compiler_params=_cparams(("arbitrary",)),
        name="out_proj",
    )(*args)


def _ffn_kernel(x_ref, h_ref, gate_ref, wg_ref, wu_ref, wd_ref, o_ref, *, fc):
    h = h_ref[...]
    dff = wg_ref.shape[1]
    acc = None
    for c in range(dff // fc):
        a = _dot(h, wg_ref[:, c * fc:(c + 1) * fc])
        u = _dot(h, wu_ref[:, c * fc:(c + 1) * fc])
        t = _dot((_silu(a) * u).astype(BF16), wd_ref[c * fc:(c + 1) * fc, :])
        acc = t if acc is None else acc + t
    x = x_ref[...]
    o_ref[...] = x + gate_ref[...] * acc.reshape(x.shape)


def _ffn(x3, hff, gate, wg, wu, wd, tile_of, tm):
    ng, _, d = x3.shape
    rows = ng * SUBLANES
    gt = tm // SUBLANES
    dff = wg.shape[1]
    fc = dff
    for cand in (1408, 1024, 768, 512, 256, 128):
        if dff % cand == 0:
            fc = cand
            break
    const = lambda t: (0, 0)
    return pl.pallas_call(
        functools.partial(_ffn_kernel, fc=fc),
        out_shape=jax.ShapeDtypeStruct(x3.shape, F32),
        grid=(rows // tm,),
        in_specs=[
            pl.BlockSpec((gt, SUBLANES, d), lambda t: (t, 0, 0)),
            pl.BlockSpec((tm, d), lambda t: (t, 0)),
            pl.BlockSpec((gt, 1, d), lambda t: (tile_of(t), 0, 0)),
            pl.BlockSpec(wg.shape, const),
            pl.BlockSpec(wu.shape, const),
            pl.BlockSpec(wd.shape, const),
        ],
        out_specs=pl.BlockSpec((gt, SUBLANES, d), lambda t: (t, 0, 0)),
        compiler_params=_cparams(("arbitrary",)),
        name="ffn_dense",
    )(x3, hff, gate, wg, wu, wd)


def _moe_kernel(x_ref, h_ref, lg_ref, gate_ref, wg_ref, wu_ref, wd_ref, o_ref, acc_scr, g_scr):
    e = pl.program_id(1)
    tm = h_ref.shape[0]

    @pl.when(e == 0)
    def _():
        lane = lax.broadcasted_iota(I32, (tm, LANES), 1)
        lg = jnp.where(lane < N_EXPERTS, lg_ref[...], -jnp.inf)
        m1 = jnp.max(lg, axis=1, keepdims=True)
        i1 = jnp.min(jnp.where(lg == m1, lane, LANES), axis=1, keepdims=True)
        rest = jnp.where(lane == i1, -jnp.inf, lg)
        m2 = jnp.max(rest, axis=1, keepdims=True)
        i2 = jnp.min(jnp.where(rest == m2, lane, LANES), axis=1, keepdims=True)
        e2 = jnp.exp(m2 - m1)
        den = 1.0 + e2
        g_scr[...] = jnp.where(lane == i1, 1.0 / den, 0.0) + jnp.where(lane == i2, e2 / den, 0.0)
        acc_scr[...] = jnp.zeros(acc_scr.shape, F32)

    h = h_ref[...]
    a = _dot(h, wg_ref[...])
    u = _dot(h, wu_ref[...])
    y = _dot((_silu(a) * u).astype(BF16), wd_ref[...])
    lane = lax.broadcasted_iota(I32, (tm, LANES), 1)
    ge = jnp.sum(jnp.where(lane == e, g_scr[...], 0.0), axis=1, keepdims=True)
    acc_scr[...] += ge * y

    @pl.when(e == pl.num_programs(1) - 1)
    def _():
        x = x_ref[...]
        o_ref[...] = x + gate_ref[...] * acc_scr[...].reshape(x.shape)


def _moe(x3, hff, logits, gate, wg, wu, wd, tile_of, tm):
    ng, _, d = x3.shape
    rows = ng * SUBLANES
    gt = tm // SUBLANES
    ne, _, eff = wg.shape
    return pl.pallas_call(
        _moe_kernel,
        out_shape=jax.ShapeDtypeStruct(x3.shape, F32),
        grid=(rows // tm, ne),
        in_specs=[
            pl.BlockSpec((gt, SUBLANES, d), lambda t, e: (t, 0, 0)),
            pl.BlockSpec((tm, d), lambda t, e: (t, 0)),
            pl.BlockSpec((tm, LANES), lambda t, e: (t, 0)),
            pl.BlockSpec((gt, 1, d), lambda t, e: (tile_of(t), 0, 0)),
            pl.BlockSpec((None, d, eff), lambda t, e: (e, 0, 0)),
            pl.BlockSpec((None, d, eff), lambda t, e: (e, 0, 0)),
            pl.BlockSpec((None, eff, d), lambda t, e: (e, 0, 0)),
        ],
        out_specs=pl.BlockSpec((gt, SUBLANES, d), lambda t, e: (t, 0, 0)),
        scratch_shapes=[pltpu.VMEM((tm, d), F32), pltpu.VMEM((tm, LANES), F32)],
        compiler_params=_cparams(("arbitrary", "arbitrary")),
        name="ffn_moe",
    )(x3, hff, logits, gate, wg, wu, wd)


def _rope_tables(pos):
    pos = pos.astype(F32)[:, None]
    half = HEAD_DIM // 2
    ang = pos * (ROPE_THETA ** (-jnp.arange(half, dtype=F32) / half))[None, :]
    c, s = jnp.cos(ang), jnp.sin(ang)
    cq = jnp.concatenate([c, c], axis=1)
    sq = jnp.concatenate([-s, s], axis=1)
    half = IDX_DIM // 2
    ang = pos * (ROPE_THETA ** (-jnp.arange(half, dtype=F32) / half))[None, :]
    c, s = jnp.cos(ang), jnp.sin(ang)
    z = jnp.zeros_like(s)
    ci = jnp.concatenate([c, c, c, c], axis=1)
    sa = jnp.concatenate([-s, z, -s, z], axis=1)
    sb = jnp.concatenate([z, s, z, s], axis=1)
    return cq, sq, ci, sa, sb


def _lane_vec(vals, lane0):
    return jnp.zeros((1, LANES), F32).at[0, lane0:lane0 + vals.shape[0]].set(vals)


def kernel(x_prompt, x_sample, cache_k, cache_v, cache_kidx, state_gdn, state_conv, page_table, c_prompt, c_sample, mod_w, mod_b, norm_mix_g, norm_ffn_g, w_in, q_norm_g, k_norm_g, conv_w, a_log, dt_bias, gdn_norm_g, w_out, ffn_w_gate, ffn_w_up, ffn_w_down, router_w, router_b, moe_w_gate, moe_w_up, moe_w_down):
    batch, seq, d = x_prompt.shape
    db, t_len, _ = x_sample.shape
    n_layers = mod_w.shape[0]
    n_pages = page_table.shape[1]
    past = n_pages * PAGE
    rows_p, rows_s = batch * seq, db * t_len
    rows = rows_p + rows_s
    assert t_len == SUBLANES and seq % LANES == 0
    tm = min(256, rows_s, seq)
    assert seq % tm == 0 and rows_s % tm == 0
    tiles_per_batch = seq // tm
    n_ptiles = rows_p // tm
    tm_moe = 2 * tm if (seq % (2 * tm) == 0 and rows_s % (2 * tm) == 0) else tm

    def tile_of_for(tile_rows):
        per_batch, n_prompt = seq // tile_rows, rows_p // tile_rows
        return lambda t: jnp.where(t < n_prompt, t // per_batch, batch + (t - n_prompt))

    tile_of = tile_of_for(tm)

    def tab_of(t):
        return jnp.where(t < n_ptiles, t % tiles_per_batch, tiles_per_batch)

    def groups(m, tile_rows=tm):
        mp = jnp.repeat(m[:batch], tile_rows // SUBLANES, axis=0)
        return jnp.concatenate([mp, m[batch:batch + db]], axis=0)[:, None, :]

    x3 = jnp.concatenate([x_prompt.reshape(rows_p, d), x_sample.reshape(rows_s, d)], axis=0)
    x3 = x3.reshape(rows // SUBLANES, SUBLANES, d)

    n_c = batch + db
    c_all = jnp.concatenate([c_prompt, c_sample, jnp.zeros((-n_c % SUBLANES, d), F32)], axis=0)
    mods = _mods(c_all, mod_w, mod_b)

    pos = jnp.concatenate([jnp.arange(seq), past + (jnp.arange(tm) % t_len)])
    tabs = _rope_tables(pos)

    kvd = KV_HEADS * HEAD_DIM
    depth = cache_k.shape[1]
    cache_k2 = cache_k.reshape(-1, HEAD_DIM)
    cache_v2 = cache_v.reshape(-1, HEAD_DIM)
    cache_kidx_t = jnp.swapaxes(cache_kidx, 2, 3)
    n_sel_s = min(TOPK_MAX, (past + t_len) // 4)
    lp = (n_pages + 1) * PAGE

    outs = {name: [] for name in ("kp", "vp", "kip", "ks", "vs", "kis", "sp", "ss", "cp", "cs")}
    col = 0
    offs = []
    for size in (ATTN_HEADS * HEAD_DIM, kvd, kvd, IDX_HEADS * IDX_DIM, IDX_DIM, IDX_HEADS,
                 GDN_HEADS * 3 * HEAD_DIM, GDN_HEADS * HEAD_DIM, GDN_HEADS, GDN_HEADS):
        offs.append((col, col + size))
        col += size
    o_q, o_k, o_v, o_qi, o_ki, o_wi, o_gq, o_gz, o_ga, o_gb = offs

    for l in range(n_layers):
        m6 = [groups(mods[l, :, j * d:(j + 1) * d]) for j in range(6)]
        w = w_in[l]
        wa = w[:, o_q[0]:o_v[1]].astype(BF16)
        qi_w = w[:, o_qi[0]:o_qi[1]].reshape(d, IDX_HEADS, 1, IDX_DIM)
        qi_w = jnp.broadcast_to(qi_w, (d, IDX_HEADS, 2, IDX_DIM)).reshape(d, IDX_HEADS * LANES)
        ki_w = w[:, o_ki[0]:o_ki[1]]
        misc_w = jnp.concatenate([w[:, o_wi[0]:o_wi[1]], w[:, o_ga[0]:o_ga[1]], w[:, o_gb[0]:o_gb[1]],
                                  jnp.zeros((d, LANES - IDX_HEADS - 2 * GDN_HEADS), F32)], axis=1)
        wi_f = jnp.concatenate([qi_w, ki_w, ki_w, misc_w], axis=1)
        wih = wi_f.astype(BF16)
        wil = (wi_f - wih.astype(F32)).astype(BF16)
        wg = w[:, o_gq[0]:o_gz[1]].astype(BF16)

        (q_bf, k_f, v_f, k_bf, v_bf, qi3, ki_f, ki3, small, gq, gz) = _inproj(
            x3, m6[0], m6[1], norm_mix_g[l][None, :], wa, wih, wil, wg,
            q_norm_g[l][None, :], k_norm_g[l][None, :], tabs, tile_of, tab_of, tm)

        small_t = jnp.transpose(small[:, GA_LANE:GA_LANE + 2 * GDN_HEADS])
        alog_l = _lane_vec(a_log[l], GA_LANE)
        dtb_l = _lane_vec(dt_bias[l], GA_LANE)
        pad4 = jnp.zeros((SUBLANES - GDN_HEADS,), F32)
        alog_c = jnp.concatenate([a_log[l], pad4])[:, None]
        dtb_c = jnp.concatenate([dt_bias[l], pad4])[:, None]
        ng = gdn_norm_g[l][None, :]

        att_p = _dsa_prompt(qi3, small, q_bf, ki3, k_bf, v_bf, batch, seq)
        gdn_p, s_p = _gdn_prompt(gq, gz, small, small_t, conv_w[l], alog_l, dtb_l, alog_c, dtb_c, ng,
                                 batch, seq)

        scores = _sample_scores(page_table, qi3, small, ki3, cache_kidx_t, l, rows_p, t_len)
        bias = _sample_select(scores.reshape(rows_s, lp), n_sel_s)
        att_s = _sample_attend(page_table, q_bf, bias, k_bf, v_bf, cache_k2, cache_v2, depth, l, rows_p, t_len)
        cstate = jnp.pad(state_conv[:, l], ((0, 0), (SUBLANES - (CONV_W - 1), 0), (0, 0)))
        cstate = cstate.reshape(db * SUBLANES, cstate.shape[2])
        gdn_s, s_s = _gdn_sample(gq, gz, small, small_t, cstate, state_gdn, l, conv_w[l], alog_l, dtb_l,
                                 alog_c, dtb_c, ng, rows_p, db, t_len)

        att = jnp.concatenate([att_p, att_s], axis=0)
        gdn = jnp.concatenate([gdn_p, gdn_s], axis=0)
        i = l // 2
        router = None
        if l % 2 == 1:
            rw = jnp.concatenate([router_w[i], jnp.zeros((d, LANES - N_EXPERTS), F32)], axis=1)
            rb = jnp.concatenate([router_b[i], jnp.zeros((LANES - N_EXPERTS,), F32)])[None, :]
            router = (rw, rb)
        res = _outproj(x3, att, gdn, m6[2], m6[3], m6[4], norm_ffn_g[l][None, :], w_out[l].astype(BF16),
                       router, tile_of, tm)
        if l % 2 == 0:
            x3, hff = res
            x3 = _ffn(x3, hff, m6[5], ffn_w_gate[i].astype(BF16), ffn_w_up[i].astype(BF16),
                      ffn_w_down[i].astype(BF16), tile_of, tm)
        else:
            x3, hff, logits = res
            gate_moe = groups(mods[l, :, 5 * d:6 * d], tm_moe)
            x3 = _moe(x3, hff, logits, gate_moe, moe_w_gate[i].astype(BF16), moe_w_up[i].astype(BF16),
                      moe_w_down[i].astype(BF16), tile_of_for(tm_moe), tm_moe)

        outs["kp"].append(k_f[:rows_p].reshape(batch, seq, KV_HEADS, HEAD_DIM))
        outs["vp"].append(v_f[:rows_p].reshape(batch, seq, KV_HEADS, HEAD_DIM))
        outs["kip"].append(ki_f[:rows_p].reshape(batch, seq, IDX_DIM))
        outs["ks"].append(k_f[rows_p:].reshape(db, t_len, KV_HEADS, HEAD_DIM))
        outs["vs"].append(v_f[rows_p:].reshape(db, t_len, KV_HEADS, HEAD_DIM))
        outs["kis"].append(ki_f[rows_p:].reshape(db, t_len, IDX_DIM))
        outs["sp"].append(s_p)
        outs["ss"].append(s_s)
        tail = CONV_W - 1
        outs["cp"].append(jnp.stack([gq[(b + 1) * seq - tail:(b + 1) * seq] for b in range(batch)], axis=0))
        gq_s = gq[rows_p:].reshape(db, t_len, -1)
        outs["cs"].append(gq_s[:, t_len - tail:])

    x2 = x3.reshape(rows, d)
    st = lambda name: jnp.stack(outs[name], axis=1)
    return (x2[:rows_p].reshape(batch, seq, d), x2[rows_p:].reshape(db, t_len, d),
            st("kp"), st("vp"), st("kip"), st("ks"), st("vs"), st("kis"),
            st("sp"), st("ss"), st("cp"), st("cs"))
```

```python
import functools
import math

import jax
import jax.numpy as jnp
from jax import lax
from jax.experimental import pallas as pl
from jax.experimental.pallas import tpu as pltpu

F32 = jnp.float32
BF16 = jnp.bfloat16
I32 = jnp.int32

HEAD_DIM = 128
ATTN_HEADS = 4
KV_HEADS = 2
GROUP = ATTN_HEADS // KV_HEADS
IDX_HEADS = 8
IDX_DIM = 64
TOPK_MAX = 256
GDN_HEADS = 4
GDN_CHUNK = 64
CONV_W = 4
N_EXPERTS = 8
PAGE = 128
ROPE_THETA = 10000.0
EPS = 1e-6

LANES = 128
SUBLANES = 8
VMEM_LIMIT = 56 * 1024 * 1024
NEG = -1e30
INT_MIN = -2147483648
WI_LANE = 0
GA_LANE = 8
GB_LANE = 12


def _cparams(sem):
    return pltpu.CompilerParams(dimension_semantics=sem, vmem_limit_bytes=VMEM_LIMIT)


def _dot(a, b):
    return jnp.dot(a, b, preferred_element_type=F32)


def _dot_nt(a, b):
    return lax.dot_general(a, b, (((1,), (1,)), ((), ())), preferred_element_type=F32)


def _dot_tn(a, b):
    return lax.dot_general(a, b, (((0,), (0,)), ((), ())), preferred_element_type=F32)


def _split(x):
    hi = x.astype(BF16)
    lo = (x - hi.astype(F32)).astype(BF16)
    return hi, lo


def _mm3(a, b, dot=_dot):
    ah, al = _split(a)
    bh, bl = _split(b)
    return dot(ah, bh) + (dot(ah, bl) + dot(al, bh))


def _split_three(a):
    a1 = a.astype(BF16)
    r1 = a - a1.astype(F32)
    a2 = r1.astype(BF16)
    a3 = (r1 - a2.astype(F32)).astype(BF16)
    return a1, a2, a3


def _mm_exact_rhs(a, b_bf16):
    a1, a2, a3 = _split_three(a)
    return _dot(a1, b_bf16) + (_dot(a2, b_bf16) + _dot(a3, b_bf16))


def _mm_exact_lhs(m_bf16, a):
    a1, a2, a3 = _split_three(a)
    return _dot(m_bf16, a1) + (_dot(m_bf16, a2) + _dot(m_bf16, a3))


def _silu(x):
    return x * jax.nn.sigmoid(x)


def _softplus(x):
    return jnp.maximum(x, 0.0) + jnp.log1p(jnp.exp(-jnp.abs(x)))


def _mods_kernel(c_ref, w_ref, b_ref, o_ref):
    o_ref[...] = _mm3(_silu(c_ref[...]), w_ref[...]) + b_ref[...]


def _mods(c_all, mod_w, mod_b):
    n_layers, d, n6 = mod_w.shape
    rows = c_all.shape[0]
    tn = n6 // 4
    return pl.pallas_call(
        _mods_kernel,
        out_shape=jax.ShapeDtypeStruct((n_layers, rows, n6), F32),
        grid=(n_layers, n6 // tn),
        in_specs=[
            pl.BlockSpec((rows, d), lambda l, j: (0, 0)),
            pl.BlockSpec((None, d, tn), lambda l, j: (l, 0, j)),
            pl.BlockSpec((None, 1, tn), lambda l, j: (l, 0, j)),
        ],
        out_specs=pl.BlockSpec((None, rows, tn), lambda l, j: (l, 0, j)),
        compiler_params=_cparams(("arbitrary", "arbitrary")),
        name="ada_mods",
    )(c_all, mod_w, mod_b.reshape(n_layers, 1, n6))


def _rms(x):
    return x * lax.rsqrt(jnp.mean(x * x, axis=-1, keepdims=True) + EPS)


def _inproj_kernel(x_ref, sh_ref, sc_ref, g_ref, wa_ref, wih_ref, wil_ref, wg_ref, qg_ref, kg_ref,
                   cq_ref, sq_ref, ci_ref, sa_ref, sb_ref,
                   q_o, k_o, v_o, kb_o, vb_o, qi3_o, ki_o, ki3_o, small_o, gq_o, gz_o):
    x = x_ref[...]
    h = (_rms(x) * g_ref[...]) * (1.0 + sc_ref[...]) + sh_ref[...]
    tm = x.shape[0] * x.shape[1]
    h2 = h.reshape(tm, x.shape[2])
    hb, hl = _split(h2)
    za = _dot(hb, wa_ref[...])
    wih = wih_ref[...]
    zi = _dot(hb, wih) + (_dot(hb, wil_ref[...]) + _dot(hl, wih))
    zg = _dot(hb, wg_ref[...])

    cq, sq = cq_ref[...], sq_ref[...]
    qg, kg = qg_ref[...], kg_ref[...]
    for hh in range(ATTN_HEADS):
        qn = _rms(za[:, hh * HEAD_DIM:(hh + 1) * HEAD_DIM]) * qg
        qr = qn * cq + pltpu.roll(qn, HEAD_DIM // 2, 1) * sq
        q_o[:, hh * HEAD_DIM:(hh + 1) * HEAD_DIM] = (qr * (HEAD_DIM ** -0.5)).astype(BF16)
    koff = ATTN_HEADS * HEAD_DIM
    for hh in range(KV_HEADS):
        kn = _rms(za[:, koff + hh * HEAD_DIM:koff + (hh + 1) * HEAD_DIM]) * kg
        kr = kn * cq + pltpu.roll(kn, HEAD_DIM // 2, 1) * sq
        k_o[:, hh * HEAD_DIM:(hh + 1) * HEAD_DIM] = kr
        kb_o[:, hh * HEAD_DIM:(hh + 1) * HEAD_DIM] = kr.astype(BF16)
    voff = koff + KV_HEADS * HEAD_DIM
    v = za[:, voff:voff + KV_HEADS * HEAD_DIM]
    v_o[...] = v
    vb_o[...] = v.astype(BF16)

    ci, sa, sb = ci_ref[...], sa_ref[...], sb_ref[...]
    first = lax.broadcasted_iota(I32, (tm, LANES), 1) < IDX_DIM

    def rope64(t):
        return t * ci + pltpu.roll(t, LANES - IDX_DIM // 2, 1) * sa + pltpu.roll(t, IDX_DIM // 2, 1) * sb

    for hh in range(IDX_HEADS):
        r = rope64(zi[:, hh * LANES:(hh + 1) * LANES]) * (IDX_DIM ** -0.5)
        hi = r.astype(BF16).astype(F32)
        qi3_o[hh, :, 0:LANES] = jnp.where(first, hi, r - hi).astype(BF16)
        qi3_o[hh, :, LANES:2 * LANES] = jnp.where(first, hi, 0.0).astype(BF16)
    r = rope64(zi[:, IDX_HEADS * LANES:(IDX_HEADS + 1) * LANES])
    ki_o[...] = r[:, :IDX_DIM]
    hi = r.astype(BF16).astype(F32)
    ki3_o[:, 0:LANES] = hi.astype(BF16)
    ki3_o[:, LANES:2 * LANES] = jnp.where(first, r - hi, 0.0).astype(BF16)
    misc = zi[:, (IDX_HEADS + 1) * LANES:(IDX_HEADS + 2) * LANES]
    lane = lax.broadcasted_iota(I32, (tm, LANES), 1)
    small_o[...] = jnp.where(lane < IDX_HEADS, misc * (IDX_HEADS ** -0.5), misc)
    gdim = gq_o.shape[1]
    gq_o[...] = zg[:, :gdim]
    gz_o[...] = zg[:, gdim:]


def _inproj(x3, sh, sc, g, wa, wih, wil, wg, qg, kg, tabs, tile_of, tab_of, tm):
    ng, _, d = x3.shape
    rows = ng * SUBLANES
    gt = tm // SUBLANES
    nt = rows // tm
    gdim = GDN_HEADS * 3 * HEAD_DIM
    zdim = GDN_HEADS * HEAD_DIM
    const = lambda t: (0, 0)
    row = lambda t: (t, 0)
    tab = lambda t: (tab_of(t), 0)
    in_specs = [
        pl.BlockSpec((gt, SUBLANES, d), lambda t: (t, 0, 0)),
        pl.BlockSpec((gt, 1, d), lambda t: (tile_of(t), 0, 0)),
        pl.BlockSpec((gt, 1, d), lambda t: (tile_of(t), 0, 0)),
        pl.BlockSpec((1, d), const),
        pl.BlockSpec(wa.shape, const),
        pl.BlockSpec(wih.shape, const),
        pl.BlockSpec(wil.shape, const),
        pl.BlockSpec(wg.shape, const),
        pl.BlockSpec((1, HEAD_DIM), const),
        pl.BlockSpec((1, HEAD_DIM), const),
    ] + [pl.BlockSpec((tm, LANES), tab)] * 5
    kvd = KV_HEADS * HEAD_DIM
    out_shape = [
        jax.ShapeDtypeStruct((rows, ATTN_HEADS * HEAD_DIM), BF16),
        jax.ShapeDtypeStruct((rows, kvd), F32),
        jax.ShapeDtypeStruct((rows, kvd), F32),
        jax.ShapeDtypeStruct((rows, kvd), BF16),
        jax.ShapeDtypeStruct((rows, kvd), BF16),
        jax.ShapeDtypeStruct((IDX_HEADS, rows, 2 * LANES), BF16),
        jax.ShapeDtypeStruct((rows, IDX_DIM), F32),
        jax.ShapeDtypeStruct((rows, 2 * LANES), BF16),
        jax.ShapeDtypeStruct((rows, LANES), F32),
        jax.ShapeDtypeStruct((rows, gdim), F32),
        jax.ShapeDtypeStruct((rows, zdim), F32),
    ]
    out_specs = [
        pl.BlockSpec((tm, ATTN_HEADS * HEAD_DIM), row),
        pl.BlockSpec((tm, kvd), row),
        pl.BlockSpec((tm, kvd), row),
        pl.BlockSpec((tm, kvd), row),
        pl.BlockSpec((tm, kvd), row),
        pl.BlockSpec((IDX_HEADS, tm, 2 * LANES), lambda t: (0, t, 0)),
        pl.BlockSpec((tm, IDX_DIM), row),
        pl.BlockSpec((tm, 2 * LANES), row),
        pl.BlockSpec((tm, LANES), row),
        pl.BlockSpec((tm, gdim), row),
        pl.BlockSpec((tm, zdim), row),
    ]
    return pl.pallas_call(
        _inproj_kernel,
        out_shape=out_shape,
        grid=(nt,),
        in_specs=in_specs,
        out_specs=out_specs,
        compiler_params=_cparams(("arbitrary",)),
        name="in_proj",
    )(x3, sh, sc, g, wa, wih, wil, wg, qg, kg, *tabs)


def _sort_key(score):
    bits = pltpu.bitcast(score, I32)
    return jnp.where(bits < 0, bits ^ jnp.int32(0x7FFFFFFF), bits)


def _key_score(key):
    return pltpu.bitcast(jnp.where(key < 0, key ^ jnp.int32(0x7FFFFFFF), key), F32)


FLT_BIG = 3.0e38


def _count(key_scr, nkc, kc, rows, n, preds):
    def body(c, accs):
        base = pl.multiple_of(c * kc, kc)
        accs = list(accs)
        for j in range(kc // LANES):
            sc = key_scr[:, pl.ds(base + j * LANES, LANES)]
            ps = preds(sc, base + j * LANES)
            for i in range(n):
                accs[i] = accs[i] + jnp.where(ps[i], 1.0, 0.0)
        return tuple(accs)

    accs = lax.fori_loop(0, nkc, body, tuple(jnp.zeros((rows, LANES), F32) for _ in range(n)))
    return [jnp.sum(a, axis=1, keepdims=True) for a in accs]


def _select_threshold(key_scr, nkc, kc, rows, n_sel, idx_bits):
    n_sel_f = float(n_sel)

    def bit_step(b, acc):
        cand = acc | (jnp.int32(1) << (31 - b))
        cand_s = jnp.broadcast_to(_key_score(cand ^ jnp.int32(INT_MIN)), (rows, LANES))
        cnt, = _count(key_scr, nkc, kc, rows, 1, lambda kk, c0: (kk >= cand_s,))
        return jnp.where(cnt >= n_sel_f, cand, acc)

    acc = lax.fori_loop(0, 32, bit_step, jnp.zeros((rows, 1), I32))
    thr = jnp.where(acc != 0, _key_score(acc ^ jnp.int32(INT_MIN)), -FLT_BIG)
    thr_b = jnp.broadcast_to(thr, (rows, LANES))
    cnt_gt, cnt_eq = _count(key_scr, nkc, kc, rows, 2, lambda kk, c0: (kk > thr_b, kk == thr_b))
    need = n_sel_f - cnt_gt
    excess = (acc != 0) & (cnt_eq > need)
    any_excess = jnp.max(jnp.where(excess, 1.0, 0.0)) > 0.0
    lane = lax.broadcasted_iota(I32, (rows, LANES), 1)

    def resolve():
        def idx_step(b, p):
            cand = p | (jnp.int32(1) << (idx_bits - 1 - b))
            cand_b = jnp.broadcast_to(cand, (rows, LANES))
            cnt, = _count(key_scr, nkc, kc, rows, 1,
                          lambda kk, c0: ((kk == thr_b) & ((lane + c0) < cand_b),))
            return jnp.where(cnt < need, cand, p)

        cut = lax.fori_loop(0, idx_bits, idx_step, jnp.zeros((rows, 1), I32))
        cut_b = jnp.broadcast_to(cut, (rows, LANES))
        drop_row = jnp.broadcast_to(excess, (rows, LANES))

        def drop(c, carry):
            base = pl.multiple_of(c * kc, kc)
            for j in range(kc // LANES):
                sl = pl.ds(base + j * LANES, LANES)
                kk = key_scr[:, sl]
                kill = drop_row & (kk == thr_b) & ((lane + (base + j * LANES)) > cut_b)
                key_scr[:, sl] = jnp.where(kill, -jnp.inf, kk)
            return carry

        lax.fori_loop(0, nkc, drop, 0)

    pl.when(any_excess)(resolve)
    return thr, any_excess


def _select_threshold_lanes(keyt_scr, n_sel, lo, hi):
    nkeys, nq = keyt_scr.shape
    n_f = float(n_sel)

    def count(n, preds):
        ways = 8
        accs = [[jnp.zeros((SUBLANES, nq), F32) for _ in range(ways)] for _ in range(n)]
        for v in range(nkeys // SUBLANES):
            ps = preds(keyt_scr[v * SUBLANES:(v + 1) * SUBLANES, :])
            for i in range(n):
                accs[i][v % ways] = accs[i][v % ways] + jnp.where(ps[i], 1.0, 0.0)
        out = []
        for i in range(n):
            parts = accs[i]
            while len(parts) > 1:
                parts = [parts[j] + parts[j + 1] for j in range(0, len(parts), 2)]
            out.append(jnp.sum(parts[0], axis=0, keepdims=True))
        return out

    def bit_step(b, acc):
        cand = acc | (jnp.int32(1) << (31 - b))
        cand_s = jnp.broadcast_to(_key_score(cand ^ jnp.int32(INT_MIN)), (SUBLANES, nq))
        cnt, = count(1, lambda kk: (kk >= cand_s,))
        return jnp.where(cnt >= n_f, cand, acc)

    k_lo = _sort_key(jnp.maximum(lo - jnp.abs(lo) - 1e-30, -FLT_BIG))
    k_hi = _sort_key(jnp.minimum(hi + jnp.abs(hi) + 1e-30, FLT_BIG))
    u_lo, u_hi = k_lo ^ jnp.int32(INT_MIN), k_hi ^ jnp.int32(INT_MIN)
    first_bit = jnp.minimum(jnp.min(lax.clz(u_lo ^ u_hi).astype(F32)).astype(I32), 31)
    acc0 = u_hi & jnp.where(first_bit > 0, jnp.int32(-1) << (32 - jnp.maximum(first_bit, 1)), 0)
    acc = lax.fori_loop(first_bit, 32, bit_step, acc0)
    thr = jnp.where(acc != 0, _key_score(acc ^ jnp.int32(INT_MIN)), -FLT_BIG)
    thr_s = jnp.broadcast_to(thr, (SUBLANES, nq))
    cnt_gt, cnt_eq = count(2, lambda kk: (kk > thr_s, kk == thr_s))
    tied = (acc != 0) & (cnt_eq > n_f - cnt_gt)
    last = keyt_scr[nkeys - LANES:nkeys, :]
    full = jnp.max(jnp.where(last >= jnp.broadcast_to(thr, (LANES, nq)), 1.0, 0.0), axis=0, keepdims=True) > 0.0
    any_tied = jnp.max(jnp.where(tied, 1.0, 0.0)) > 0.0
    any_full = jnp.max(jnp.where(full, 1.0, 0.0)) > 0.0
    return thr, any_tied, any_full


LANE_TOP = 12
STREAMS = 2


def _dsa_prompt_kernel(qi3_ref, small_ref, q_ref, ki3_ref, k_ref, v_ref, o_ref, key_scr, w_scr, cand_scr,
                       ckey_scr, m_scr, acc_scr, *, tq, kc, n_sel, idx_bits):
    i = pl.program_id(1)
    nkc = (i * tq + tq + kc - 1) // kc
    q3 = qi3_ref[...].reshape(IDX_HEADS * tq, 2 * LANES)
    wi = small_ref[:, WI_LANE:WI_LANE + IDX_HEADS]
    for h in range(IDX_HEADS):
        w_scr[h] = jnp.broadcast_to(wi[:, h:h + 1], (tq, LANES))
    row = i * tq + lax.broadcasted_iota(I32, (tq, LANES), 0)
    lane = lax.broadcasted_iota(I32, (tq, LANES), 1)
    cand_chunks = LANE_TOP * LANES // kc
    reduce_keys = nkc > cand_chunks
    cand_scr[...] = jnp.full(cand_scr.shape, -jnp.inf, F32)

    def score_chunk(c, carry):
        base = pl.multiple_of(c * kc, kc)
        s = _dot_nt(q3, ki3_ref[pl.ds(base, kc), :])
        keys = []
        for j in range(kc // LANES):
            acc = None
            for h in range(IDX_HEADS):
                t = w_scr[h] * jnp.maximum(s[h * tq:(h + 1) * tq, j * LANES:(j + 1) * LANES], 0.0)
                acc = t if acc is None else acc + t
            valid = (lane + (base + j * LANES)) <= row
            sc = jnp.where(valid, acc, -jnp.inf)
            key_scr[:, pl.ds(base + j * LANES, LANES)] = sc
            keys.append(sc)
        for lvl in range(LANE_TOP):
            kept = cand_scr[:, lvl * LANES:(lvl + 1) * LANES]
            for j in range(len(keys)):
                kept, keys[j] = jnp.maximum(kept, keys[j]), jnp.minimum(kept, keys[j])
            cand_scr[:, lvl * LANES:(lvl + 1) * LANES] = kept
        return carry

    lax.fori_loop(0, nkc, score_chunk, 0)

    def from_candidates():
        for lvl in range(LANE_TOP):
            ckey_scr[lvl * LANES:(lvl + 1) * LANES, :] = cand_scr[:, lvl * LANES:(lvl + 1) * LANES].T
        deep = (n_sel + LANES - 1) // LANES - 1
        lo = jnp.min(ckey_scr[deep * LANES:(deep + 1) * LANES, :], axis=0, keepdims=True)
        hi = jnp.max(ckey_scr[0:LANES, :], axis=0, keepdims=True)
        thr_row, any_tied, any_full = _select_threshold_lanes(ckey_scr, n_sel, lo, hi)
        thr_sq = jnp.broadcast_to(thr_row, (tq, LANES)).T
        return thr_sq, jnp.logical_or(any_tied, jnp.logical_and(any_full, reduce_keys))

    thr_c, redo = from_candidates()
    thr_b = lax.cond(
        redo,
        lambda: jnp.broadcast_to(_select_threshold(key_scr, nkc, kc, tq, n_sel, idx_bits)[0], (tq, LANES)),
        lambda: thr_c)

    qs = []
    for g in range(KV_HEADS):
        qs.append(jnp.concatenate(
            [q_ref[:, (g * GROUP + a) * HEAD_DIM:(g * GROUP + a + 1) * HEAD_DIM] for a in range(GROUP)], axis=0))

    rq = GROUP * tq
    ones = jnp.ones((kc, HEAD_DIM), BF16)
    m_scr[...] = jnp.full(m_scr.shape, NEG, F32)
    acc_scr[...] = jnp.zeros(acc_scr.shape, F32)

    def attend_step(cc, carry):
        for st in range(STREAMS):
            c = cc * STREAMS + st
            live = c < nkc
            base = pl.multiple_of(jnp.minimum(c, nkc - 1) * kc, kc)
            biases = []
            for j in range(kc // LANES):
                sel = jnp.logical_and(key_scr[:, pl.ds(base + j * LANES, LANES)] >= thr_b, live)
                biases.append(jnp.where(sel, 0.0, NEG))
            bias = jnp.concatenate(biases, axis=1)
            bias = jnp.concatenate([bias] * GROUP, axis=0)
            for g in range(KV_HEADS):
                slot = st * KV_HEADS + g
                kg = k_ref[pl.ds(base, kc), g * HEAD_DIM:(g + 1) * HEAD_DIM]
                vg = jnp.concatenate([v_ref[pl.ds(base, kc), g * HEAD_DIM:(g + 1) * HEAD_DIM], ones], axis=1)
                s = _dot_nt(qs[g], kg) + bias
                m = m_scr[slot]
                m_new = jnp.maximum(m, jnp.broadcast_to(jnp.max(s, axis=1, keepdims=True), (rq, LANES)))
                alpha = jnp.exp(m - m_new)
                p = jnp.concatenate([jnp.exp(s[:, j * LANES:(j + 1) * LANES] - m_new)
                                     for j in range(kc // LANES)], axis=1)
                m_scr[slot] = m_new
                acc_scr[slot] = jnp.concatenate([alpha, alpha], axis=1) * acc_scr[slot] + _dot(p.astype(BF16), vg)
        return carry

    lax.fori_loop(0, (nkc + STREAMS - 1) // STREAMS, attend_step, 0)
    for g in range(KV_HEADS):
        m_all = m_scr[g]
        for st in range(1, STREAMS):
            m_all = jnp.maximum(m_all, m_scr[st * KV_HEADS + g])
        acc = None
        for st in range(STREAMS):
            w = jnp.exp(m_scr[st * KV_HEADS + g] - m_all)
            part = jnp.concatenate([w, w], axis=1) * acc_scr[st * KV_HEADS + g]
            acc = part if acc is None else acc + part
        o = acc[:, :HEAD_DIM] / acc[:, HEAD_DIM:]
        for a in range(GROUP):
            hh = g * GROUP + a
            o_ref[:, hh * HEAD_DIM:(hh + 1) * HEAD_DIM] = o[a * tq:(a + 1) * tq].astype(BF16)


def _dsa_prompt(qi3, small, q_bf, ki3, k_bf, v_bf, batch, seq):
    tq = LANES
    kc = min(512, seq)
    nq = seq // tq
    n_sel = min(TOPK_MAX, seq // 4)
    idx_bits = max(1, (seq - 1).bit_length())
    kvd = KV_HEADS * HEAD_DIM
    qrow = lambda b, i: (b * nq + i, 0)
    kern = functools.partial(_dsa_prompt_kernel, tq=tq, kc=kc, n_sel=n_sel, idx_bits=idx_bits)
    return pl.pallas_call(
        kern,
        out_shape=jax.ShapeDtypeStruct((batch * seq, ATTN_HEADS * HEAD_DIM), BF16),
        grid=(batch, nq),
        in_specs=[
            pl.BlockSpec((IDX_HEADS, tq, 2 * LANES), lambda b, i: (0, b * nq + i, 0)),
            pl.BlockSpec((tq, LANES), qrow),
            pl.BlockSpec((tq, ATTN_HEADS * HEAD_DIM), qrow),
            pl.BlockSpec((seq, 2 * LANES), lambda b, i: (b, 0)),
            pl.BlockSpec((seq, kvd), lambda b, i: (b, 0)),
            pl.BlockSpec((seq, kvd), lambda b, i: (b, 0)),
        ],
        out_specs=pl.BlockSpec((tq, ATTN_HEADS * HEAD_DIM), qrow),
        scratch_shapes=[pltpu.VMEM((tq, seq), F32), pltpu.VMEM((IDX_HEADS, tq, LANES), F32),
                        pltpu.VMEM((tq, LANE_TOP * LANES), F32),
                        pltpu.VMEM((LANE_TOP * LANES, tq), F32),
                        pltpu.VMEM((STREAMS * KV_HEADS, GROUP * tq, LANES), F32),
                        pltpu.VMEM((STREAMS * KV_HEADS, GROUP * tq, 2 * HEAD_DIM), F32)],
        compiler_params=_cparams(("arbitrary", "arbitrary")),
        name="dsa_prompt",
    )(qi3, small, q_bf, ki3, k_bf, v_bf)


SEQ_PER_STEP = 2


def _sample_scores_kernel(pt_ref, qi3_ref, small_ref, ki3n_ref, *rest, n_pages, t_len):
    pages = rest[:SEQ_PER_STEP * n_pages]
    s_o = rest[SEQ_PER_STEP * n_pages]
    nr = SEQ_PER_STEP * t_len
    q3 = qi3_ref[...].reshape(IDX_HEADS * nr, 2 * LANES)
    wi = small_ref[:, WI_LANE:WI_LANE + IDX_HEADS]
    knew = jnp.concatenate([ki3n_ref[...], jnp.zeros((PAGE - nr, 2 * LANES), BF16)], axis=0)
    zpad = jnp.zeros((2 * LANES - 3 * IDX_DIM, PAGE), F32)
    lane = lax.broadcasted_iota(I32, (t_len, LANES), 1)
    trow = lax.broadcasted_iota(I32, (t_len, LANES), 0)

    def head_sum(s, j):
        acc = None
        for h in range(IDX_HEADS):
            r0 = h * nr + j * t_len
            t = wi[j * t_len:(j + 1) * t_len, h:h + 1] * jnp.maximum(s[r0:r0 + t_len], 0.0)
            acc = t if acc is None else acc + t
        return acc

    for j in range(SEQ_PER_STEP):
        for p in range(n_pages):
            kp = pages[j * n_pages + p][...]
            hi = kp.astype(BF16).astype(F32)
            k3 = jnp.concatenate([hi, hi, kp - hi, zpad], axis=0).astype(BF16)
            s_o[j, :, p * PAGE:(p + 1) * PAGE] = head_sum(_dot(q3, k3), j)
        sn = head_sum(_dot_nt(q3, knew), j)
        ok = (lane >= j * t_len) & (lane - j * t_len <= trow)
        s_o[j, :, n_pages * PAGE:(n_pages + 1) * PAGE] = jnp.where(ok, sn, -jnp.inf)


def _sample_scores(page_table, qi3, small, ki3, cache_kidx_t, layer, rows_p, t_len):
    db, n_pages = page_table.shape
    nr = SEQ_PER_STEP * t_len
    base = rows_p // nr
    lp = (n_pages + 1) * PAGE
    in_specs = [
        pl.BlockSpec((IDX_HEADS, nr, 2 * LANES), lambda n, pt: (0, base + n, 0)),
        pl.BlockSpec((nr, LANES), lambda n, pt: (base + n, 0)),
        pl.BlockSpec((nr, 2 * LANES), lambda n, pt: (base + n, 0)),
    ]
    for j in range(SEQ_PER_STEP):
        for p in range(n_pages):
            in_specs.append(pl.BlockSpec(
                (None, None, IDX_DIM, PAGE),
                lambda n, pt, j=j, p=p: (pt[n * SEQ_PER_STEP + j, p], layer, 0, 0)))
    kern = functools.partial(_sample_scores_kernel, n_pages=n_pages, t_len=t_len)
    return pl.pallas_call(
        kern,
        out_shape=jax.ShapeDtypeStruct((db, t_len, lp), F32),
        grid_spec=pltpu.PrefetchScalarGridSpec(
            num_scalar_prefetch=1,
            grid=(db // SEQ_PER_STEP,),
            in_specs=in_specs,
            out_specs=pl.BlockSpec((SEQ_PER_STEP, t_len, lp), lambda n, pt: (n, 0, 0)),
        ),
        compiler_params=_cparams(("arbitrary",)),
        name="dsa_sample_scores",
    )(page_table, qi3, small, ki3, *([cache_kidx_t] * (SEQ_PER_STEP * n_pages)))


def _sample_select_kernel(s_ref, b_ref, key_scr, *, n_sel, idx_bits):
    rows, lp = s_ref.shape
    nkc = lp // LANES
    for c in range(nkc):
        sc = s_ref[:, c * LANES:(c + 1) * LANES]
        key_scr[:, c * LANES:(c + 1) * LANES] = sc
    thr, _ = _select_threshold(key_scr, nkc, LANES, rows, n_sel, idx_bits)
    thr_b = jnp.broadcast_to(thr, (rows, LANES))
    for c in range(nkc):
        b_ref[:, c * LANES:(c + 1) * LANES] = jnp.where(key_scr[:, c * LANES:(c + 1) * LANES] >= thr_b, 0.0, NEG)


def _sample_select(scores2d, n_sel):
    rows, lp = scores2d.shape
    tr = min(LANES, rows)
    kern = functools.partial(_sample_select_kernel, n_sel=n_sel, idx_bits=max(1, (lp - 1).bit_length()))
    return pl.pallas_call(
        kern,
        out_shape=jax.ShapeDtypeStruct((rows, lp), F32),
        grid=(rows // tr,),
        in_specs=[pl.BlockSpec((tr, lp), lambda r: (r, 0))],
        out_specs=pl.BlockSpec((tr, lp), lambda r: (r, 0)),
        scratch_shapes=[pltpu.VMEM((tr, lp), F32)],
        compiler_params=_cparams(("arbitrary",)),
        name="dsa_sample_select",
    )(scores2d)


def _sample_attend_kernel(pt_ref, q_ref, bias_ref, kn_ref, vn_ref, *rest, n_pages, t_len):
    npg = SEQ_PER_STEP * n_pages
    kpages, vpages = rest[:npg], rest[npg:2 * npg]
    o_ref = rest[2 * npg]
    kc_scr, vc_scr = rest[2 * npg + 1], rest[2 * npg + 2]
    nr = SEQ_PER_STEP * t_len
    lp = (n_pages + 1) * PAGE
    kvd = KV_HEADS * HEAD_DIM
    bias = jnp.concatenate([bias_ref[...]] * GROUP, axis=0)
    pad = jnp.zeros((PAGE - nr, kvd), BF16)
    kc_scr[n_pages * PAGE:lp, :] = jnp.concatenate([kn_ref[...], pad], axis=0)
    vc_scr[n_pages * PAGE:lp, :] = jnp.concatenate([vn_ref[...], pad], axis=0)
    for j in range(SEQ_PER_STEP):
        for p in range(n_pages):
            for g in range(KV_HEADS):
                head_rows = pl.ds(g, PAGE, stride=KV_HEADS)
                cols = slice(g * HEAD_DIM, (g + 1) * HEAD_DIM)
                kc_scr[p * PAGE:(p + 1) * PAGE, cols] = kpages[j * n_pages + p][head_rows, :].astype(BF16)
                vc_scr[p * PAGE:(p + 1) * PAGE, cols] = vpages[j * n_pages + p][head_rows, :].astype(BF16)
        for g in range(KV_HEADS):
            qs = jnp.concatenate(
                [q_ref[:, (g * GROUP + a) * HEAD_DIM:(g * GROUP + a + 1) * HEAD_DIM] for a in range(GROUP)],
                axis=0)
            s = _dot_nt(qs, kc_scr[:, g * HEAD_DIM:(g + 1) * HEAD_DIM]) + bias
            m = jnp.max(s, axis=1, keepdims=True)
            p_ = jnp.exp(s - m)
            l = jnp.sum(p_, axis=1, keepdims=True)
            o = _dot(p_.astype(BF16), vc_scr[:, g * HEAD_DIM:(g + 1) * HEAD_DIM]) / l
            for a in range(GROUP):
                hh = g * GROUP + a
                r0 = a * nr + j * t_len
                o_ref[j * t_len:(j + 1) * t_len, hh * HEAD_DIM:(hh + 1) * HEAD_DIM] = (
                    o[r0:r0 + t_len].astype(BF16))


def _sample_attend(page_table, q_bf, bias2d, k_bf, v_bf, cache_k2, cache_v2, depth, layer, rows_p, t_len):
    db, n_pages = page_table.shape
    nr = SEQ_PER_STEP * t_len
    base = rows_p // nr
    lp = (n_pages + 1) * PAGE
    kvd = KV_HEADS * HEAD_DIM
    in_specs = [
        pl.BlockSpec((nr, ATTN_HEADS * HEAD_DIM), lambda n, pt: (base + n, 0)),
        pl.BlockSpec((nr, lp), lambda n, pt: (n, 0)),
        pl.BlockSpec((nr, kvd), lambda n, pt: (base + n, 0)),
        pl.BlockSpec((nr, kvd), lambda n, pt: (base + n, 0)),
    ]
    for _ in range(2):
        for j in range(SEQ_PER_STEP):
            for p in range(n_pages):
                in_specs.append(pl.BlockSpec(
                    (PAGE * KV_HEADS, HEAD_DIM),
                    lambda n, pt, j=j, p=p: (pt[n * SEQ_PER_STEP + j, p] * depth + layer, 0)))
    kern = functools.partial(_sample_attend_kernel, n_pages=n_pages, t_len=t_len)
    npg = SEQ_PER_STEP * n_pages
    return pl.pallas_call(
        kern,
        out_shape=jax.ShapeDtypeStruct((db * t_len, ATTN_HEADS * HEAD_DIM), BF16),
        grid_spec=pltpu.PrefetchScalarGridSpec(
            num_scalar_prefetch=1,
            grid=(db // SEQ_PER_STEP,),
            in_specs=in_specs,
            out_specs=pl.BlockSpec((nr, ATTN_HEADS * HEAD_DIM), lambda n, pt: (n, 0)),
            scratch_shapes=[pltpu.VMEM((lp, kvd), BF16), pltpu.VMEM((lp, kvd), BF16)],
        ),
        compiler_params=_cparams(("arbitrary",)),
        name="dsa_sample_attend",
    )(page_table, q_bf, bias2d, k_bf, v_bf, *([cache_k2] * npg), *([cache_v2] * npg))


def _chunk_masks(n, chunk):
    ri = lax.broadcasted_iota(I32, (n, n), 0)
    ci = lax.broadcasted_iota(I32, (n, n), 1)

    def same(size):
        sh = size.bit_length() - 1
        return (ri >> sh) == (ci >> sh)

    same_c = same(chunk)
    incl = same_c & (ci <= ri)
    strict = same_c & (ci < ri)
    base = min(SUBLANES, chunk)
    levels = []
    s = base
    while s < chunk:
        levels.append(same(2 * s) & jnp.logical_not(same(s)))
        s *= 2
    return incl, strict, same(base), levels, (ri == ci)


def _unit_lower_inverse(a_heads, same_base, levels, eye):
    def mm(xs, ys):
        return _dot(xs[0], ys[0]) + (_dot(xs[0], ys[1]) + _dot(xs[1], ys[0]))

    ident = jnp.where(eye, 1.0, 0.0)
    ad = [jnp.where(same_base, a, 0.0) for a in a_heads]
    ad_s = [_split(x) for x in ad]
    a2 = [mm(s, s) for s in ad_s]
    a2_s = [_split(x) for x in a2]
    a4_s = [_split(mm(s, s)) for s in a2_s]
    t = [ident - x + y - mm(xs, ys) for x, y, xs, ys in zip(ad, a2, ad_s, a2_s)]
    t = [x + mm(_split(x), ys) for x, ys in zip(t, a4_s)]
    for lv in levels:
        t_s = [_split(x) for x in t]
        to = [mm(xs, _split(jnp.where(lv, a, 0.0))) for xs, a in zip(t_s, a_heads)]
        t = [x - mm(_split(y), xs) for x, y, xs in zip(t, to, t_s)]
    return t


def _mm1(a, b, dot=_dot):
    return dot(a.astype(BF16), b.astype(BF16))


def _gdn_intra(q, k, v, beta_c, gc_c, gc_r, masks):
    incl, strict, same_base, levels, eye = masks
    heads = range(len(q))
    decay = [jnp.exp(jnp.where(incl, gc_c[h] - gc_r[h], -jnp.inf)) for h in heads]
    kb = [k[h] * beta_c[h] for h in heads]
    a = [jnp.where(strict, _mm1(kb[h], k[h], _dot_nt) * decay[h], 0.0) for h in heads]
    t = _unit_lower_inverse(a, same_base, levels, eye)
    uw = [_mm3(t[h], jnp.concatenate([v[h] * beta_c[h], kb[h] * jnp.exp(gc_c[h])], axis=1)) for h in heads]
    qk = [_mm1(q[h], k[h], _dot_nt) * decay[h] for h in heads]
    return [x[:, :HEAD_DIM] for x in uw], [x[:, HEAD_DIM:] for x in uw], qk


def _l2(x):
    return x * lax.rsqrt(jnp.sum(x * x, axis=-1, keepdims=True) + EPS)


def _cum_matrices(n, chunk):
    ri = lax.broadcasted_iota(I32, (n, n), 0)
    ci = lax.broadcasted_iota(I32, (n, n), 1)
    sh = chunk.bit_length() - 1
    same = (ri >> sh) == (ci >> sh)
    lower = jnp.where(same & (ci <= ri), 1.0, 0.0).astype(BF16)
    upper = jnp.where(same & (ri <= ci), 1.0, 0.0).astype(BF16)
    return lower, upper


def _gdn_gates(sm, smt, alog_l, dtb_l, alog_c, dtb_c, chunk):
    n = sm.shape[0]
    lower, upper = _cum_matrices(n, chunk)
    g_tile = -jnp.exp(alog_l) * _softplus(sm + dtb_l)
    beta_tile = jax.nn.sigmoid(sm)
    gc_cols = _mm_exact_lhs(lower, g_tile)
    g_rows = -jnp.exp(alog_c) * _softplus(smt + dtb_c)
    gc_rows = _mm_exact_rhs(g_rows, upper)
    return beta_tile, gc_cols, gc_rows


def _gdn_prompt_kernel(gq_ref, gz_ref, sm_ref, smt_ref, cw_ref, alog_l, dtb_l, alog_c, dtb_c, ng_ref,
                       o_ref, s_o_ref, stage, s_scr, *, tt, chunk):
    t_idx = pl.program_id(1)
    hd = HEAD_DIM
    nh = GDN_HEADS

    @pl.when(t_idx == 0)
    def _():
        stage[0:SUBLANES, :] = jnp.zeros((SUBLANES, stage.shape[1]), F32)
        s_scr[...] = jnp.zeros(s_scr.shape, F32)

    x = gq_ref[...]
    stage[SUBLANES:SUBLANES + tt, :] = x
    y = None
    for j in range(CONV_W):
        term = stage[pl.ds(SUBLANES - (CONV_W - 1) + j, tt), :] * cw_ref[j:j + 1, :]
        y = term if y is None else y + term
    stage[0:SUBLANES, :] = x[tt - SUBLANES:tt, :]
    y = _silu(y)

    beta_tile, gc_cols, gc_rows = _gdn_gates(sm_ref[...], smt_ref[...], alog_l[...], dtb_l[...],
                                             alog_c[...], dtb_c[...], chunk)
    masks = _chunk_masks(tt, chunk)
    ng = ng_ref[...]
    heads = range(nh)
    q = [_l2(y[:, h * hd:(h + 1) * hd]) * (hd ** -0.5) for h in heads]
    k = [_l2(y[:, (nh + h) * hd:(nh + h + 1) * hd]) for h in heads]
    v = [y[:, (2 * nh + h) * hd:(2 * nh + h + 1) * hd] for h in heads]
    beta_c = [beta_tile[:, GB_LANE + h:GB_LANE + h + 1] for h in heads]
    gc_c = [gc_cols[:, GA_LANE + h:GA_LANE + h + 1] for h in heads]
    gc_r = [gc_rows[h:h + 1, :] for h in heads]
    u, w, qk = _gdn_intra(q, k, v, beta_c, gc_c, gc_r, masks)
    qg = [q[h] * jnp.exp(gc_c[h]) for h in heads]
    s = [s_scr[h] for h in heads]
    vnew = [[] for _ in heads]
    ointer = [[] for _ in heads]
    for c in range(tt // chunk):
        r = slice(c * chunk, (c + 1) * chunk)
        for h in heads:
            vn = u[h][r] - _mm1(w[h][r], s[h])
            ointer[h].append(_mm1(qg[h][r], s[h]))
            g_last = gc_c[h][(c + 1) * chunk - 1:(c + 1) * chunk, :]
            kdec = k[h][r] * jnp.exp(g_last - gc_c[h][r])
            s[h] = s[h] * jnp.exp(g_last) + _mm1(kdec, vn, _dot_tn)
            vnew[h].append(vn)
    for h in heads:
        s_scr[h] = s[h]
        o = jnp.concatenate(ointer[h], axis=0) + _mm1(qk[h], jnp.concatenate(vnew[h], axis=0))
        o = _rms(o) * ng * _silu(gz_ref[:, h * hd:(h + 1) * hd])
        o_ref[:, h * hd:(h + 1) * hd] = o.astype(BF16)

    @pl.when(t_idx == pl.num_programs(1) - 1)
    def _():
        s_o_ref[...] = s_scr[...]


def _gdn_prompt(gq, gz, small, small_t, conv_w_l, alog_l, dtb_l, alog_c, dtb_c, ng, batch, seq):
    tt = min(256, seq)
    chunk = min(GDN_CHUNK, seq)
    nt = seq // tt
    gdim = gq.shape[1]
    zdim = gz.shape[1]
    row = lambda b, t: (b * nt + t, 0)
    const = lambda b, t: (0, 0)
    kern = functools.partial(_gdn_prompt_kernel, tt=tt, chunk=chunk)
    return pl.pallas_call(
        kern,
        out_shape=[jax.ShapeDtypeStruct((batch * seq, zdim), BF16),
                   jax.ShapeDtypeStruct((batch, GDN_HEADS, HEAD_DIM, HEAD_DIM), F32)],
        grid=(batch, nt),
        in_specs=[
            pl.BlockSpec((tt, gdim), row),
            pl.BlockSpec((tt, zdim), row),
            pl.BlockSpec((tt, LANES), row),
            pl.BlockSpec((SUBLANES, tt), lambda b, t: (0, b * nt + t)),
            pl.BlockSpec((CONV_W, gdim), const),
            pl.BlockSpec((1, LANES), const),
            pl.BlockSpec((1, LANES), const),
            pl.BlockSpec((SUBLANES, 1), const),
            pl.BlockSpec((SUBLANES, 1), const),
            pl.BlockSpec((1, HEAD_DIM), const),
        ],
        out_specs=[pl.BlockSpec((tt, zdim), row),
                   pl.BlockSpec((None, GDN_HEADS, HEAD_DIM, HEAD_DIM), lambda b, t: (b, 0, 0, 0))],
        scratch_shapes=[pltpu.VMEM((tt + SUBLANES, gdim), F32),
                        pltpu.VMEM((GDN_HEADS, HEAD_DIM, HEAD_DIM), F32)],
        compiler_params=_cparams(("arbitrary", "arbitrary")),
        name="gdn_prompt",
    )(gq, gz, small, small_t, conv_w_l, alog_l, dtb_l, alog_c, dtb_c, ng)


def _gdn_sample_kernel(gq_ref, gz_ref, sm_ref, smt_ref, cst_ref, s0_ref, cw_ref, alog_l, dtb_l, alog_c,
                       dtb_c, ng_ref, o_ref, s_o_ref, stage, uw_scr, vn_scr, oi_scr, *, nb, t_len):
    hd = HEAD_DIM
    nh = GDN_HEADS
    n = nb * t_len
    gdim = gq_ref.shape[1]
    stage[:, 0:SUBLANES, :] = cst_ref[...].reshape(nb, SUBLANES, gdim)
    stage[:, SUBLANES:SUBLANES + t_len, :] = gq_ref[...].reshape(nb, t_len, gdim)
    y = None
    for j in range(CONV_W):
        term = stage[:, pl.ds(SUBLANES - (CONV_W - 1) + j, t_len), :] * cw_ref[j:j + 1, :]
        y = term if y is None else y + term
    y = _silu(y).reshape(n, gdim)

    beta_tile, gc_cols, gc_rows = _gdn_gates(sm_ref[...], smt_ref[...], alog_l[...], dtb_l[...],
                                             alog_c[...], dtb_c[...], t_len)
    masks = _chunk_masks(n, t_len)
    ng = ng_ref[...]
    ri = lax.broadcasted_iota(I32, (n, n), 0)
    ci = lax.broadcasted_iota(I32, (n, n), 1)
    sh = t_len.bit_length() - 1
    pick_last = jnp.where(((ri >> sh) == (ci >> sh)) & ((ci & (t_len - 1)) == t_len - 1), 1.0, 0.0).astype(BF16)
    g_last_cols = _mm_exact_lhs(pick_last, gc_cols)
    heads = range(nh)
    q = [_l2(y[:, h * hd:(h + 1) * hd]) * (hd ** -0.5) for h in heads]
    k = [_l2(y[:, (nh + h) * hd:(nh + h + 1) * hd]) for h in heads]
    v = [y[:, (2 * nh + h) * hd:(2 * nh + h + 1) * hd] for h in heads]
    beta_c = [beta_tile[:, GB_LANE + h:GB_LANE + h + 1] for h in heads]
    gc_c = [gc_cols[:, GA_LANE + h:GA_LANE + h + 1] for h in heads]
    gc_r = [gc_rows[h:h + 1, :] for h in heads]
    u, w, qk = _gdn_intra(q, k, v, beta_c, gc_c, gc_r, masks)
    for h in heads:
        g_last_c = g_last_cols[:, GA_LANE + h:GA_LANE + h + 1]
        uw_scr[h, 0] = u[h]
        uw_scr[h, 1] = w[h]
        uw_scr[h, 2] = q[h] * jnp.exp(gc_c[h])
        uw_scr[h, 3] = k[h] * jnp.exp(g_last_c - gc_c[h])
        uw_scr[h, 4] = jnp.broadcast_to(jnp.exp(g_last_c), (n, hd))

    def seq_step(i, carry):
        r0 = pl.multiple_of(i * t_len, t_len)
        rows = pl.ds(r0, t_len)
        for h in heads:
            s = s0_ref[i, h]
            vn = uw_scr[h, 0, rows, :] - _mm1(uw_scr[h, 1, rows, :], s)
            oi_scr[h, rows, :] = _mm1(uw_scr[h, 2, rows, :], s)
            vn_scr[h, rows, :] = vn
            dec = uw_scr[h, 4, pl.ds(r0, 1), :]
            s_o_ref[i, h] = s * dec + _mm1(uw_scr[h, 3, rows, :], vn, _dot_tn)
        return carry

    lax.fori_loop(0, nb, seq_step, 0)
    for h in heads:
        o = oi_scr[h] + _mm1(qk[h], vn_scr[h])
        o = _rms(o) * ng * _silu(gz_ref[:, h * hd:(h + 1) * hd])
        o_ref[:, h * hd:(h + 1) * hd] = o.astype(BF16)


def _gdn_sample(gq, gz, small, small_t, cstate, state_gdn, layer, conv_w_l, alog_l, dtb_l, alog_c, dtb_c, ng,
                rows_p, db, t_len):
    nb = min(16, db)
    n = nb * t_len
    base = rows_p // n
    gdim = gq.shape[1]
    zdim = gz.shape[1]
    row = lambda i: (base + i, 0)
    const = lambda i: (0, 0)
    kern = functools.partial(_gdn_sample_kernel, nb=nb, t_len=t_len)
    return pl.pallas_call(
        kern,
        out_shape=[jax.ShapeDtypeStruct((db * t_len, zdim), BF16),
                   jax.ShapeDtypeStruct((db, GDN_HEADS, HEAD_DIM, HEAD_DIM), F32)],
        grid=(db // nb,),
        in_specs=[
            pl.BlockSpec((n, gdim), row),
            pl.BlockSpec((n, zdim), row),
            pl.BlockSpec((n, LANES), row),
            pl.BlockSpec((SUBLANES, n), lambda i: (0, base + i)),
            pl.BlockSpec((n, gdim), lambda i: (i, 0)),
            pl.BlockSpec((nb, None, GDN_HEADS, HEAD_DIM, HEAD_DIM), lambda i: (i, layer, 0, 0, 0)),
            pl.BlockSpec((CONV_W, gdim), const),
            pl.BlockSpec((1, LANES), const),
            pl.BlockSpec((1, LANES), const),
            pl.BlockSpec((SUBLANES, 1), const),
            pl.BlockSpec((SUBLANES, 1), const),
            pl.BlockSpec((1, HEAD_DIM), const),
        ],
        out_specs=[pl.BlockSpec((n, zdim), lambda i: (i, 0)),
                   pl.BlockSpec((nb, GDN_HEADS, HEAD_DIM, HEAD_DIM), lambda i: (i, 0, 0, 0))],
        scratch_shapes=[pltpu.VMEM((nb, 2 * SUBLANES, gdim), F32),
                        pltpu.VMEM((GDN_HEADS, 5, n, HEAD_DIM), F32),
                        pltpu.VMEM((GDN_HEADS, n, HEAD_DIM), F32),
                        pltpu.VMEM((GDN_HEADS, n, HEAD_DIM), F32)],
        compiler_params=_cparams(("arbitrary",)),
        name="gdn_sample",
    )(gq, gz, small, small_t, cstate, state_gdn, conv_w_l, alog_l, dtb_l, alog_c, dtb_c, ng)


def _outproj_kernel(x_ref, att_ref, gdn_ref, gate_ref, sh_ref, sc_ref, g_ref, wo_ref, *rest, moe):
    if moe:
        rw_ref, rb_ref, x_o, h_o, lg_o = rest
    else:
        x_o, h_o = rest
    x = x_ref[...]
    adim = att_ref.shape[1]
    y = _dot(att_ref[...], wo_ref[0:adim, :]) + _dot(gdn_ref[...], wo_ref[adim:, :])
    xn = x + gate_ref[...] * y.reshape(x.shape)
    x_o[...] = xn
    h = (_rms(xn) * g_ref[...]) * (1.0 + sc_ref[...]) + sh_ref[...]
    h2 = h.reshape(y.shape)
    h_o[...] = h2.astype(BF16)
    if moe:
        lg_o[...] = _mm3(h2, rw_ref[...]) + rb_ref[...]


def _outproj(x3, att, gdn, gate, sh, sc, g, wo, router, tile_of, tm):
    ng, _, d = x3.shape
    rows = ng * SUBLANES
    gt = tm // SUBLANES
    moe = router is not None
    const = lambda t: (0, 0)
    row = lambda t: (t, 0)
    modspec = pl.BlockSpec((gt, 1, d), lambda t: (tile_of(t), 0, 0))
    in_specs = [
        pl.BlockSpec((gt, SUBLANES, d), lambda t: (t, 0, 0)),
        pl.BlockSpec((tm, att.shape[1]), row),
        pl.BlockSpec((tm, gdn.shape[1]), row),
        modspec, modspec, modspec,
        pl.BlockSpec((1, d), const),
        pl.BlockSpec(wo.shape, const),
    ]
    out_shape = [jax.ShapeDtypeStruct(x3.shape, F32), jax.ShapeDtypeStruct((rows, d), BF16)]
    out_specs = [pl.BlockSpec((gt, SUBLANES, d), lambda t: (t, 0, 0)), pl.BlockSpec((tm, d), row)]
    args = [x3, att, gdn, gate, sh, sc, g, wo]
    if moe:
        in_specs += [pl.BlockSpec((d, LANES), const), pl.BlockSpec((1, LANES), const)]
        out_shape.append(jax.ShapeDtypeStruct((rows, LANES), F32))
        out_specs.append(pl.BlockSpec((tm, LANES), row))
        args += list(router)
    return pl.pallas_call(
        functools.partial(_outproj_kernel, moe=moe),
        out_shape=out_shape,
        grid=(rows // tm,),
        in_specs=in_specs,
        out_specs=out_specs,
        compiler_params=_cparams(("arbitrary",)),
        name="out_proj",
    )(*args)


def _ffn_kernel(x_ref, h_ref, gate_ref, wg_ref, wu_ref, wd_ref, o_ref, *, fc):
    h = h_ref[...]
    dff = wg_ref.shape[1]
    acc = None
    for c in range(dff // fc):
        a = _dot(h, wg_ref[:, c * fc:(c + 1) * fc])
        u = _dot(h, wu_ref[:, c * fc:(c + 1) * fc])
        t = _dot((_silu(a) * u).astype(BF16), wd_ref[c * fc:(c + 1) * fc, :])
        acc = t if acc is None else acc + t
    x = x_ref[...]
    o_ref[...] = x + gate_ref[...] * acc.reshape(x.shape)


def _ffn(x3, hff, gate, wg, wu, wd, tile_of, tm):
    ng, _, d = x3.shape
    rows = ng * SUBLANES
    gt = tm // SUBLANES
    dff = wg.shape[1]
    fc = dff
    for cand in (1408, 1024, 768, 512, 256, 128):
        if dff % cand == 0:
            fc = cand
            break
    const = lambda t: (0, 0)
    return pl.pallas_call(
        functools.partial(_ffn_kernel, fc=fc),
        out_shape=jax.ShapeDtypeStruct(x3.shape, F32),
        grid=(rows // tm,),
        in_specs=[
            pl.BlockSpec((gt, SUBLANES, d), lambda t: (t, 0, 0)),
            pl.BlockSpec((tm, d), lambda t: (t, 0)),
            pl.BlockSpec((gt, 1, d), lambda t: (tile_of(t), 0, 0)),
            pl.BlockSpec(wg.shape, const),
            pl.BlockSpec(wu.shape, const),
            pl.BlockSpec(wd.shape, const),
        ],
        out_specs=pl.BlockSpec((gt, SUBLANES, d), lambda t: (t, 0, 0)),
        compiler_params=_cparams(("arbitrary",)),
        name="ffn_dense",
    )(x3, hff, gate, wg, wu, wd)


def _moe_kernel(x_ref, h_ref, lg_ref, gate_ref, wg_ref, wu_ref, wd_ref, o_ref, acc_scr, g_scr):
    e = pl.program_id(1)
    tm = h_ref.shape[0]

    @pl.when(e == 0)
    def _():
        lane = lax.broadcasted_iota(I32, (tm, LANES), 1)
        lg = jnp.where(lane < N_EXPERTS, lg_ref[...], -jnp.inf)
        m1 = jnp.max(lg, axis=1, keepdims=True)
        i1 = jnp.min(jnp.where(lg == m1, lane, LANES), axis=1, keepdims=True)
        rest = jnp.where(lane == i1, -jnp.inf, lg)
        m2 = jnp.max(rest, axis=1, keepdims=True)
        i2 = jnp.min(jnp.where(rest == m2, lane, LANES), axis=1, keepdims=True)
        e2 = jnp.exp(m2 - m1)
        den = 1.0 + e2
        g_scr[...] = jnp.where(lane == i1, 1.0 / den, 0.0) + jnp.where(lane == i2, e2 / den, 0.0)
        acc_scr[...] = jnp.zeros(acc_scr.shape, F32)

    h = h_ref[...]
    a = _dot(h, wg_ref[...])
    u = _dot(h, wu_ref[...])
    y = _dot((_silu(a) * u).astype(BF16), wd_ref[...])
    lane = lax.broadcasted_iota(I32, (tm, LANES), 1)
    ge = jnp.sum(jnp.where(lane == e, g_scr[...], 0.0), axis=1, keepdims=True)
    acc_scr[...] += ge * y

    @pl.when(e == pl.num_programs(1) - 1)
    def _():
        x = x_ref[...]
        o_ref[...] = x + gate_ref[...] * acc_scr[...].reshape(x.shape)


def _moe(x3, hff, logits, gate, wg, wu, wd, tile_of, tm):
    ng, _, d = x3.shape
    rows = ng * SUBLANES
    gt = tm // SUBLANES
    ne, _, eff = wg.shape
    return pl.pallas_call(
        _moe_kernel,
        out_shape=jax.ShapeDtypeStruct(x3.shape, F32),
        grid=(rows // tm, ne),
        in_specs=[
            pl.BlockSpec((gt, SUBLANES, d), lambda t, e: (t, 0, 0)),
            pl.BlockSpec((tm, d), lambda t, e: (t, 0)),
            pl.BlockSpec((tm, LANES), lambda t, e: (t, 0)),
            pl.BlockSpec((gt, 1, d), lambda t, e: (tile_of(t), 0, 0)),
            pl.BlockSpec((None, d, eff), lambda t, e: (e, 0, 0)),
            pl.BlockSpec((None, d, eff), lambda t, e: (e, 0, 0)),
            pl.BlockSpec((None, eff, d), lambda t, e: (e, 0, 0)),
        ],
        out_specs=pl.BlockSpec((gt, SUBLANES, d), lambda t, e: (t, 0, 0)),
        scratch_shapes=[pltpu.VMEM((tm, d), F32), pltpu.VMEM((tm, LANES), F32)],
        compiler_params=_cparams(("arbitrary", "arbitrary")),
        name="ffn_moe",
    )(x3, hff, logits, gate, wg, wu, wd)


def _rope_tables(pos):
    pos = pos.astype(F32)[:, None]
    half = HEAD_DIM // 2
    ang = pos * (ROPE_THETA ** (-jnp.arange(half, dtype=F32) / half))[None, :]
    c, s = jnp.cos(ang), jnp.sin(ang)
    cq = jnp.concatenate([c, c], axis=1)
    sq = jnp.concatenate([-s, s], axis=1)
    half = IDX_DIM // 2
    ang = pos * (ROPE_THETA ** (-jnp.arange(half, dtype=F32) / half))[None, :]
    c, s = jnp.cos(ang), jnp.sin(ang)
    z = jnp.zeros_like(s)
    ci = jnp.concatenate([c, c, c, c], axis=1)
    sa = jnp.concatenate([-s, z, -s, z], axis=1)
    sb = jnp.concatenate([z, s, z, s], axis=1)
    return cq, sq, ci, sa, sb


def _lane_vec(vals, lane0):
    return jnp.zeros((1, LANES), F32).at[0, lane0:lane0 + vals.shape[0]].set(vals)


def kernel(x_prompt, x_sample, cache_k, cache_v, cache_kidx, state_gdn, state_conv, page_table, c_prompt, c_sample, mod_w, mod_b, norm_mix_g, norm_ffn_g, w_in, q_norm_g, k_norm_g, conv_w, a_log, dt_bias, gdn_norm_g, w_out, ffn_w_gate, ffn_w_up, ffn_w_down, router_w, router_b, moe_w_gate, moe_w_up, moe_w_down):
    batch, seq, d = x_prompt.shape
    db, t_len, _ = x_sample.shape
    n_layers = mod_w.shape[0]
    n_pages = page_table.shape[1]
    past = n_pages * PAGE
    rows_p, rows_s = batch * seq, db * t_len
    rows = rows_p + rows_s
    assert t_len == SUBLANES and seq % LANES == 0
    tm = min(256, rows_s, seq)
    assert seq % tm == 0 and rows_s % tm == 0
    tiles_per_batch = seq // tm
    n_ptiles = rows_p // tm
    tm_moe = 2 * tm if (seq % (2 * tm) == 0 and rows_s % (2 * tm) == 0) else tm

    def tile_of_for(tile_rows):
        per_batch, n_prompt = seq // tile_rows, rows_p // tile_rows
        return lambda t: jnp.where(t < n_prompt, t // per_batch, batch + (t - n_prompt))

    tile_of = tile_of_for(tm)

    def tab_of(t):
        return jnp.where(t < n_ptiles, t % tiles_per_batch, tiles_per_batch)

    def groups(m, tile_rows=tm):
        mp = jnp.repeat(m[:batch], tile_rows // SUBLANES, axis=0)
        return jnp.concatenate([mp, m[batch:batch + db]], axis=0)[:, None, :]

    x3 = jnp.concatenate([x_prompt.reshape(rows_p, d), x_sample.reshape(rows_s, d)], axis=0)
    x3 = x3.reshape(rows // SUBLANES, SUBLANES, d)

    n_c = batch + db
    c_all = jnp.concatenate([c_prompt, c_sample, jnp.zeros((-n_c % SUBLANES, d), F32)], axis=0)
    mods = _mods(c_all, mod_w, mod_b)

    pos = jnp.concatenate([jnp.arange(seq), past + (jnp.arange(tm) % t_len)])
    tabs = _rope_tables(pos)

    kvd = KV_HEADS * HEAD_DIM
    depth = cache_k.shape[1]
    cache_k2 = cache_k.reshape(-1, HEAD_DIM)
    cache_v2 = cache_v.reshape(-1, HEAD_DIM)
    cache_kidx_t = jnp.swapaxes(cache_kidx, 2, 3)
    n_sel_s = min(TOPK_MAX, (past + t_len) // 4)
    lp = (n_pages + 1) * PAGE

    outs = {name: [] for name in ("kp", "vp", "kip", "ks", "vs", "kis", "sp", "ss", "cp", "cs")}
    col = 0
    offs = []
    for size in (ATTN_HEADS * HEAD_DIM, kvd, kvd, IDX_HEADS * IDX_DIM, IDX_DIM, IDX_HEADS,
                 GDN_HEADS * 3 * HEAD_DIM, GDN_HEADS * HEAD_DIM, GDN_HEADS, GDN_HEADS):
        offs.append((col, col + size))
        col += size
    o_q, o_k, o_v, o_qi, o_ki, o_wi, o_gq, o_gz, o_ga, o_gb = offs

    for l in range(n_layers):
        m6 = [groups(mods[l, :, j * d:(j + 1) * d]) for j in range(6)]
        w = w_in[l]
        wa = w[:, o_q[0]:o_v[1]].astype(BF16)
        qi_w = w[:, o_qi[0]:o_qi[1]].reshape(d, IDX_HEADS, 1, IDX_DIM)
        qi_w = jnp.broadcast_to(qi_w, (d, IDX_HEADS, 2, IDX_DIM)).reshape(d, IDX_HEADS * LANES)
        ki_w = w[:, o_ki[0]:o_ki[1]]
        misc_w = jnp.concatenate([w[:, o_wi[0]:o_wi[1]], w[:, o_ga[0]:o_ga[1]], w[:, o_gb[0]:o_gb[1]],
                                  jnp.zeros((d, LANES - IDX_HEADS - 2 * GDN_HEADS), F32)], axis=1)
        wi_f = jnp.concatenate([qi_w, ki_w, ki_w, misc_w], axis=1)
        wih = wi_f.astype(BF16)
        wil = (wi_f - wih.astype(F32)).astype(BF16)
        wg = w[:, o_gq[0]:o_gz[1]].astype(BF16)

        (q_bf, k_f, v_f, k_bf, v_bf, qi3, ki_f, ki3, small, gq, gz) = _inproj(
            x3, m6[0], m6[1], norm_mix_g[l][None, :], wa, wih, wil, wg,
            q_norm_g[l][None, :], k_norm_g[l][None, :], tabs, tile_of, tab_of, tm)

        small_t = jnp.transpose(small[:, GA_LANE:GA_LANE + 2 * GDN_HEADS])
        alog_l = _lane_vec(a_log[l], GA_LANE)
        dtb_l = _lane_vec(dt_bias[l], GA_LANE)
        pad4 = jnp.zeros((SUBLANES - GDN_HEADS,), F32)
        alog_c = jnp.concatenate([a_log[l], pad4])[:, None]
        dtb_c = jnp.concatenate([dt_bias[l], pad4])[:, None]
        ng = gdn_norm_g[l][None, :]

        att_p = _dsa_prompt(qi3, small, q_bf, ki3, k_bf, v_bf, batch, seq)
        gdn_p, s_p = _gdn_prompt(gq, gz, small, small_t, conv_w[l], alog_l, dtb_l, alog_c, dtb_c, ng,
                                 batch, seq)

        scores = _sample_scores(page_table, qi3, small, ki3, cache_kidx_t, l, rows_p, t_len)
        bias = _sample_select(scores.reshape(rows_s, lp), n_sel_s)
        att_s = _sample_attend(page_table, q_bf, bias, k_bf, v_bf, cache_k2, cache_v2, depth, l, rows_p, t_len)
        cstate = jnp.pad(state_conv[:, l], ((0, 0), (SUBLANES - (CONV_W - 1), 0), (0, 0)))
        cstate = cstate.reshape(db * SUBLANES, cstate.shape[2])
        gdn_s, s_s = _gdn_sample(gq, gz, small, small_t, cstate, state_gdn, l, conv_w[l], alog_l, dtb_l,
                                 alog_c, dtb_c, ng, rows_p, db, t_len)

        att = jnp.concatenate([att_p, att_s], axis=0)
        gdn = jnp.concatenate([gdn_p, gdn_s], axis=0)
        i = l // 2
        router = None
        if l % 2 == 1:
            rw = jnp.concatenate([router_w[i], jnp.zeros((d, LANES - N_EXPERTS), F32)], axis=1)
            rb = jnp.concatenate([router_b[i], jnp.zeros((LANES - N_EXPERTS,), F32)])[None, :]
            router = (rw, rb)
        res = _outproj(x3, att, gdn, m6[2], m6[3], m6[4], norm_ffn_g[l][None, :], w_out[l].astype(BF16),
                       router, tile_of, tm)
        if l % 2 == 0:
            x3, hff = res
            x3 = _ffn(x3, hff, m6[5], ffn_w_gate[i].astype(BF16), ffn_w_up[i].astype(BF16),
                      ffn_w_down[i].astype(BF16), tile_of, tm)
        else:
            x3, hff, logits = res
            gate_moe = groups(mods[l, :, 5 * d:6 * d], tm_moe)
            x3 = _moe(x3, hff, logits, gate_moe, moe_w_gate[i].astype(BF16), moe_w_up[i].astype(BF16),
                      moe_w_down[i].astype(BF16), tile_of_for(tm_moe), tm_moe)

        outs["kp"].append(k_f[:rows_p].reshape(batch, seq, KV_HEADS, HEAD_DIM))
        outs["vp"].append(v_f[:rows_p].reshape(batch, seq, KV_HEADS, HEAD_DIM))
        outs["kip"].append(ki_f[:rows_p].reshape(batch, seq, IDX_DIM))
        outs["ks"].append(k_f[rows_p:].reshape(db, t_len, KV_HEADS, HEAD_DIM))
        outs["vs"].append(v_f[rows_p:].reshape(db, t_len, KV_HEADS, HEAD_DIM))
        outs["kis"].append(ki_f[rows_p:].reshape(db, t_len, IDX_DIM))
        outs["sp"].append(s_p)
        outs["ss"].append(s_s)
        tail = CONV_W - 1
        outs["cp"].append(jnp.stack([gq[(b + 1) * seq - tail:(b + 1) * seq] for b in range(batch)], axis=0))
        gq_s = gq[rows_p:].reshape(db, t_len, -1)
        outs["cs"].append(gq_s[:, t_len - tail:])

    x2 = x3.reshape(rows, d)
    st = lambda name: jnp.stack(outs[name], axis=1)
    return (x2[:rows_p].reshape(batch, seq, d), x2[rows_p:].reshape(db, t_len, d),
            st("kp"), st("vp"), st("kip"), st("ks"), st("vs"), st("kis"),
            st("sp"), st("ss"), st("cp"), st("cs"))
```
